```python
import math
import jax, jax.numpy as jnp
from jax import lax
import numpy as np

D_MODEL = 1024
BATCH = 32
SEQ = 2048
DEPTH = 1

MLA_HEADS = 8
MLA_Q_RANK = 256
MLA_KV_RANK = 256
MLA_NOPE = 64
MLA_ROPE = 32
MLA_V = 64
ROPE_BASE = 10000.0
DSA_HEADS = 8
DSA_HEAD_DIM = 64
IDX_HEADS = 8
IDX_DIM = 32
TOPK_MAX = 256
D_FF = int(math.ceil(8 * D_MODEL / 3 / 256)) * 256
ALPHA = (2 * DEPTH) ** 0.25
BETA = (8 * DEPTH) ** -0.25
LN_EPS = 1e-5
RMS_EPS = 1e-6
QBLOCK = 128

MLA_WIDTH = MLA_HEADS * MLA_V
DSA_WIDTH = DSA_HEADS * DSA_HEAD_DIM
SPLIT_SIZES = (
    MLA_Q_RANK,
    MLA_KV_RANK,
    MLA_ROPE,
    DSA_WIDTH,
    DSA_WIDTH,
    DSA_WIDTH,
    IDX_HEADS * IDX_DIM,
    IDX_DIM,
    IDX_HEADS,
    D_MODEL,
    D_MODEL,
)
D_IN = sum(SPLIT_SIZES)

kernel_name = "hybrid_mla_dsa_gated_deepnorm"


def split_cols(t):
    out = []
    start = 0
    for n in SPLIT_SIZES:
        out.append(t[..., start:start + n])
        start += n
    return out


def layer_norm(x, g, b):
    xf = x.astype(jnp.float32)
    mu = jnp.mean(xf, axis=-1, keepdims=True)
    var = jnp.mean(jnp.square(xf - mu), axis=-1, keepdims=True)
    y = (xf - mu) * lax.rsqrt(var + LN_EPS) * g.astype(jnp.float32) + b.astype(jnp.float32)
    return y.astype(x.dtype)


def rms_norm(x, g):
    xf = x.astype(jnp.float32)
    y = xf * lax.rsqrt(jnp.mean(jnp.square(xf), axis=-1, keepdims=True) + RMS_EPS)
    return (y * g.astype(jnp.float32)).astype(x.dtype)


def rope(x, pos):
    half = x.shape[-1] // 2
    inv_freq = ROPE_BASE ** (-jnp.arange(half, dtype=jnp.float32) / half)
    ang = pos.astype(jnp.float32)[:, :, None, None] * inv_freq
    cos, sin = jnp.cos(ang), jnp.sin(ang)
    xf = x.astype(jnp.float32)
    x1, x2 = xf[..., :half], xf[..., half:]
    return jnp.concatenate([x1 * cos - x2 * sin, x2 * cos + x1 * sin], axis=-1).astype(x.dtype)


def alibi_slopes(n):
    return jnp.asarray([2.0 ** (-8.0 * (i + 1) / n) for i in range(n)], dtype=jnp.float32)


def to_blocks(t):
    b, s = t.shape[0], t.shape[1]
    return jnp.moveaxis(t.reshape((b, s // QBLOCK, QBLOCK) + t.shape[2:]), 1, 0)


def from_blocks(t):
    t = jnp.moveaxis(t, 0, 1)
    return t.reshape((t.shape[0], t.shape[1] * t.shape[2]) + t.shape[3:])


def mla_attention(q_nope, q_rope, k_nope, k_rope, v, pos):
    scale = 1.0 / math.sqrt(MLA_NOPE + MLA_ROPE)

    def block(args):
        qn, qr, qp = args
        s = jnp.einsum('bqhd,bshd->bhqs', qn, k_nope) + jnp.einsum('bqhd,bsd->bhqs', qr, k_rope)
        s = s.astype(jnp.float32) * scale
        causal = pos[:, None, None, :] <= qp[:, None, :, None]
        s = jnp.where(causal, s, -jnp.inf)
        p = jax.nn.softmax(s, axis=-1).astype(v.dtype)
        return jnp.einsum('bhqs,bshd->bqhd', p, v)

    out = lax.map(block, (to_blocks(q_nope), to_blocks(q_rope), to_blocks(pos)))
    return from_blocks(out)


def dsa_attention(q, k, v, q_idx, k_idx, w_idx, pos):
    n_keys = k.shape[1]
    topk = min(TOPK_MAX, n_keys // 4)
    scale = 1.0 / math.sqrt(DSA_HEAD_DIM)
    idx_scale = 1.0 / math.sqrt(IDX_DIM * IDX_HEADS)
    slopes = alibi_slopes(DSA_HEADS)
    gather = jax.vmap(lambda table, ind: table[ind])

    def block(args):
        qb, qib, wb, qp = args
        dots = jnp.einsum('bqhd,bsd->bhqs', qib, k_idx).astype(jnp.float32)
        score = jnp.einsum('bqh,bhqs->bqs', wb.astype(jnp.float32), jax.nn.relu(dots)) * idx_scale
        causal = pos[:, None, :] <= qp[:, :, None]
        score = jnp.where(causal, score, -jnp.inf)
        _, sel = lax.top_k(score, topk)
        k_sel = gather(k, sel)
        v_sel = gather(v, sel)
        p_sel = gather(pos, sel)
        valid = p_sel <= qp[:, :, None]
        s = jnp.einsum('bqhd,bqkhd->bhqk', qb, k_sel).astype(jnp.float32) * scale
        dist = (qp[:, :, None] - p_sel).astype(jnp.float32)
        s = s - slopes[None, :, None, None] * dist[:, None]
        s = jnp.where(valid[:, None], s, -jnp.inf)
        p = jax.nn.softmax(s, axis=-1).astype(v.dtype)
        return jnp.einsum('bhqk,bqkhd->bqhd', p, v_sel)

    out = lax.map(block, (to_blocks(q), to_blocks(q_idx), to_blocks(w_idx), to_blocks(pos)))
    return from_blocks(out)


def setup_inputs(seed: int = 0) -> dict:
    key = jax.random.key(seed)
    ks = jax.random.split(key, 20)
    f32 = jnp.float32

    def nrm(k, shape, scale):
        return jax.random.normal(k, shape, f32) * scale

    L = DEPTH
    x = jax.random.normal(ks[0], (BATCH, SEQ, D_MODEL), f32)
    positions = jnp.broadcast_to(jnp.arange(SEQ, dtype=jnp.int32), (BATCH, SEQ))
    return {
        "x": x,
        "positions": positions,
        "w_in": nrm(ks[1], (L, D_MODEL, D_IN), D_MODEL ** -0.5),
        "mla_q_norm": 1.0 + nrm(ks[2], (L, MLA_Q_RANK), 0.01),
        "mla_kv_norm": 1.0 + nrm(ks[3], (L, MLA_KV_RANK), 0.01),
        "mla_w_uq": nrm(ks[4], (L, MLA_Q_RANK, MLA_HEADS * (MLA_NOPE + MLA_ROPE)), MLA_Q_RANK ** -0.5),
        "mla_w_ukv": nrm(ks[5], (L, MLA_KV_RANK, MLA_HEADS * (MLA_NOPE + MLA_V)), MLA_KV_RANK ** -0.5),
        "w_branch_a": nrm(ks[6], (L, MLA_WIDTH, D_MODEL), BETA * MLA_WIDTH ** -0.5),
        "w_branch_b": nrm(ks[7], (L, DSA_WIDTH, D_MODEL), BETA * DSA_WIDTH ** -0.5),
        "w_out": nrm(ks[8], (L, D_MODEL, D_MODEL), BETA * D_MODEL ** -0.5),
        "ln1_g": 1.0 + nrm(ks[9], (L, D_MODEL), 0.01),
        "ln1_b": nrm(ks[10], (L, D_MODEL), 0.01),
        "ffn_w_in": nrm(ks[11], (L, D_MODEL, 2 * D_FF), D_MODEL ** -0.5),
        "ffn_w_down": nrm(ks[12], (L, D_FF, D_MODEL), BETA * D_FF ** -0.5),
        "ln2_g": 1.0 + nrm(ks[13], (L, D_MODEL), 0.01),
        "ln2_b": nrm(ks[14], (L, D_MODEL), 0.01),
    }


def reference(x, positions, w_in, mla_q_norm, mla_kv_norm, mla_w_uq, mla_w_ukv,
              w_branch_a, w_branch_b, w_out, ln1_g, ln1_b, ffn_w_in, ffn_w_down,
              ln2_g, ln2_b):
    B, S, _ = x.shape
    h = x
    for l in range(DEPTH):
        proj = jnp.einsum('bsd,de->bse', h, w_in[l])
        c_q, c_kv, k_r, q_b, k_b, v_b, q_i, k_i, w_i, g_a, g_b = split_cols(proj)

        c_q = rms_norm(c_q, mla_q_norm[l])
        q_a = jnp.einsum('bsr,re->bse', c_q, mla_w_uq[l]).reshape(B, S, MLA_HEADS, MLA_NOPE + MLA_ROPE)
        q_nope = q_a[..., :MLA_NOPE]
        q_rope = rope(q_a[..., MLA_NOPE:], positions)
        c_kv = rms_norm(c_kv, mla_kv_norm[l])
        kv = jnp.einsum('bsr,re->bse', c_kv, mla_w_ukv[l]).reshape(B, S, MLA_HEADS, MLA_NOPE + MLA_V)
        k_nope, v_a = kv[..., :MLA_NOPE], kv[..., MLA_NOPE:]
        k_rope = rope(k_r[:, :, None, :], positions)[:, :, 0]
        o_a = mla_attention(q_nope, q_rope, k_nope, k_rope, v_a, positions).reshape(B, S, MLA_WIDTH)

        o_b = dsa_attention(
            q_b.reshape(B, S, DSA_HEADS, DSA_HEAD_DIM),
            k_b.reshape(B, S, DSA_HEADS, DSA_HEAD_DIM),
            v_b.reshape(B, S, DSA_HEADS, DSA_HEAD_DIM),
            q_i.reshape(B, S, IDX_HEADS, IDX_DIM), k_i, w_i, positions,
        ).reshape(B, S, DSA_WIDTH)

        y_a = jnp.einsum('bse,ed->bsd', o_a, w_branch_a[l])
        y_b = jnp.einsum('bse,ed->bsd', o_b, w_branch_b[l])
        mixed = jax.nn.sigmoid(g_a) * y_a + jax.nn.sigmoid(g_b) * y_b
        mix_out = jnp.einsum('bsd,de->bse', mixed, w_out[l])
        h = layer_norm(ALPHA * h + mix_out, ln1_g[l], ln1_b[l])

        gu = jnp.einsum('bsd,df->bsf', h, ffn_w_in[l])
        gate, up = gu[..., :D_FF], gu[..., D_FF:]
        f = jnp.einsum('bsf,fd->bsd', jax.nn.silu(gate) * up, ffn_w_down[l])
        h = layer_norm(ALPHA * h + f, ln2_g[l], ln2_b[l])
    return h
```

```python
import functools
import math

import jax
import jax.numpy as jnp
from jax import lax
from jax.experimental import pallas as pl
from jax.experimental.pallas import tpu as pltpu

F32 = jnp.float32
BF16 = jnp.bfloat16

D_MODEL = 1024
MLA_HEADS = 8
MLA_Q_RANK = 256
MLA_KV_RANK = 256
MLA_NOPE = 64
MLA_ROPE = 32
MLA_V = 64
ROPE_BASE = 10000.0
DSA_HEADS = 8
DSA_HEAD_DIM = 64
IDX_HEADS = 8
IDX_DIM = 32
TOPK_MAX = 256
D_FF = 2816
DEPTH = 1
ALPHA = (2 * DEPTH) ** 0.25
LN_EPS = 1e-5
RMS_EPS = 1e-6

LANES = 128
MLA_QK = MLA_NOPE + MLA_ROPE
MLA_SCALE = 1.0 / math.sqrt(MLA_QK)
DSA_SCALE = 1.0 / math.sqrt(DSA_HEAD_DIM)
IDX_SCALE = 1.0 / math.sqrt(IDX_DIM * IDX_HEADS)
ALIBI_SLOPES = tuple(2.0 ** (-8.0 * (i + 1) / DSA_HEADS) for i in range(DSA_HEADS))

NEG = -1e30
INT_MIN = -(2 ** 31)

PROJ_ROWS = 512
POST_ROWS = 256
MLA_TQ = 256
DSA_TQ = 128
DSA_CH = 256
VMEM_LIMIT = 56 * 1024 * 1024


def _const_spec(shape):
    zeros = (0,) * len(shape)
    return pl.BlockSpec(shape, lambda *_: zeros, pipeline_mode=pl.Buffered(1))


def _dot(a, b):
    return jnp.dot(a, b, preferred_element_type=F32)


def _dot_nt(a, b):
    return lax.dot_general(a, b, (((1,), (1,)), ((), ())), preferred_element_type=F32)


def _rms_norm(x, g):
    return x * lax.rsqrt(jnp.mean(x * x, axis=-1, keepdims=True) + RMS_EPS) * g


def _layer_norm(x, g, b):
    mu = jnp.mean(x, axis=-1, keepdims=True)
    xc = x - mu
    var = jnp.mean(xc * xc, axis=-1, keepdims=True)
    return xc * lax.rsqrt(var + LN_EPS) * g + b


def _sigmoid(x):
    return 1.0 / (1.0 + jnp.exp(-x))


def _softmax_step(q, k, v, bias, m, l, acc):
    s = _dot_nt(q, k) + bias
    m_new = jnp.maximum(m, jnp.max(s, axis=-1, keepdims=True))
    p = jnp.exp(s - m_new)
    alpha = jnp.exp(m - m_new)
    l_new = alpha * l + jnp.sum(p, axis=-1, keepdims=True)
    acc_new = alpha * acc + _dot(p.astype(BF16), v)
    return m_new, l_new, acc_new


def _proj_kernel(x_ref, pos_ref, freq_ref, sign_ref, wc_ref, wkr_ref, wdsa_ref, widx_ref,
                 wgate_ref, qn_ref, kvn_ref, wuqm_ref, wuqs_ref, wuk_ref, wuv_ref,
                 qa_ref, ka_ref, va_ref, qb_ref, kb_ref, vb_ref, qi_ref, ki_ref, wi_ref,
                 gate_ref):
    xb = x_ref[...].astype(BF16)
    ang = pos_ref[...].astype(F32) * freq_ref[...]
    cos_t = jnp.cos(ang)
    sin_t = jnp.sin(ang) * sign_ref[...]

    c = _dot(xb, wc_ref[...])
    cq = _rms_norm(c[:, :MLA_Q_RANK], qn_ref[...]).astype(BF16)
    ckv = _rms_norm(c[:, MLA_Q_RANK:], kvn_ref[...]).astype(BF16)

    qm = _dot(cq, wuqm_ref[...])
    qs = _dot(cq, wuqs_ref[...])
    cos_q = cos_t * MLA_SCALE
    sin_q = sin_t * MLA_SCALE
    for h in range(MLA_HEADS):
        sl = slice(h * LANES, (h + 1) * LANES)
        qa_ref[:, sl] = (qm[:, sl] * cos_q + qs[:, sl] * sin_q).astype(BF16)

    kr = _dot(xb, wkr_ref[...])
    kr_rot = kr[:, :LANES] * cos_t + kr[:, LANES:] * sin_t
    kn = _dot(ckv, wuk_ref[...])
    for h in range(MLA_HEADS):
        sl = slice(h * LANES, (h + 1) * LANES)
        ka_ref[:, sl] = (kn[:, sl] + kr_rot).astype(BF16)
    va_ref[...] = _dot(ckv, wuv_ref[...]).astype(BF16)

    d = _dot(xb, wdsa_ref[...])
    w = DSA_HEADS * DSA_HEAD_DIM
    qb_ref[...] = d[:, :w].astype(BF16)
    kb_ref[...] = d[:, w:2 * w].astype(BF16)
    vb_ref[...] = d[:, 2 * w:].astype(BF16)

    ix = _dot(xb, widx_ref[...])
    qi_ref[...] = ix[:, :2 * LANES].astype(BF16)
    ki_ref[...] = ix[:, 2 * LANES:3 * LANES].astype(BF16)
    wi_ref[...] = ix[:, 3 * LANES:]

    gate_ref[...] = _dot(xb, wgate_ref[...]).astype(BF16)


def _proj_call(x2, pos2, freq, sign, weights, rows):
    n = x2.shape[0]
    (wc, wkr, wdsa, widx, wgate, qn, kvn, wuqm, wuqs, wuk, wuv) = weights
    out_widths = (8 * LANES, 8 * LANES, 512, 512, 512, 512, 2 * LANES, LANES, LANES, 2 * D_MODEL)
    out_dtypes = (BF16, BF16, BF16, BF16, BF16, BF16, BF16, BF16, F32, BF16)
    row_spec = lambda wdt: pl.BlockSpec((rows, wdt), lambda i: (i, 0))
    consts = (freq, sign, wc, wkr, wdsa, widx, wgate, qn, kvn, wuqm, wuqs, wuk, wuv)
    return pl.pallas_call(
        _proj_kernel,
        grid=(n // rows,),
        in_specs=[row_spec(D_MODEL), row_spec(1)] + [_const_spec(a.shape) for a in consts],
        out_specs=[row_spec(wdt) for wdt in out_widths],
        out_shape=[jax.ShapeDtypeStruct((n, wdt), dt) for wdt, dt in zip(out_widths, out_dtypes)],
        compiler_params=pltpu.CompilerParams(
            dimension_semantics=("parallel",), vmem_limit_bytes=VMEM_LIMIT),
        name="proj",
    )(x2, pos2, *consts)


def _mla_kernel(q_ref, k_ref, v_ref, o_ref, *, tq):
    i = pl.program_id(1)
    row = lax.broadcasted_iota(jnp.int32, (tq, tq), 0)
    col = lax.broadcasted_iota(jnp.int32, (tq, tq), 1)
    diag_bias = jnp.where(col <= row, 0.0, NEG)
    lane = lax.broadcasted_iota(jnp.int32, (tq, LANES), 1)
    init = (jnp.full((tq, 1), NEG, F32), jnp.zeros((tq, 1), F32), jnp.zeros((tq, LANES), F32))
    diag = pl.multiple_of(i * tq, tq)

    for pair in range(MLA_HEADS // 2):
        vsl = slice(pair * LANES, (pair + 1) * LANES)
        outs = []
        for e in range(2):
            h = 2 * pair + e
            hsl = slice(h * LANES, (h + 1) * LANES)
            q = q_ref[:, hsl]

            def body(j, carry, hsl=hsl, vsl=vsl, q=q):
                ks = pl.multiple_of(j * tq, tq)
                return _softmax_step(q, k_ref[pl.ds(ks, tq), hsl], v_ref[pl.ds(ks, tq), vsl],
                                     0.0, *carry)

            carry = lax.fori_loop(0, i, body, init)
            m, l, acc = _softmax_step(q, k_ref[pl.ds(diag, tq), hsl], v_ref[pl.ds(diag, tq), vsl],
                                      diag_bias, *carry)
            outs.append(acc / l)
        o_ref[:, vsl] = jnp.where(lane < MLA_V, outs[0], outs[1]).astype(BF16)


def _mla_call(qa, ka, va, tq):
    b, s, _ = qa.shape
    return pl.pallas_call(
        functools.partial(_mla_kernel, tq=tq),
        grid=(b, s // tq),
        in_specs=[
            pl.BlockSpec((None, tq, 8 * LANES), lambda bi, i: (bi, i, 0)),
            pl.BlockSpec((None, s, 8 * LANES), lambda bi, i: (bi, 0, 0)),
            pl.BlockSpec((None, s, MLA_HEADS * MLA_V), lambda bi, i: (bi, 0, 0)),
        ],
        out_specs=pl.BlockSpec((None, tq, MLA_HEADS * MLA_V), lambda bi, i: (bi, i, 0)),
        out_shape=jax.ShapeDtypeStruct((b, s, MLA_HEADS * MLA_V), BF16),
        compiler_params=pltpu.CompilerParams(
            dimension_semantics=("parallel", "arbitrary"), vmem_limit_bytes=VMEM_LIMIT),
        name="mla",
    )(qa, ka, va)


def _dsa_kernel(qb_ref, kb_ref, vb_ref, qi_ref, ki_ref, wi_ref, posk_ref, posq_ref, o_ref,
                key_scr, bias_scr, jsel_scr, *, tq, ch, topk, seq):
    i = pl.program_id(1)
    nvis = ((i + 1) * tq + ch - 1) // ch
    row = i * tq + lax.broadcasted_iota(jnp.int32, (tq, ch), 0)
    col = lax.broadcasted_iota(jnp.int32, (tq, ch), 1)
    lane = lax.broadcasted_iota(jnp.int32, (tq, LANES), 1)
    nsub = ch // LANES

    qi = qi_ref[...]
    wi = wi_ref[...]
    qi_heads = []
    for h in range(IDX_HEADS):
        group = qi[:, (h // 4) * LANES:(h // 4 + 1) * LANES]
        lo = IDX_DIM * (h % 4)
        keep = jnp.where(lane >= lo, jnp.where(lane < lo + IDX_DIM, 1.0, 0.0), 0.0).astype(BF16)
        qi_heads.append(group * keep)
    wi_cols = [wi[:, h:h + 1] for h in range(IDX_HEADS)]

    def score_body(c, _):
        ks = pl.multiple_of(c * ch, ch)
        kic = ki_ref[pl.ds(ks, ch), :]
        sc = jnp.zeros((tq, ch), F32)
        for h in range(IDX_HEADS):
            sc = sc + wi_cols[h] * jnp.maximum(_dot_nt(qi_heads[h], kic), 0.0)
        sc = sc * IDX_SCALE
        sc = jnp.where(sc == 0.0, 0.0, sc)
        bits = pltpu.bitcast(sc, jnp.int32)
        key = bits ^ ((bits >> 31) & 0x7FFFFFFF)
        key_scr[c] = jnp.where(col + ks <= row, key, INT_MIN)
        return 0

    lax.fori_loop(0, nvis, score_body, 0)

    def count(pred):
        def body(c, acc):
            ones = pred(key_scr[c], c)
            part = ones[:, :LANES]
            for j in range(1, nsub):
                part = part + ones[:, j * LANES:(j + 1) * LANES]
            return acc + part
        acc = lax.fori_loop(0, nvis, body, jnp.zeros((tq, LANES), F32))
        return jnp.sum(acc, axis=-1, keepdims=True)

    def bis_body(it, carry):
        lo, cnt_lo, cnt_hi = carry
        cand = lo + jnp.left_shift(jnp.int32(1), 31 - it)
        cnt = count(lambda k, c: jnp.where(k >= cand, 1.0, 0.0))
        ok = cnt >= topk
        return jnp.where(ok, cand, lo), jnp.where(ok, cnt, cnt_lo), jnp.where(ok, cnt_hi, cnt)

    n_bits = jnp.where((i + 1) * tq <= topk, 0, 32)
    tau, cnt_ge, cnt_gt = lax.fori_loop(
        0, n_bits, bis_body,
        (jnp.full((tq, 1), INT_MIN, jnp.int32),
         jnp.zeros((tq, 1), F32) + (nvis * ch).astype(F32),
         jnp.zeros((tq, 1), F32)))
    has_tau = tau > INT_MIN
    need = topk - cnt_gt
    jsel_scr[...] = jnp.broadcast_to(jnp.where(has_tau, seq, -1), (tq, LANES))
    excess = jnp.where(has_tau, jnp.where(cnt_ge - cnt_gt > need, 1.0, 0.0), 0.0)

    @pl.when(jnp.max(excess) > 0.0)
    def _():
        def idx_body(it, lo):
            cand = lo + jnp.left_shift(jnp.int32(1), seq.bit_length() - 2 - it)
            cnt = count(lambda k, c: jnp.where(
                k == tau, jnp.where(col + c * ch < cand, 1.0, 0.0), 0.0))
            return jnp.where(cnt < need, cand, lo)
        last = lax.fori_loop(0, seq.bit_length() - 1, idx_body, jnp.zeros((tq, 1), jnp.int32))
        jsel_scr[...] = jnp.broadcast_to(jnp.where(has_tau, last, -1), (tq, LANES))

    jsel = jsel_scr[:, :1]

    def bias_body(c, _):
        k = key_scr[c]
        tie = jnp.where(col + c * ch <= jsel, 0.0, NEG)
        bias_scr[c] = jnp.where(k > tau, 0.0, jnp.where(k == tau, tie, NEG))
        return 0

    lax.fori_loop(0, nvis, bias_body, 0)

    pos0 = posq_ref[0:1, 0:1].astype(F32)
    init = (jnp.full((tq, 1), NEG, F32), jnp.zeros((tq, 1), F32), jnp.zeros((tq, LANES), F32))
    for pair in range(DSA_HEADS // 2):
        psl = slice(pair * LANES, (pair + 1) * LANES)
        qpair = qb_ref[:, psl]
        outs = []
        for e in range(2):
            slope = ALIBI_SLOPES[2 * pair + e]
            if e == 0:
                keep = jnp.where(lane < DSA_HEAD_DIM, 1.0, 0.0).astype(BF16)
            else:
                keep = jnp.where(lane >= DSA_HEAD_DIM, 1.0, 0.0).astype(BF16)
            q = qpair * keep

            def body(c, carry, q=q, psl=psl, slope=slope):
                ks = pl.multiple_of(c * ch, ch)
                alibi = (posk_ref[c].astype(F32) - pos0) * slope
                return _softmax_step(q, kb_ref[pl.ds(ks, ch), psl], vb_ref[pl.ds(ks, ch), psl],
                                     bias_scr[c] + alibi, *carry)

            m, l, acc = lax.fori_loop(0, nvis, body, init)
            outs.append(acc / l)
        o_ref[:, psl] = jnp.where(lane < DSA_HEAD_DIM, outs[0], outs[1]).astype(BF16)


def _dsa_call(qb, kb, vb, qi, ki, wi, positions, tq, ch):
    b, s, w = qb.shape
    nch = s // ch
    topk = min(TOPK_MAX, s // 4)
    posk = positions.reshape(b, nch, 1, ch)
    posq = positions.reshape(b, s, 1)
    full = lambda wdt: pl.BlockSpec((None, s, wdt), lambda bi, i: (bi, 0, 0))
    tile = lambda wdt: pl.BlockSpec((None, tq, wdt), lambda bi, i: (bi, i, 0))
    return pl.pallas_call(
        functools.partial(_dsa_kernel, tq=tq, ch=ch, topk=topk, seq=s),
        grid=(b, s // tq),
        in_specs=[tile(w), full(w), full(w), tile(2 * LANES), full(LANES), tile(LANES),
                  pl.BlockSpec((None, nch, 1, ch), lambda bi, i: (bi, 0, 0, 0)), tile(1)],
        out_specs=tile(w),
        out_shape=jax.ShapeDtypeStruct((b, s, w), BF16),
        scratch_shapes=[pltpu.VMEM((nch, tq, ch), jnp.int32), pltpu.VMEM((nch, tq, ch), F32),
                        pltpu.VMEM((tq, LANES), jnp.int32)],
        compiler_params=pltpu.CompilerParams(
            dimension_semantics=("parallel", "arbitrary"), vmem_limit_bytes=VMEM_LIMIT),
        name="dsa",
    )(qb, kb, vb, qi, ki, wi, posk, posq)


def _post_kernel(x_ref, oa_ref, ob_ref, gate_ref, wa_ref, wb_ref, wo_ref, g1_ref, b1_ref,
                 wfi_ref, wfd_ref, g2_ref, b2_ref, out_ref):
    ya = _dot(oa_ref[...], wa_ref[...])
    yb = _dot(ob_ref[...], wb_ref[...])
    mixed = (_sigmoid(gate_ref[:, :D_MODEL].astype(F32)) * ya
             + _sigmoid(gate_ref[:, D_MODEL:].astype(F32)) * yb)
    mix_out = _dot(mixed.astype(BF16), wo_ref[...])
    h1 = _layer_norm(ALPHA * x_ref[...] + mix_out, g1_ref[...], b1_ref[...])
    gu = _dot(h1.astype(BF16), wfi_ref[...])
    gate = gu[:, :D_FF]
    act = gate * _sigmoid(gate) * gu[:, D_FF:]
    f = _dot(act.astype(BF16), wfd_ref[...])
    out_ref[...] = _layer_norm(ALPHA * h1 + f, g2_ref[...], b2_ref[...])


def _post_call(x2, oa, ob, gates, weights, rows):
    n = x2.shape[0]
    row_spec = lambda wdt: pl.BlockSpec((rows, wdt), lambda i: (i, 0))
    return pl.pallas_call(
        _post_kernel,
        grid=(n // rows,),
        in_specs=[row_spec(D_MODEL), row_spec(oa.shape[1]), row_spec(ob.shape[1]),
                  row_spec(2 * D_MODEL)] + [_const_spec(a.shape) for a in weights],
        out_specs=row_spec(D_MODEL),
        out_shape=jax.ShapeDtypeStruct((n, D_MODEL), F32),
        compiler_params=pltpu.CompilerParams(
            dimension_semantics=("parallel",), vmem_limit_bytes=VMEM_LIMIT),
        name="post",
    )(x2, oa, ob, gates, *weights)


def _pad_cols(w, width):
    return jnp.pad(w, ((0, 0), (0, width - w.shape[1])))


def _prep_proj_weights(w_in, q_norm, kv_norm, w_uq, w_ukv):
    sizes = (MLA_Q_RANK, MLA_KV_RANK, MLA_ROPE, 512, 512, 512, IDX_HEADS * IDX_DIM, IDX_DIM,
             IDX_HEADS, D_MODEL, D_MODEL)
    parts, start = [], 0
    for n in sizes:
        parts.append(w_in[:, start:start + n])
        start += n
    w_cq, w_ckv, w_kr, w_qb, w_kb, w_vb, w_qi, w_ki, w_wi, w_ga, w_gb = parts
    half = MLA_ROPE // 2
    d = w_in.shape[0]

    def rope_group(first, second):
        return jnp.concatenate([jnp.zeros((d, MLA_NOPE), F32), first, second,
                                jnp.zeros((d, LANES - MLA_QK), F32)], axis=1)

    wc = jnp.concatenate([w_cq, w_ckv], axis=1)
    wkr = jnp.concatenate([rope_group(w_kr[:, :half], w_kr[:, half:]),
                           rope_group(w_kr[:, half:], w_kr[:, :half])], axis=1)
    wdsa = jnp.concatenate([w_qb * DSA_SCALE, w_kb, w_vb], axis=1)
    widx = jnp.concatenate([w_qi, w_ki, w_ki, w_ki, w_ki, _pad_cols(w_wi, LANES)], axis=1)
    wgate = jnp.concatenate([w_ga, w_gb], axis=1)

    r = w_uq.shape[0]
    uq = w_uq.reshape(r, MLA_HEADS, MLA_QK)
    zq = lambda n: jnp.zeros((r, MLA_HEADS, n), F32)
    wuqm = jnp.concatenate([uq, zq(LANES - MLA_QK)], axis=2).reshape(r, MLA_HEADS * LANES)
    wuqs = jnp.concatenate([zq(MLA_NOPE), uq[:, :, MLA_NOPE + half:], uq[:, :, MLA_NOPE:MLA_NOPE + half],
                            zq(LANES - MLA_QK)], axis=2).reshape(r, MLA_HEADS * LANES)
    ukv = w_ukv.reshape(r, MLA_HEADS, MLA_NOPE + MLA_V)
    wuk = jnp.concatenate([ukv[:, :, :MLA_NOPE], zq(LANES - MLA_NOPE)], axis=2).reshape(r, MLA_HEADS * LANES)
    wuv = ukv[:, :, MLA_NOPE:].reshape(r, MLA_HEADS * MLA_V)

    bf = lambda a: a.astype(BF16)
    return (bf(wc), bf(wkr), bf(wdsa), bf(widx), bf(wgate), q_norm.reshape(1, -1),
            kv_norm.reshape(1, -1), bf(wuqm), bf(wuqs), bf(wuk), bf(wuv))


def _rope_tables():
    half = MLA_ROPE // 2
    inv_freq = ROPE_BASE ** (-jnp.arange(half, dtype=F32) / half)
    z = lambda n: jnp.zeros((n,), F32)
    freq = jnp.concatenate([z(MLA_NOPE), inv_freq, inv_freq, z(LANES - MLA_QK)]).reshape(1, LANES)
    sign = jnp.concatenate([z(MLA_NOPE), -jnp.ones((half,), F32), jnp.ones((half,), F32),
                            z(LANES - MLA_QK)]).reshape(1, LANES)
    return freq, sign


def kernel(x, positions, w_in, mla_q_norm, mla_kv_norm, mla_w_uq, mla_w_ukv, w_branch_a, w_branch_b,
           w_out, ln1_g, ln1_b, ffn_w_in, ffn_w_down, ln2_g, ln2_b):
    b, s, d = x.shape
    assert w_in.shape[0] == DEPTH and d == D_MODEL
    assert s & (s - 1) == 0 and s % max(MLA_TQ, DSA_CH) == 0 and (b * s) % PROJ_ROWS == 0
    n = b * s
    h = x.reshape(n, d)
    pos2 = positions.reshape(n, 1)
    freq, sign = _rope_tables()
    for l in range(DEPTH):
        pw = _prep_proj_weights(w_in[l], mla_q_norm[l], mla_kv_norm[l], mla_w_uq[l], mla_w_ukv[l])
        qa, ka, va, qb, kb, vb, qi, ki, wi, gates = _proj_call(h, pos2, freq, sign, pw, PROJ_ROWS)
        r3 = lambda a: a.reshape(b, s, a.shape[1])
        o_a = _mla_call(r3(qa), r3(ka), r3(va), MLA_TQ)
        o_b = _dsa_call(r3(qb), r3(kb), r3(vb), r3(qi), r3(ki), r3(wi), positions, DSA_TQ, DSA_CH)
        row = lambda v: v.reshape(1, -1)
        post_w = (w_branch_a[l].astype(BF16), w_branch_b[l].astype(BF16), w_out[l].astype(BF16),
                  row(ln1_g[l]), row(ln1_b[l]), ffn_w_in[l].astype(BF16), ffn_w_down[l].astype(BF16),
                  row(ln2_g[l]), row(ln2_b[l]))
        h = _post_call(h, o_a.reshape(n, -1), o_b.reshape(n, -1), gates, post_w, POST_ROWS)
    return h.reshape(b, s, d)
```

```python
import functools
import math

import jax
import jax.numpy as jnp
from jax import lax
from jax.experimental import pallas as pl
from jax.experimental.pallas import tpu as pltpu

F32 = jnp.float32
BF16 = jnp.bfloat16

D_MODEL = 1024
MLA_HEADS = 8
MLA_Q_RANK = 256
MLA_KV_RANK = 256
MLA_NOPE = 64
MLA_ROPE = 32
MLA_V = 64
ROPE_BASE = 10000.0
DSA_HEADS = 8
DSA_HEAD_DIM = 64
IDX_HEADS = 8
IDX_DIM = 32
TOPK_MAX = 256
D_FF = 2816
DEPTH = 1
ALPHA = (2 * DEPTH) ** 0.25
LN_EPS = 1e-5
RMS_EPS = 1e-6

LANES = 128
MLA_QK = MLA_NOPE + MLA_ROPE
MLA_SCALE = 1.0 / math.sqrt(MLA_QK)
DSA_SCALE = 1.0 / math.sqrt(DSA_HEAD_DIM)
IDX_SCALE = 1.0 / math.sqrt(IDX_DIM * IDX_HEADS)
ALIBI_SLOPES = tuple(2.0 ** (-8.0 * (i + 1) / DSA_HEADS) for i in range(DSA_HEADS))

NEG = -1e30
INT_MIN = -(2 ** 31)

PROJ_ROWS = 512
POST_ROWS = 256
MLA_TQ = 256
DSA_TQ = 256
DSA_CH = 256
VMEM_LIMIT = 56 * 1024 * 1024


def _const_spec(shape):
    zeros = (0,) * len(shape)
    return pl.BlockSpec(shape, lambda *_: zeros, pipeline_mode=pl.Buffered(1))


def _dot(a, b):
    return jnp.dot(a, b, preferred_element_type=F32)


def _dot_nt(a, b):
    return lax.dot_general(a, b, (((1,), (1,)), ((), ())), preferred_element_type=F32)


def _rms_norm(x, g):
    return x * lax.rsqrt(jnp.mean(x * x, axis=-1, keepdims=True) + RMS_EPS) * g


def _layer_norm(x, g, b):
    mu = jnp.mean(x, axis=-1, keepdims=True)
    xc = x - mu
    var = jnp.mean(xc * xc, axis=-1, keepdims=True)
    return xc * lax.rsqrt(var + LN_EPS) * g + b


def _sigmoid(x):
    return 1.0 / (1.0 + jnp.exp(-x))


def _wide(x, n):
    return x if n == 1 else jnp.concatenate([x] * n, axis=1)


def _softmax_step(s, v, m_ref, l_ref, acc_ref):
    m_prev = m_ref[...]
    m_new = jnp.maximum(m_prev, jnp.max(s, axis=-1, keepdims=True))
    alpha = jnp.exp(m_prev - m_new)
    p = jnp.exp(s - _wide(m_new, s.shape[1] // LANES))
    l_ref[...] = alpha * l_ref[...] + jnp.sum(p, axis=-1, keepdims=True)
    acc_ref[...] = alpha * acc_ref[...] + _dot(p.astype(BF16), v)
    m_ref[...] = m_new


def _softmax_init(m_ref, l_ref, acc_ref):
    m_ref[...] = jnp.full(m_ref.shape, NEG, F32)
    l_ref[...] = jnp.zeros(l_ref.shape, F32)
    acc_ref[...] = jnp.zeros(acc_ref.shape, F32)


def _softmax_finish(o_ref, l_ref, acc_ref, head_dim):
    rows = o_ref.shape[0]
    lane = lax.broadcasted_iota(jnp.int32, (rows, LANES), 1)
    for pair in range(acc_ref.shape[0] // 2):
        even = acc_ref[2 * pair] / l_ref[2 * pair]
        odd = acc_ref[2 * pair + 1] / l_ref[2 * pair + 1]
        o_ref[:, pair * LANES:(pair + 1) * LANES] = jnp.where(lane < head_dim, even, odd).astype(BF16)


def _proj_kernel(x_ref, pos_ref, freq_ref, sign_ref, wc_ref, wkr_ref, wdsa_ref, widx_ref,
                 wgate_ref, qn_ref, kvn_ref, wuqm_ref, wuqs_ref, wuk_ref, wuv_ref,
                 qa_ref, ka_ref, va_ref, qb_ref, kb_ref, vb_ref, qi_ref, ki_ref, wi_ref,
                 gate_ref):
    xb = x_ref[...].astype(BF16)
    ang = pos_ref[...].astype(F32) * freq_ref[...]
    cos_t = jnp.cos(ang)
    sin_t = jnp.sin(ang) * sign_ref[...]

    c = _dot(xb, wc_ref[...])
    cq = _rms_norm(c[:, :MLA_Q_RANK], qn_ref[...]).astype(BF16)
    ckv = _rms_norm(c[:, MLA_Q_RANK:], kvn_ref[...]).astype(BF16)

    qm = _dot(cq, wuqm_ref[...])
    qs = _dot(cq, wuqs_ref[...])
    cos_q = cos_t * MLA_SCALE
    sin_q = sin_t * MLA_SCALE
    for h in range(MLA_HEADS):
        sl = slice(h * LANES, (h + 1) * LANES)
        qa_ref[:, sl] = (qm[:, sl] * cos_q + qs[:, sl] * sin_q).astype(BF16)

    kr = _dot(xb, wkr_ref[...])
    kr_rot = kr[:, :LANES] * cos_t + kr[:, LANES:] * sin_t
    kn = _dot(ckv, wuk_ref[...])
    for h in range(MLA_HEADS):
        sl = slice(h * LANES, (h + 1) * LANES)
        ka_ref[:, sl] = (kn[:, sl] + kr_rot).astype(BF16)
    va_ref[...] = _dot(ckv, wuv_ref[...]).astype(BF16)

    d = _dot(xb, wdsa_ref[...])
    w = DSA_HEADS * DSA_HEAD_DIM
    qb_ref[...] = d[:, :w].astype(BF16)
    kb_ref[...] = d[:, w:2 * w].astype(BF16)
    vb_ref[...] = d[:, 2 * w:].astype(BF16)

    ix = _dot(xb, widx_ref[...])
    qi_ref[...] = ix[:, :2 * LANES].astype(BF16)
    ki_ref[...] = ix[:, 2 * LANES:3 * LANES].astype(BF16)
    wi_ref[...] = ix[:, 3 * LANES:]

    gate_ref[...] = _dot(xb, wgate_ref[...]).astype(BF16)


def _proj_call(x2, pos2, freq, sign, weights, rows):
    n = x2.shape[0]
    (wc, wkr, wdsa, widx, wgate, qn, kvn, wuqm, wuqs, wuk, wuv) = weights
    out_widths = (8 * LANES, 8 * LANES, 512, 512, 512, 512, 2 * LANES, LANES, LANES, 2 * D_MODEL)
    out_dtypes = (BF16, BF16, BF16, BF16, BF16, BF16, BF16, BF16, F32, BF16)
    row_spec = lambda wdt: pl.BlockSpec((rows, wdt), lambda i: (i, 0))
    consts = (freq, sign, wc, wkr, wdsa, widx, wgate, qn, kvn, wuqm, wuqs, wuk, wuv)
    return pl.pallas_call(
        _proj_kernel,
        grid=(n // rows,),
        in_specs=[row_spec(D_MODEL), row_spec(1)] + [_const_spec(a.shape) for a in consts],
        out_specs=[row_spec(wdt) for wdt in out_widths],
        out_shape=[jax.ShapeDtypeStruct((n, wdt), dt) for wdt, dt in zip(out_widths, out_dtypes)],
        compiler_params=pltpu.CompilerParams(
            dimension_semantics=("parallel",), vmem_limit_bytes=VMEM_LIMIT),
        name="proj",
    )(x2, pos2, *consts)


def _mla_kernel(q_ref, k_ref, v_ref, o_ref, m_scr, l_scr, acc_scr, *, tq):
    i = pl.program_id(1)
    row = i * tq + lax.broadcasted_iota(jnp.int32, (tq, tq), 0)
    col = lax.broadcasted_iota(jnp.int32, (tq, tq), 1)
    _softmax_init(m_scr, l_scr, acc_scr)

    def body(j, _):
        ks = pl.multiple_of(j * tq, tq)
        causal = jnp.where(col + ks <= row, 0.0, NEG)
        for h in range(MLA_HEADS):
            hsl = slice(h * LANES, (h + 1) * LANES)
            vsl = slice((h // 2) * LANES, (h // 2 + 1) * LANES)
            s = _dot_nt(q_ref[:, hsl], k_ref[pl.ds(ks, tq), hsl]) + causal
            _softmax_step(s, v_ref[pl.ds(ks, tq), vsl], m_scr.at[h], l_scr.at[h], acc_scr.at[h])
        return 0

    lax.fori_loop(0, i + 1, body, 0)
    _softmax_finish(o_ref, l_scr, acc_scr, MLA_V)


def _head_state(heads, rows):
    return [pltpu.VMEM((heads, rows, LANES), F32) for _ in range(3)]


def _mla_call(qa, ka, va, tq):
    b, s, _ = qa.shape
    return pl.pallas_call(
        functools.partial(_mla_kernel, tq=tq),
        grid=(b, s // tq),
        in_specs=[
            pl.BlockSpec((None, tq, 8 * LANES), lambda bi, i: (bi, i, 0)),
            pl.BlockSpec((None, s, 8 * LANES), lambda bi, i: (bi, 0, 0)),
            pl.BlockSpec((None, s, MLA_HEADS * MLA_V), lambda bi, i: (bi, 0, 0)),
        ],
        out_specs=pl.BlockSpec((None, tq, MLA_HEADS * MLA_V), lambda bi, i: (bi, i, 0)),
        out_shape=jax.ShapeDtypeStruct((b, s, MLA_HEADS * MLA_V), BF16),
        scratch_shapes=_head_state(MLA_HEADS, tq),
        compiler_params=pltpu.CompilerParams(
            dimension_semantics=("parallel", "arbitrary"), vmem_limit_bytes=VMEM_LIMIT),
        name="mla",
    )(qa, ka, va)


def _dsa_kernel(qb_ref, kb_ref, vb_ref, qi_ref, ki_ref, wi_ref, posk_ref, o_ref,
                key_scr, bias_scr, jsel_scr, m_scr, l_scr, acc_scr, *, tq, ch, topk, seq):
    i = pl.program_id(1)
    nvis = ((i + 1) * tq + ch - 1) // ch
    row = i * tq + lax.broadcasted_iota(jnp.int32, (tq, ch), 0)
    col = lax.broadcasted_iota(jnp.int32, (tq, ch), 1)
    lane = lax.broadcasted_iota(jnp.int32, (tq, LANES), 1)
    nsub = ch // LANES

    qi = qi_ref[...]
    wi = wi_ref[...]
    qi_heads = []
    for h in range(IDX_HEADS):
        group = qi[:, (h // 4) * LANES:(h // 4 + 1) * LANES]
        lo = IDX_DIM * (h % 4)
        keep = jnp.where(lane >= lo, jnp.where(lane < lo + IDX_DIM, 1.0, 0.0), 0.0).astype(BF16)
        qi_heads.append(group * keep)
    wi_cols = [wi[:, h:h + 1] for h in range(IDX_HEADS)]

    def score_body(c, _):
        ks = pl.multiple_of(c * ch, ch)
        kic = ki_ref[pl.ds(ks, ch), :]
        sc = jnp.zeros((tq, ch), F32)
        for h in range(IDX_HEADS):
            sc = sc + wi_cols[h] * jnp.maximum(_dot_nt(qi_heads[h], kic), 0.0)
        sc = sc * IDX_SCALE
        sc = jnp.where(sc == 0.0, 0.0, sc)
        bits = pltpu.bitcast(sc, jnp.int32)
        key = bits ^ ((bits >> 31) & 0x7FFFFFFF)
        key_scr[c] = jnp.where(col + ks <= row, key, INT_MIN)
        return 0

    lax.fori_loop(0, nvis, score_body, 0)

    def count(pred):
        def body(c, acc):
            ones = pred(key_scr[c], c)
            part = ones[:, :LANES]
            for j in range(1, nsub):
                part = part + ones[:, j * LANES:(j + 1) * LANES]
            return acc + part
        acc = lax.fori_loop(0, nvis, body, jnp.zeros((tq, LANES), F32))
        return jnp.broadcast_to(jnp.sum(acc, axis=-1, keepdims=True), (tq, LANES))

    def bis_body(it, carry):
        lo, cnt_lo, cnt_hi = carry
        cand = lo + jnp.left_shift(jnp.int32(1), 31 - it)
        cand_w = _wide(cand, nsub)
        cnt = count(lambda k, c: jnp.where(k >= cand_w, 1.0, 0.0))
        ok = cnt >= topk
        return jnp.where(ok, cand, lo), jnp.where(ok, cnt, cnt_lo), jnp.where(ok, cnt_hi, cnt)

    n_bits = jnp.where((i + 1) * tq <= topk, 0, 32)
    tau, cnt_ge, cnt_gt = lax.fori_loop(
        0, n_bits, bis_body,
        (jnp.full((tq, LANES), INT_MIN, jnp.int32),
         jnp.zeros((tq, LANES), F32) + (nvis * ch).astype(F32),
         jnp.zeros((tq, LANES), F32)))
    has_tau = tau > INT_MIN
    need = topk - cnt_gt
    tau_w = _wide(tau, nsub)
    jsel_scr[...] = jnp.where(has_tau, seq, -1)
    excess = jnp.where(has_tau, jnp.where(cnt_ge - cnt_gt > need, 1.0, 0.0), 0.0)

    @pl.when(jnp.max(excess) > 0.0)
    def _():
        def idx_body(it, lo):
            cand = lo + jnp.left_shift(jnp.int32(1), seq.bit_length() - 2 - it)
            cand_w = _wide(cand, nsub)
            cnt = count(lambda k, c: jnp.where(
                k == tau_w, jnp.where(col + c * ch < cand_w, 1.0, 0.0), 0.0))
            return jnp.where(cnt < need, cand, lo)
        last = lax.fori_loop(0, seq.bit_length() - 1, idx_body, jnp.zeros((tq, LANES), jnp.int32))
        jsel_scr[...] = jnp.where(has_tau, last, -1)

    jsel_w = _wide(jsel_scr[...], nsub)

    def bias_body(c, _):
        k = key_scr[c]
        tie = jnp.where(col + c * ch <= jsel_w, 0.0, NEG)
        bias_scr[c] = jnp.where(k > tau_w, 0.0, jnp.where(k == tau_w, tie, NEG))
        return 0

    lax.fori_loop(0, nvis, bias_body, 0)

    q_heads = []
    for h in range(DSA_HEADS):
        qpair = qb_ref[:, (h // 2) * LANES:(h // 2 + 1) * LANES]
        if h % 2 == 0:
            keep = jnp.where(lane < DSA_HEAD_DIM, 1.0, 0.0).astype(BF16)
        else:
            keep = jnp.where(lane >= DSA_HEAD_DIM, 1.0, 0.0).astype(BF16)
        q_heads.append(qpair * keep)
    _softmax_init(m_scr, l_scr, acc_scr)

    def attn_body(c, _):
        ks = pl.multiple_of(c * ch, ch)
        sel = bias_scr[c]
        posf = posk_ref[c].astype(F32)
        for h in range(DSA_HEADS):
            psl = slice((h // 2) * LANES, (h // 2 + 1) * LANES)
            s = _dot_nt(q_heads[h], kb_ref[pl.ds(ks, ch), psl]) + (sel + posf * ALIBI_SLOPES[h])
            _softmax_step(s, vb_ref[pl.ds(ks, ch), psl], m_scr.at[h], l_scr.at[h], acc_scr.at[h])
        return 0

    lax.fori_loop(0, nvis, attn_body, 0)
    _softmax_finish(o_ref, l_scr, acc_scr, DSA_HEAD_DIM)


def _dsa_call(qb, kb, vb, qi, ki, wi, positions, tq, ch):
    b, s, w = qb.shape
    nch = s // ch
    topk = min(TOPK_MAX, s // 4)
    posk = positions.reshape(b, nch, 1, ch)
    full = lambda wdt: pl.BlockSpec((None, s, wdt), lambda bi, i: (bi, 0, 0))
    tile = lambda wdt: pl.BlockSpec((None, tq, wdt), lambda bi, i: (bi, i, 0))
    return pl.pallas_call(
        functools.partial(_dsa_kernel, tq=tq, ch=ch, topk=topk, seq=s),
        grid=(b, s // tq),
        in_specs=[tile(w), full(w), full(w), tile(2 * LANES), full(LANES), tile(LANES),
                  pl.BlockSpec((None, nch, 1, ch), lambda bi, i: (bi, 0, 0, 0))],
        out_specs=tile(w),
        out_shape=jax.ShapeDtypeStruct((b, s, w), BF16),
        scratch_shapes=[pltpu.VMEM((nch, tq, ch), jnp.int32), pltpu.VMEM((nch, tq, ch), F32),
                        pltpu.VMEM((tq, LANES), jnp.int32)] + _head_state(DSA_HEADS, tq),
        compiler_params=pltpu.CompilerParams(
            dimension_semantics=("parallel", "arbitrary"), vmem_limit_bytes=VMEM_LIMIT),
        name="dsa",
    )(qb, kb, vb, qi, ki, wi, posk)


def _post_kernel(x_ref, oa_ref, ob_ref, gate_ref, wa_ref, wb_ref, wo_ref, g1_ref, b1_ref,
                 wfi_ref, wfd_ref, g2_ref, b2_ref, out_ref):
    ya = _dot(oa_ref[...], wa_ref[...])
    yb = _dot(ob_ref[...], wb_ref[...])
    mixed = (_sigmoid(gate_ref[:, :D_MODEL].astype(F32)) * ya
             + _sigmoid(gate_ref[:, D_MODEL:].astype(F32)) * yb)
    mix_out = _dot(mixed.astype(BF16), wo_ref[...])
    h1 = _layer_norm(ALPHA * x_ref[...] + mix_out, g1_ref[...], b1_ref[...])
    gu = _dot(h1.astype(BF16), wfi_ref[...])
    gate = gu[:, :D_FF]
    act = gate * _sigmoid(gate) * gu[:, D_FF:]
    f = _dot(act.astype(BF16), wfd_ref[...])
    out_ref[...] = _layer_norm(ALPHA * h1 + f, g2_ref[...], b2_ref[...])


def _post_call(x2, oa, ob, gates, weights, rows):
    n = x2.shape[0]
    row_spec = lambda wdt: pl.BlockSpec((rows, wdt), lambda i: (i, 0))
    return pl.pallas_call(
        _post_kernel,
        grid=(n // rows,),
        in_specs=[row_spec(D_MODEL), row_spec(oa.shape[1]), row_spec(ob.shape[1]),
                  row_spec(2 * D_MODEL)] + [_const_spec(a.shape) for a in weights],
        out_specs=row_spec(D_MODEL),
        out_shape=jax.ShapeDtypeStruct((n, D_MODEL), F32),
        compiler_params=pltpu.CompilerParams(
            dimension_semantics=("parallel",), vmem_limit_bytes=VMEM_LIMIT),
        name="post",
    )(x2, oa, ob, gates, *weights)


def _pad_cols(w, width):
    return jnp.pad(w, ((0, 0), (0, width - w.shape[1])))


def _prep_proj_weights(w_in, q_norm, kv_norm, w_uq, w_ukv):
    sizes = (MLA_Q_RANK, MLA_KV_RANK, MLA_ROPE, 512, 512, 512, IDX_HEADS * IDX_DIM, IDX_DIM,
             IDX_HEADS, D_MODEL, D_MODEL)
    parts, start = [], 0
    for n in sizes:
        parts.append(w_in[:, start:start + n])
        start += n
    w_cq, w_ckv, w_kr, w_qb, w_kb, w_vb, w_qi, w_ki, w_wi, w_ga, w_gb = parts
    half = MLA_ROPE // 2
    d = w_in.shape[0]

    def rope_group(first, second):
        return jnp.concatenate([jnp.zeros((d, MLA_NOPE), F32), first, second,
                                jnp.zeros((d, LANES - MLA_QK), F32)], axis=1)

    wc = jnp.concatenate([w_cq, w_ckv], axis=1)
    wkr = jnp.concatenate([rope_group(w_kr[:, :half], w_kr[:, half:]),
                           rope_group(w_kr[:, half:], w_kr[:, :half])], axis=1)
    wdsa = jnp.concatenate([w_qb * DSA_SCALE, w_kb, w_vb], axis=1)
    widx = jnp.concatenate([w_qi, w_ki, w_ki, w_ki, w_ki, _pad_cols(w_wi, LANES)], axis=1)
    wgate = jnp.concatenate([w_ga, w_gb], axis=1)

    r = w_uq.shape[0]
    uq = w_uq.reshape(r, MLA_HEADS, MLA_QK)
    zq = lambda n: jnp.zeros((r, MLA_HEADS, n), F32)
    wuqm = jnp.concatenate([uq, zq(LANES - MLA_QK)], axis=2).reshape(r, MLA_HEADS * LANES)
    wuqs = jnp.concatenate([zq(MLA_NOPE), uq[:, :, MLA_NOPE + half:], uq[:, :, MLA_NOPE:MLA_NOPE + half],
                            zq(LANES - MLA_QK)], axis=2).reshape(r, MLA_HEADS * LANES)
    ukv = w_ukv.reshape(r, MLA_HEADS, MLA_NOPE + MLA_V)
    wuk = jnp.concatenate([ukv[:, :, :MLA_NOPE], zq(LANES - MLA_NOPE)], axis=2).reshape(r, MLA_HEADS * LANES)
    wuv = ukv[:, :, MLA_NOPE:].reshape(r, MLA_HEADS * MLA_V)

    bf = lambda a: a.astype(BF16)
    return (bf(wc), bf(wkr), bf(wdsa), bf(widx), bf(wgate), q_norm.reshape(1, -1),
            kv_norm.reshape(1, -1), bf(wuqm), bf(wuqs), bf(wuk), bf(wuv))


def _rope_tables():
    half = MLA_ROPE // 2
    inv_freq = ROPE_BASE ** (-jnp.arange(half, dtype=F32) / half)
    z = lambda n: jnp.zeros((n,), F32)
    freq = jnp.concatenate([z(MLA_NOPE), inv_freq, inv_freq, z(LANES - MLA_QK)]).reshape(1, LANES)
    sign = jnp.concatenate([z(MLA_NOPE), -jnp.ones((half,), F32), jnp.ones((half,), F32),
                            z(LANES - MLA_QK)]).reshape(1, LANES)
    return freq, sign


def kernel(x, positions, w_in, mla_q_norm, mla_kv_norm, mla_w_uq, mla_w_ukv, w_branch_a, w_branch_b,
           w_out, ln1_g, ln1_b, ffn_w_in, ffn_w_down, ln2_g, ln2_b):
    b, s, d = x.shape
    assert w_in.shape[0] == DEPTH and d == D_MODEL
    assert s & (s - 1) == 0 and s % max(MLA_TQ, DSA_CH) == 0 and (b * s) % PROJ_ROWS == 0
    n = b * s
    h = x.reshape(n, d)
    pos2 = positions.reshape(n, 1)
    freq, sign = _rope_tables()
    for l in range(DEPTH):
        pw = _prep_proj_weights(w_in[l], mla_q_norm[l], mla_kv_norm[l], mla_w_uq[l], mla_w_ukv[l])
        qa, ka, va, qb, kb, vb, qi, ki, wi, gates = _proj_call(h, pos2, freq, sign, pw, PROJ_ROWS)
        r3 = lambda a: a.reshape(b, s, a.shape[1])
        o_a = _mla_call(r3(qa), r3(ka), r3(va), MLA_TQ)
        o_b = _dsa_call(r3(qb), r3(kb), r3(vb), r3(qi), r3(ki), r3(wi), positions, DSA_TQ, DSA_CH)
        row = lambda v: v.reshape(1, -1)
        post_w = (w_branch_a[l].astype(BF16), w_branch_b[l].astype(BF16), w_out[l].astype(BF16),
                  row(ln1_g[l]), row(ln1_b[l]), ffn_w_in[l].astype(BF16), ffn_w_down[l].astype(BF16),
                  row(ln2_g[l]), row(ln2_b[l]))
        h = _post_call(h, o_a.reshape(n, -1), o_b.reshape(n, -1), gates, post_w, POST_ROWS)
    return h.reshape(b, s, d)
```

```python
import functools
import math

import jax
import jax.numpy as jnp
from jax import lax
from jax.experimental import pallas as pl
from jax.experimental.pallas import tpu as pltpu

F32 = jnp.float32
BF16 = jnp.bfloat16

D_MODEL = 1024
MLA_HEADS = 8
MLA_Q_RANK = 256
MLA_KV_RANK = 256
MLA_NOPE = 64
MLA_ROPE = 32
MLA_V = 64
ROPE_BASE = 10000.0
DSA_HEADS = 8
DSA_HEAD_DIM = 64
IDX_HEADS = 8
IDX_DIM = 32
TOPK_MAX = 256
D_FF = 2816
DEPTH = 1
ALPHA = (2 * DEPTH) ** 0.25
LN_EPS = 1e-5
RMS_EPS = 1e-6

LANES = 128
SUBLANES = 8
MLA_QK = MLA_NOPE + MLA_ROPE
MLA_SCALE = 1.0 / math.sqrt(MLA_QK)
DSA_SCALE = 1.0 / math.sqrt(DSA_HEAD_DIM)
IDX_SCALE = 1.0 / math.sqrt(IDX_DIM * IDX_HEADS)
ALIBI_SLOPES = tuple(2.0 ** (-8.0 * (i + 1) / DSA_HEADS) for i in range(DSA_HEADS))
POS_SPLIT = 64

NEG = -1e30
INT_MIN = -(2 ** 31)

PROJ_ROWS = 512
POST_ROWS = 256
ATT_TQ = 256
ATT_CH = 256
VMEM_LIMIT = 56 * 1024 * 1024


def _const_spec(shape):
    zeros = (0,) * len(shape)
    return pl.BlockSpec(shape, lambda *_: zeros, pipeline_mode=pl.Buffered(1))


def _dot(a, b):
    return jnp.dot(a, b, preferred_element_type=F32)


def _dot_nt(a, b):
    return lax.dot_general(a, b, (((1,), (1,)), ((), ())), preferred_element_type=F32)


def _rms_norm(x, g):
    return x * lax.rsqrt(jnp.mean(x * x, axis=-1, keepdims=True) + RMS_EPS) * g


def _layer_norm(x, g, b):
    mu = jnp.mean(x, axis=-1, keepdims=True)
    xc = x - mu
    var = jnp.mean(xc * xc, axis=-1, keepdims=True)
    return xc * lax.rsqrt(var + LN_EPS) * g + b


def _sigmoid(x):
    return 1.0 / (1.0 + jnp.exp(-x))


def _fold_rows(x, op, ways=4):
    parts = [x[j * SUBLANES:(j + 1) * SUBLANES] for j in range(x.shape[0] // SUBLANES)]
    accs = parts[:ways]
    for j in range(ways, len(parts)):
        accs[j % ways] = op(accs[j % ways], parts[j])
    while len(accs) > 1:
        accs = [op(accs[k], accs[k + 1]) for k in range(0, len(accs) - 1, 2)] + accs[len(accs) & ~1:]
    return accs[0]


def _attn_init(m_ref, l_ref, acc_ref):
    m_ref[...] = jnp.full(m_ref.shape, NEG, F32)
    l_ref[...] = jnp.zeros(l_ref.shape, F32)
    acc_ref[...] = jnp.zeros(acc_ref.shape, F32)


def _logits_step(s_t, s_ref, m_ref):
    s_ref[...] = s_t
    m_ref[...] = jnp.maximum(m_ref[...], _fold_rows(s_t, jnp.maximum))


def _values_step(s_ref, v_t, m_ref, l_ref, acc_ref):
    p = jnp.exp(s_ref[...] - jnp.max(m_ref[...], axis=0, keepdims=True))
    l_ref[...] = l_ref[...] + _fold_rows(p, jnp.add)
    acc_ref[...] = acc_ref[...] + _dot(v_t, p.astype(BF16))


def _attn_finish(o_ref, l_ref, acc_ref):
    norm = lambda h: acc_ref[h] / jnp.sum(l_ref[h], axis=0, keepdims=True)
    for pair in range(acc_ref.shape[0] // 2):
        o_t = jnp.concatenate([norm(2 * pair), norm(2 * pair + 1)], axis=0)
        o_ref[:, pair * LANES:(pair + 1) * LANES] = o_t.T.astype(BF16)


def _attn_scratch(heads, dv, tq, nch, ch):
    return [pltpu.VMEM((heads, nch, ch, tq), F32), pltpu.VMEM((heads, SUBLANES, tq), F32),
            pltpu.VMEM((heads, SUBLANES, tq), F32), pltpu.VMEM((heads, dv, tq), F32)]


def _proj_kernel(x_ref, pos_ref, freq_ref, sign_ref, wc_ref, wkr_ref, wdsa_ref, wvbt_ref, widx_ref,
                 wgate_ref, qn_ref, kvn_ref, wuqm_ref, wuqs_ref, wuk_ref, wuvt_ref,
                 qa_ref, ka_ref, vat_ref, qb_ref, kb_ref, vbt_ref, qi_ref, ki_ref, wi_ref,
                 kpos_ref, gate_ref, *, ch):
    rows = x_ref.shape[0]
    xb = x_ref[...].astype(BF16)
    pos = pos_ref[...]
    ang = pos.astype(F32) * freq_ref[...]
    cos_t = jnp.cos(ang)
    sin_t = jnp.sin(ang) * sign_ref[...]

    c = _dot(xb, wc_ref[...])
    cq = _rms_norm(c[:, :MLA_Q_RANK], qn_ref[...]).astype(BF16)
    ckv = _rms_norm(c[:, MLA_Q_RANK:], kvn_ref[...]).astype(BF16)

    qm = _dot(cq, wuqm_ref[...])
    qs = _dot(cq, wuqs_ref[...])
    cos_q = cos_t * MLA_SCALE
    sin_q = sin_t * MLA_SCALE
    for h in range(MLA_HEADS):
        sl = slice(h * LANES, (h + 1) * LANES)
        qa_ref[:, sl] = (qm[:, sl] * cos_q + qs[:, sl] * sin_q).astype(BF16)

    kr = _dot(xb, wkr_ref[...])
    kr_rot = kr[:, :LANES] * cos_t + kr[:, LANES:] * sin_t
    kn = _dot(ckv, wuk_ref[...])
    for h in range(MLA_HEADS):
        sl = slice(h * LANES, (h + 1) * LANES)
        ka_ref[:, sl] = (kn[:, sl] + kr_rot).astype(BF16)

    d = _dot(xb, wdsa_ref[...])
    w = DSA_HEADS * DSA_HEAD_DIM
    qb_ref[...] = d[:, :w].astype(BF16)
    kb_ref[...] = d[:, w:].astype(BF16)

    va_t = _dot_nt(wuvt_ref[...], ckv).astype(BF16)
    vb_t = _dot_nt(wvbt_ref[...], xb).astype(BF16)
    for k in range(rows // ch):
        vat_ref[k] = va_t[:, k * ch:(k + 1) * ch]
        vbt_ref[k] = vb_t[:, k * ch:(k + 1) * ch]

    ix = _dot(xb, widx_ref[...])
    qi_ref[...] = ix[:, :2 * LANES].astype(BF16)
    ki_ref[...] = ix[:, 2 * LANES:3 * LANES].astype(BF16)
    wi_ref[...] = ix[:, 3 * LANES:]

    lane = lax.broadcasted_iota(jnp.int32, (rows, LANES), 1)
    hi = (pos >> (POS_SPLIT.bit_length() - 1)).astype(F32)
    lo = (pos & (POS_SPLIT - 1)).astype(F32)
    kpos_ref[...] = jnp.where(lane == 0, hi, jnp.where(lane == 1, lo, 0.0)).astype(BF16)

    gate_ref[...] = _dot(xb, wgate_ref[...]).astype(BF16)


def _proj_call(x2, pos2, freq, sign, weights, rows, ch):
    n = x2.shape[0]
    (wc, wkr, wdsa, wvbt, widx, wgate, qn, kvn, wuqm, wuqs, wuk, wuvt) = weights
    row_spec = lambda wdt: pl.BlockSpec((rows, wdt), lambda i: (i, 0))
    row_out = lambda wdt, dt: (row_spec(wdt), jax.ShapeDtypeStruct((n, wdt), dt))
    t_out = lambda wdt: (pl.BlockSpec((rows // ch, wdt, ch), lambda i: (i, 0, 0)),
                         jax.ShapeDtypeStruct((n // ch, wdt, ch), BF16))
    outs = [row_out(8 * LANES, BF16), row_out(8 * LANES, BF16), t_out(MLA_HEADS * MLA_V),
            row_out(512, BF16), row_out(512, BF16), t_out(DSA_HEADS * DSA_HEAD_DIM),
            row_out(2 * LANES, BF16), row_out(LANES, BF16), row_out(LANES, F32),
            row_out(LANES, BF16), row_out(2 * D_MODEL, BF16)]
    consts = (freq, sign, wc, wkr, wdsa, wvbt, widx, wgate, qn, kvn, wuqm, wuqs, wuk, wuvt)
    return pl.pallas_call(
        functools.partial(_proj_kernel, ch=ch),
        grid=(n // rows,),
        in_specs=[row_spec(D_MODEL), row_spec(1)] + [_const_spec(a.shape) for a in consts],
        out_specs=[o[0] for o in outs],
        out_shape=[o[1] for o in outs],
        compiler_params=pltpu.CompilerParams(
            dimension_semantics=("parallel",), vmem_limit_bytes=VMEM_LIMIT),
        name="proj",
    )(x2, pos2, *consts)


def _mla_kernel(q_ref, k_ref, vt_ref, o_ref, s_scr, m_scr, l_scr, acc_scr, *, tq, ch):
    i = pl.program_id(1)
    nvis = ((i + 1) * tq + ch - 1) // ch
    key = lax.broadcasted_iota(jnp.int32, (ch, tq), 0)
    query = i * tq + lax.broadcasted_iota(jnp.int32, (ch, tq), 1)
    _attn_init(m_scr, l_scr, acc_scr)

    def logits_body(c, _):
        ks = pl.multiple_of(c * ch, ch)
        causal = jnp.where(key + ks <= query, 0.0, NEG)
        for h in range(MLA_HEADS):
            hsl = slice(h * LANES, (h + 1) * LANES)
            s_t = _dot_nt(k_ref[pl.ds(ks, ch), hsl], q_ref[:, hsl]) + causal
            _logits_step(s_t, s_scr.at[h, c], m_scr.at[h])
        return 0

    def values_body(c, _):
        for h in range(MLA_HEADS):
            _values_step(s_scr.at[h, c], vt_ref[c, h * MLA_V:(h + 1) * MLA_V, :],
                         m_scr.at[h], l_scr.at[h], acc_scr.at[h])
        return 0

    lax.fori_loop(0, nvis, logits_body, 0)
    lax.fori_loop(0, nvis, values_body, 0)
    _attn_finish(o_ref, l_scr, acc_scr)


def _mla_call(qa, ka, vat, tq, ch):
    b, s, _ = qa.shape
    wv = MLA_HEADS * MLA_V
    return pl.pallas_call(
        functools.partial(_mla_kernel, tq=tq, ch=ch),
        grid=(b, s // tq),
        in_specs=[
            pl.BlockSpec((None, tq, 8 * LANES), lambda bi, i: (bi, i, 0)),
            pl.BlockSpec((None, s, 8 * LANES), lambda bi, i: (bi, 0, 0)),
            pl.BlockSpec((None, s // ch, wv, ch), lambda bi, i: (bi, 0, 0, 0)),
        ],
        out_specs=pl.BlockSpec((None, tq, wv), lambda bi, i: (bi, i, 0)),
        out_shape=jax.ShapeDtypeStruct((b, s, wv), BF16),
        scratch_shapes=_attn_scratch(MLA_HEADS, MLA_V, tq, s // ch, ch),
        compiler_params=pltpu.CompilerParams(
            dimension_semantics=("parallel", "arbitrary"), vmem_limit_bytes=VMEM_LIMIT),
        name="mla",
    )(qa, ka, vat)


def _dsa_kernel(qb_ref, kb_ref, vbt_ref, qi_ref, ki_ref, wi_ref, kpos_ref, o_ref,
                key_scr, bias_scr, jsel_scr, s_scr, m_scr, l_scr, acc_scr, *, tq, ch, topk, seq):
    i = pl.program_id(1)
    nvis = ((i + 1) * tq + ch - 1) // ch
    key_idx = lax.broadcasted_iota(jnp.int32, (ch, tq), 0)
    query = i * tq + lax.broadcasted_iota(jnp.int32, (ch, tq), 1)
    lane = lax.broadcasted_iota(jnp.int32, (tq, LANES), 1)

    qi = qi_ref[...]
    wi_t = wi_ref[...].T
    qi_heads = []
    for h in range(IDX_HEADS):
        group = qi[:, (h // 4) * LANES:(h // 4 + 1) * LANES]
        lo = IDX_DIM * (h % 4)
        keep = jnp.where(lane >= lo, jnp.where(lane < lo + IDX_DIM, 1.0, 0.0), 0.0).astype(BF16)
        qi_heads.append(group * keep)

    def score_body(c, _):
        ks = pl.multiple_of(c * ch, ch)
        kic = ki_ref[pl.ds(ks, ch), :]
        sc = jnp.zeros((ch, tq), F32)
        for h in range(IDX_HEADS):
            sc = sc + wi_t[h:h + 1, :] * jnp.maximum(_dot_nt(kic, qi_heads[h]), 0.0)
        sc = sc * IDX_SCALE
        sc = jnp.where(sc == 0.0, 0.0, sc)
        bits = pltpu.bitcast(sc, jnp.int32)
        key = bits ^ ((bits >> 31) & 0x7FFFFFFF)
        key_scr[c] = jnp.where(key_idx + ks <= query, key, INT_MIN)
        return 0

    lax.fori_loop(0, nvis, score_body, 0)

    def count(pred):
        def body(c, acc):
            return acc + _fold_rows(pred(key_scr[c], c), jnp.add)
        acc = lax.fori_loop(0, nvis, body, jnp.zeros((SUBLANES, tq), F32))
        return jnp.sum(acc, axis=0, keepdims=True)

    def bis_body(it, carry):
        lo, cnt_lo, cnt_hi = carry
        cand = lo + jnp.left_shift(jnp.int32(1), 31 - it)
        cnt = count(lambda k, c: jnp.where(k >= cand, 1.0, 0.0))
        ok = cnt >= topk
        return jnp.where(ok, cand, lo), jnp.where(ok, cnt, cnt_lo), jnp.where(ok, cnt_hi, cnt)

    n_bits = jnp.where((i + 1) * tq <= topk, 0, 32)
    tau, cnt_ge, cnt_gt = lax.fori_loop(
        0, n_bits, bis_body,
        (jnp.full((1, tq), INT_MIN, jnp.int32),
         jnp.zeros((1, tq), F32) + (nvis * ch).astype(F32),
         jnp.zeros((1, tq), F32)))
    has_tau = tau > INT_MIN
    need = topk - cnt_gt
    jsel_scr[...] = jnp.where(has_tau, seq, -1)
    excess = jnp.where(has_tau, jnp.where(cnt_ge - cnt_gt > need, 1.0, 0.0), 0.0)

    @pl.when(jnp.max(excess) > 0.0)
    def _():
        def idx_body(it, lo):
            cand = lo + jnp.left_shift(jnp.int32(1), seq.bit_length() - 2 - it)
            cnt = count(lambda k, c: jnp.where(
                k == tau, jnp.where(key_idx + c * ch < cand, 1.0, 0.0), 0.0))
            return jnp.where(cnt < need, cand, lo)
        last = lax.fori_loop(0, seq.bit_length() - 1, idx_body, jnp.zeros((1, tq), jnp.int32))
        jsel_scr[...] = jnp.where(has_tau, last, -1)

    jsel = jsel_scr[...]

    def bias_body(c, _):
        k = key_scr[c]
        tie = jnp.where(key_idx + c * ch <= jsel, 0.0, NEG)
        bias_scr[c] = jnp.where(k > tau, 0.0, jnp.where(k == tau, tie, NEG))
        return 0

    lax.fori_loop(0, nvis, bias_body, 0)

    q_heads = []
    for h in range(DSA_HEADS):
        qpair = qb_ref[:, (h // 2) * LANES:(h // 2 + 1) * LANES]
        if h % 2 == 0:
            keep = jnp.where(lane < DSA_HEAD_DIM, 1.0, 0.0).astype(BF16)
        else:
            keep = jnp.where(lane >= DSA_HEAD_DIM, 1.0, 0.0).astype(BF16)
        slope = ALIBI_SLOPES[h]
        q_pos = jnp.where(lane == 0, POS_SPLIT * slope, jnp.where(lane == 1, slope, 0.0)).astype(BF16)
        q_heads.append(jnp.concatenate([qpair * keep, q_pos], axis=1))
    _attn_init(m_scr, l_scr, acc_scr)

    def logits_body(c, _):
        ks = pl.multiple_of(c * ch, ch)
        sel = bias_scr[c]
        kpos = kpos_ref[pl.ds(ks, ch), :]
        for h in range(DSA_HEADS):
            psl = slice((h // 2) * LANES, (h // 2 + 1) * LANES)
            k_aug = jnp.concatenate([kb_ref[pl.ds(ks, ch), psl], kpos], axis=1)
            _logits_step(_dot_nt(k_aug, q_heads[h]) + sel, s_scr.at[h, c], m_scr.at[h])
        return 0

    def values_body(c, _):
        for h in range(DSA_HEADS):
            _values_step(s_scr.at[h, c], vbt_ref[c, h * DSA_HEAD_DIM:(h + 1) * DSA_HEAD_DIM, :],
                         m_scr.at[h], l_scr.at[h], acc_scr.at[h])
        return 0

    lax.fori_loop(0, nvis, logits_body, 0)
    lax.fori_loop(0, nvis, values_body, 0)
    _attn_finish(o_ref, l_scr, acc_scr)


def _dsa_call(qb, kb, vbt, qi, ki, wi, kpos, tq, ch):
    b, s, w = qb.shape
    nch = s // ch
    topk = min(TOPK_MAX, s // 4)
    full = lambda wdt: pl.BlockSpec((None, s, wdt), lambda bi, i: (bi, 0, 0))
    tile = lambda wdt: pl.BlockSpec((None, tq, wdt), lambda bi, i: (bi, i, 0))
    return pl.pallas_call(
        functools.partial(_dsa_kernel, tq=tq, ch=ch, topk=topk, seq=s),
        grid=(b, s // tq),
        in_specs=[tile(w), full(w), pl.BlockSpec((None, nch, w, ch), lambda bi, i: (bi, 0, 0, 0)),
                  tile(2 * LANES), full(LANES), tile(LANES), full(LANES)],
        out_specs=tile(w),
        out_shape=jax.ShapeDtypeStruct((b, s, w), BF16),
        scratch_shapes=[pltpu.VMEM((nch, ch, tq), jnp.int32), pltpu.VMEM((nch, ch, tq), F32),
                        pltpu.VMEM((1, tq), jnp.int32)]
        + _attn_scratch(DSA_HEADS, DSA_HEAD_DIM, tq, nch, ch),
        compiler_params=pltpu.CompilerParams(
            dimension_semantics=("parallel", "arbitrary"), vmem_limit_bytes=VMEM_LIMIT),
        name="dsa",
    )(qb, kb, vbt, qi, ki, wi, kpos)


def _post_kernel(x_ref, oa_ref, ob_ref, gate_ref, wa_ref, wb_ref, wo_ref, g1_ref, b1_ref,
                 wfi_ref, wfd_ref, g2_ref, b2_ref, out_ref):
    ya = _dot(oa_ref[...], wa_ref[...])
    yb = _dot(ob_ref[...], wb_ref[...])
    mixed = (_sigmoid(gate_ref[:, :D_MODEL].astype(F32)) * ya
             + _sigmoid(gate_ref[:, D_MODEL:].astype(F32)) * yb)
    mix_out = _dot(mixed.astype(BF16), wo_ref[...])
    h1 = _layer_norm(ALPHA * x_ref[...] + mix_out, g1_ref[...], b1_ref[...])
    gu = _dot(h1.astype(BF16), wfi_ref[...])
    gate = gu[:, :D_FF]
    act = gate * _sigmoid(gate) * gu[:, D_FF:]
    f = _dot(act.astype(BF16), wfd_ref[...])
    out_ref[...] = _layer_norm(ALPHA * h1 + f, g2_ref[...], b2_ref[...])


def _post_call(x2, oa, ob, gates, weights, rows):
    n = x2.shape[0]
    row_spec = lambda wdt: pl.BlockSpec((rows, wdt), lambda i: (i, 0))
    return pl.pallas_call(
        _post_kernel,
        grid=(n // rows,),
        in_specs=[row_spec(D_MODEL), row_spec(oa.shape[1]), row_spec(ob.shape[1]),
                  row_spec(2 * D_MODEL)] + [_const_spec(a.shape) for a in weights],
        out_specs=row_spec(D_MODEL),
        out_shape=jax.ShapeDtypeStruct((n, D_MODEL), F32),
        compiler_params=pltpu.CompilerParams(
            dimension_semantics=("parallel",), vmem_limit_bytes=VMEM_LIMIT),
        name="post",
    )(x2, oa, ob, gates, *weights)


def _pad_cols(w, width):
    return jnp.pad(w, ((0, 0), (0, width - w.shape[1])))


def _prep_proj_weights(w_in, q_norm, kv_norm, w_uq, w_ukv):
    sizes = (MLA_Q_RANK, MLA_KV_RANK, MLA_ROPE, 512, 512, 512, IDX_HEADS * IDX_DIM, IDX_DIM,
             IDX_HEADS, D_MODEL, D_MODEL)
    parts, start = [], 0
    for n in sizes:
        parts.append(w_in[:, start:start + n])
        start += n
    w_cq, w_ckv, w_kr, w_qb, w_kb, w_vb, w_qi, w_ki, w_wi, w_ga, w_gb = parts
    half = MLA_ROPE // 2
    d = w_in.shape[0]

    def rope_group(first, second):
        return jnp.concatenate([jnp.zeros((d, MLA_NOPE), F32), first, second,
                                jnp.zeros((d, LANES - MLA_QK), F32)], axis=1)

    wc = jnp.concatenate([w_cq, w_ckv], axis=1)
    wkr = jnp.concatenate([rope_group(w_kr[:, :half], w_kr[:, half:]),
                           rope_group(w_kr[:, half:], w_kr[:, :half])], axis=1)
    wdsa = jnp.concatenate([w_qb * DSA_SCALE, w_kb], axis=1)
    widx = jnp.concatenate([w_qi, w_ki, w_ki, w_ki, w_ki, _pad_cols(w_wi, LANES)], axis=1)
    wgate = jnp.concatenate([w_ga, w_gb], axis=1)

    r = w_uq.shape[0]
    uq = w_uq.reshape(r, MLA_HEADS, MLA_QK)
    zq = lambda n: jnp.zeros((r, MLA_HEADS, n), F32)
    wuqm = jnp.concatenate([uq, zq(LANES - MLA_QK)], axis=2).reshape(r, MLA_HEADS * LANES)
    wuqs = jnp.concatenate([zq(MLA_NOPE), uq[:, :, MLA_NOPE + half:], uq[:, :, MLA_NOPE:MLA_NOPE + half],
                            zq(LANES - MLA_QK)], axis=2).reshape(r, MLA_HEADS * LANES)
    ukv = w_ukv.reshape(r, MLA_HEADS, MLA_NOPE + MLA_V)
    wuk = jnp.concatenate([ukv[:, :, :MLA_NOPE], zq(LANES - MLA_NOPE)], axis=2).reshape(r, MLA_HEADS * LANES)
    wuvt = ukv[:, :, MLA_NOPE:].reshape(r, MLA_HEADS * MLA_V).T

    bf = lambda a: a.astype(BF16)
    return (bf(wc), bf(wkr), bf(wdsa), bf(w_vb.T), bf(widx), bf(wgate), q_norm.reshape(1, -1),
            kv_norm.reshape(1, -1), bf(wuqm), bf(wuqs), bf(wuk), bf(wuvt))


def _rope_tables():
    half = MLA_ROPE // 2
    inv_freq = ROPE_BASE ** (-jnp.arange(half, dtype=F32) / half)
    z = lambda n: jnp.zeros((n,), F32)
    freq = jnp.concatenate([z(MLA_NOPE), inv_freq, inv_freq, z(LANES - MLA_QK)]).reshape(1, LANES)
    sign = jnp.concatenate([z(MLA_NOPE), -jnp.ones((half,), F32), jnp.ones((half,), F32),
                            z(LANES - MLA_QK)]).reshape(1, LANES)
    return freq, sign


def kernel(x, positions, w_in, mla_q_norm, mla_kv_norm, mla_w_uq, mla_w_ukv, w_branch_a, w_branch_b,
           w_out, ln1_g, ln1_b, ffn_w_in, ffn_w_down, ln2_g, ln2_b):
    b, s, d = x.shape
    assert w_in.shape[0] == DEPTH and d == D_MODEL
    assert s & (s - 1) == 0 and s % ATT_TQ == 0 and s % ATT_CH == 0 and PROJ_ROWS % ATT_CH == 0
    assert (b * s) % PROJ_ROWS == 0 and s <= POS_SPLIT * 256
    n = b * s
    h = x.reshape(n, d)
    pos2 = positions.reshape(n, 1)
    freq, sign = _rope_tables()
    for l in range(DEPTH):
        pw = _prep_proj_weights(w_in[l], mla_q_norm[l], mla_kv_norm[l], mla_w_uq[l], mla_w_ukv[l])
        qa, ka, vat, qb, kb, vbt, qi, ki, wi, kpos, gates = _proj_call(
            h, pos2, freq, sign, pw, PROJ_ROWS, ATT_CH)
        r3 = lambda a: a.reshape(b, s, a.shape[1])
        t4 = lambda a: a.reshape(b, s // ATT_CH, a.shape[1], ATT_CH)
        o_a = _mla_call(r3(qa), r3(ka), t4(vat), ATT_TQ, ATT_CH)
        o_b = _dsa_call(r3(qb), r3(kb), t4(vbt), r3(qi), r3(ki), r3(wi), r3(kpos), ATT_TQ, ATT_CH)
        row = lambda v: v.reshape(1, -1)
        post_w = (w_branch_a[l].astype(BF16), w_branch_b[l].astype(BF16), w_out[l].astype(BF16),
                  row(ln1_g[l]), row(ln1_b[l]), ffn_w_in[l].astype(BF16), ffn_w_down[l].astype(BF16),
                  row(ln2_g[l]), row(ln2_b[l]))
        h = _post_call(h, o_a.reshape(n, -1), o_b.reshape(n, -1), gates, post_w, POST_ROWS)
    return h.reshape(b, s, d)
```

```python
import functools
import math
import struct

import jax
import jax.numpy as jnp
from jax import lax
from jax.experimental import pallas as pl
from jax.experimental.pallas import tpu as pltpu

F32 = jnp.float32
BF16 = jnp.bfloat16

D_MODEL = 1024
MLA_HEADS = 8
MLA_Q_RANK = 256
MLA_KV_RANK = 256
MLA_NOPE = 64
MLA_ROPE = 32
MLA_V = 64
ROPE_BASE = 10000.0
DSA_HEADS = 8
DSA_HEAD_DIM = 64
IDX_HEADS = 8
IDX_DIM = 32
TOPK_MAX = 256
D_FF = 2816
DEPTH = 1
ALPHA = (2 * DEPTH) ** 0.25
LN_EPS = 1e-5
RMS_EPS = 1e-6

LANES = 128
SUBLANES = 8
MLA_QK = MLA_NOPE + MLA_ROPE
MLA_SCALE = 1.0 / math.sqrt(MLA_QK)
DSA_SCALE = 1.0 / math.sqrt(DSA_HEAD_DIM)
IDX_SCALE = 1.0 / math.sqrt(IDX_DIM * IDX_HEADS)
ALIBI_SLOPES = tuple(2.0 ** (-8.0 * (i + 1) / DSA_HEADS) for i in range(DSA_HEADS))
POS_SPLIT = 64
POS_TERMS = 3
LOG2E = math.log2(math.e)

NEG = -1e30
INT_MIN = -(2 ** 31)

PROJ_ROWS = 512
POST_ROWS = 256
ATT_TQ = 256
ATT_CH = 256
VMEM_LIMIT = 56 * 1024 * 1024


def _const_spec(shape):
    zeros = (0,) * len(shape)
    return pl.BlockSpec(shape, lambda *_: zeros, pipeline_mode=pl.Buffered(1))


def _dot(a, b):
    return jnp.dot(a, b, preferred_element_type=F32)


def _dot_nt(a, b):
    return lax.dot_general(a, b, (((1,), (1,)), ((), ())), preferred_element_type=F32)


def _rms_norm(x, g):
    return x * lax.rsqrt(jnp.mean(x * x, axis=-1, keepdims=True) + RMS_EPS) * g


def _layer_norm(x, g, b):
    mu = jnp.mean(x, axis=-1, keepdims=True)
    xc = x - mu
    var = jnp.mean(xc * xc, axis=-1, keepdims=True)
    return xc * lax.rsqrt(var + LN_EPS) * g + b


def _sigmoid(x):
    return 1.0 / (1.0 + jnp.exp(-x))


def _bf16_terms(x, n):
    terms = []
    for _ in range(n):
        bits = struct.unpack("<I", struct.pack("<f", x))[0]
        bits = (bits + 0x7FFF + ((bits >> 16) & 1)) & 0xFFFF0000
        term = struct.unpack("<f", struct.pack("<I", bits))[0]
        terms.append(term)
        x -= term
    return terms


def _fold_rows(x, op, ways=4):
    parts = [x[j * SUBLANES:(j + 1) * SUBLANES] for j in range(x.shape[0] // SUBLANES)]
    accs = parts[:ways]
    for j in range(ways, len(parts)):
        accs[j % ways] = op(accs[j % ways], parts[j])
    while len(accs) > 1:
        accs = [op(accs[k], accs[k + 1]) for k in range(0, len(accs) - 1, 2)] + accs[len(accs) & ~1:]
    return accs[0]


def _attention(nvis, heads, prep, logits, values_t, o_ref, scratch):
    s_scr, cmax_scr, m_scr, l_scr, acc_scr = scratch
    m_scr[...] = jnp.full(m_scr.shape, NEG, F32)
    l_scr[...] = jnp.zeros(l_scr.shape, F32)
    acc_scr[...] = jnp.zeros(acc_scr.shape, F32)

    def logits_part(c, slot):
        ctx = prep(jnp.asarray(c, jnp.int32))
        for h in range(heads):
            s_t = logits(ctx, h)
            s_scr[h, slot] = s_t
            cmax_scr[h, slot] = _fold_rows(s_t, jnp.maximum)

    def values_part(c, slot):
        for h in range(heads):
            m_prev = m_scr[h]
            m_new = jnp.maximum(m_prev, jnp.max(cmax_scr[h, slot], axis=0, keepdims=True))
            alpha = jnp.exp2(m_prev - m_new)
            p = jnp.exp2(s_scr[h, slot] - m_new)
            l_scr[h] = alpha * l_scr[h] + _fold_rows(p, jnp.add)
            acc_scr[h] = alpha * acc_scr[h] + _dot(values_t(c, h), p.astype(BF16))
            m_scr[h] = m_new

    logits_part(0, 0)

    def body(c, _):
        slot = c % 2
        values_part(c, slot)
        logits_part(c + 1, 1 - slot)
        return 0

    lax.fori_loop(0, nvis - 1, body, 0)
    values_part(nvis - 1, (nvis - 1) % 2)

    norm = lambda h: acc_scr[h] / jnp.sum(l_scr[h], axis=0, keepdims=True)
    for pair in range(heads // 2):
        o_t = jnp.concatenate([norm(2 * pair), norm(2 * pair + 1)], axis=0)
        o_ref[:, pair * LANES:(pair + 1) * LANES] = o_t.T.astype(BF16)


def _attn_scratch(heads, dv, tq, ch):
    return [pltpu.VMEM((heads, 2, ch, tq), F32), pltpu.VMEM((heads, 2, SUBLANES, tq), F32),
            pltpu.VMEM((heads, 1, tq), F32), pltpu.VMEM((heads, SUBLANES, tq), F32),
            pltpu.VMEM((heads, dv, tq), F32)]


def _proj_kernel(x_ref, pos_ref, freq_ref, sign_ref, wc_ref, wkr_ref, wdsa_ref, wvbt_ref, widx_ref,
                 wgate_ref, qn_ref, kvn_ref, wuqm_ref, wuqs_ref, wuk_ref, wuvt_ref,
                 qa_ref, ka_ref, vat_ref, qb_ref, kb_ref, vbt_ref, qi_ref, ki_ref, wi_ref,
                 kpos_ref, gate_ref, *, ch):
    rows = x_ref.shape[0]
    xb = x_ref[...].astype(BF16)
    pos = pos_ref[...]
    ang = pos.astype(F32) * freq_ref[...]
    cos_t = jnp.cos(ang)
    sin_t = jnp.sin(ang) * sign_ref[...]

    c = _dot(xb, wc_ref[...])
    cq = _rms_norm(c[:, :MLA_Q_RANK], qn_ref[...]).astype(BF16)
    ckv = _rms_norm(c[:, MLA_Q_RANK:], kvn_ref[...]).astype(BF16)

    qm = _dot(cq, wuqm_ref[...])
    qs = _dot(cq, wuqs_ref[...])
    cos_q = cos_t * (MLA_SCALE * LOG2E)
    sin_q = sin_t * (MLA_SCALE * LOG2E)
    for h in range(MLA_HEADS):
        sl = slice(h * LANES, (h + 1) * LANES)
        qa_ref[:, sl] = (qm[:, sl] * cos_q + qs[:, sl] * sin_q).astype(BF16)

    kr = _dot(xb, wkr_ref[...])
    kr_rot = kr[:, :LANES] * cos_t + kr[:, LANES:] * sin_t
    kn = _dot(ckv, wuk_ref[...])
    for h in range(MLA_HEADS):
        sl = slice(h * LANES, (h + 1) * LANES)
        ka_ref[:, sl] = (kn[:, sl] + kr_rot).astype(BF16)

    d = _dot(xb, wdsa_ref[...])
    w = DSA_HEADS * DSA_HEAD_DIM
    qb_ref[...] = (d[:, :w] * LOG2E).astype(BF16)
    kb_ref[...] = d[:, w:].astype(BF16)

    va_t = _dot_nt(wuvt_ref[...], ckv).astype(BF16)
    vb_t = _dot_nt(wvbt_ref[...], xb).astype(BF16)
    for k in range(rows // ch):
        vat_ref[k] = va_t[:, k * ch:(k + 1) * ch]
        vbt_ref[k] = vb_t[:, k * ch:(k + 1) * ch]

    ix = _dot(xb, widx_ref[...])
    qi_ref[...] = ix[:, :2 * LANES].astype(BF16)
    ki_ref[...] = ix[:, 2 * LANES:3 * LANES].astype(BF16)
    wi_ref[...] = ix[:, 3 * LANES:]

    lane = lax.broadcasted_iota(jnp.int32, (rows, LANES), 1)
    hi = (pos >> (POS_SPLIT.bit_length() - 1)).astype(F32)
    lo = (pos & (POS_SPLIT - 1)).astype(F32)
    kpos_ref[...] = jnp.where(lane < POS_TERMS, hi,
                              jnp.where(lane < 2 * POS_TERMS, lo, 0.0)).astype(BF16)

    gate_ref[...] = _dot(xb, wgate_ref[...]).astype(BF16)


def _proj_call(x2, pos2, freq, sign, weights, rows, ch):
    n = x2.shape[0]
    (wc, wkr, wdsa, wvbt, widx, wgate, qn, kvn, wuqm, wuqs, wuk, wuvt) = weights
    row_spec = lambda wdt: pl.BlockSpec((rows, wdt), lambda i: (i, 0))
    row_out = lambda wdt, dt: (row_spec(wdt), jax.ShapeDtypeStruct((n, wdt), dt))
    t_out = lambda wdt: (pl.BlockSpec((rows // ch, wdt, ch), lambda i: (i, 0, 0)),
                         jax.ShapeDtypeStruct((n // ch, wdt, ch), BF16))
    outs = [row_out(8 * LANES, BF16), row_out(8 * LANES, BF16), t_out(MLA_HEADS * MLA_V),
            row_out(512, BF16), row_out(512, BF16), t_out(DSA_HEADS * DSA_HEAD_DIM),
            row_out(2 * LANES, BF16), row_out(LANES, BF16), row_out(LANES, F32),
            row_out(LANES, BF16), row_out(2 * D_MODEL, BF16)]
    consts = (freq, sign, wc, wkr, wdsa, wvbt, widx, wgate, qn, kvn, wuqm, wuqs, wuk, wuvt)
    return pl.pallas_call(
        functools.partial(_proj_kernel, ch=ch),
        grid=(n // rows,),
        in_specs=[row_spec(D_MODEL), row_spec(1)] + [_const_spec(a.shape) for a in consts],
        out_specs=[o[0] for o in outs],
        out_shape=[o[1] for o in outs],
        compiler_params=pltpu.CompilerParams(
            dimension_semantics=("parallel",), vmem_limit_bytes=VMEM_LIMIT),
        name="proj",
    )(x2, pos2, *consts)


def _mla_kernel(q_ref, k_ref, vt_ref, o_ref, *scratch, tq, ch):
    i = pl.program_id(1)
    nvis = ((i + 1) * tq + ch - 1) // ch
    key = lax.broadcasted_iota(jnp.int32, (ch, tq), 0)
    query = i * tq + lax.broadcasted_iota(jnp.int32, (ch, tq), 1)

    def prep(c):
        ks = pl.multiple_of(c * ch, ch)
        return ks, jnp.where(key + ks <= query, 0.0, NEG)

    def logits(ctx, h):
        ks, causal = ctx
        hsl = slice(h * LANES, (h + 1) * LANES)
        return _dot_nt(k_ref[pl.ds(ks, ch), hsl], q_ref[:, hsl]) + causal

    def values_t(c, h):
        return vt_ref[c, h * MLA_V:(h + 1) * MLA_V, :]

    _attention(nvis, MLA_HEADS, prep, logits, values_t, o_ref, scratch)


def _mla_call(qa, ka, vat, tq, ch):
    b, s, _ = qa.shape
    wv = MLA_HEADS * MLA_V
    return pl.pallas_call(
        functools.partial(_mla_kernel, tq=tq, ch=ch),
        grid=(b, s // tq),
        in_specs=[
            pl.BlockSpec((None, tq, 8 * LANES), lambda bi, i: (bi, i, 0)),
            pl.BlockSpec((None, s, 8 * LANES), lambda bi, i: (bi, 0, 0)),
            pl.BlockSpec((None, s // ch, wv, ch), lambda bi, i: (bi, 0, 0, 0)),
        ],
        out_specs=pl.BlockSpec((None, tq, wv), lambda bi, i: (bi, i, 0)),
        out_shape=jax.ShapeDtypeStruct((b, s, wv), BF16),
        scratch_shapes=_attn_scratch(MLA_HEADS, MLA_V, tq, ch),
        compiler_params=pltpu.CompilerParams(
            dimension_semantics=("parallel", "arbitrary"), vmem_limit_bytes=VMEM_LIMIT),
        name="mla",
    )(qa, ka, vat)


def _dsa_kernel(qb_ref, kb_ref, vbt_ref, qi_ref, ki_ref, wi_ref, kpos_ref, o_ref,
                key_scr, bias_scr, jsel_scr, *scratch, tq, ch, topk, seq):
    i = pl.program_id(1)
    nvis = ((i + 1) * tq + ch - 1) // ch
    key_idx = lax.broadcasted_iota(jnp.int32, (ch, tq), 0)
    query = i * tq + lax.broadcasted_iota(jnp.int32, (ch, tq), 1)
    lane = lax.broadcasted_iota(jnp.int32, (tq, LANES), 1)

    qi = qi_ref[...]
    wi_t = wi_ref[...].T
    qi_heads = []
    for h in range(IDX_HEADS):
        group = qi[:, (h // 4) * LANES:(h // 4 + 1) * LANES]
        lo = IDX_DIM * (h % 4)
        keep = jnp.where(lane >= lo, jnp.where(lane < lo + IDX_DIM, 1.0, 0.0), 0.0).astype(BF16)
        qi_heads.append(group * keep)

    def score_body(c, _):
        ks = pl.multiple_of(c * ch, ch)
        kic = ki_ref[pl.ds(ks, ch), :]
        sc = jnp.zeros((ch, tq), F32)
        for h in range(IDX_HEADS):
            sc = sc + wi_t[h:h + 1, :] * jnp.maximum(_dot_nt(kic, qi_heads[h]), 0.0)
        sc = sc * IDX_SCALE
        sc = jnp.where(sc == 0.0, 0.0, sc)
        bits = pltpu.bitcast(sc, jnp.int32)
        key = bits ^ ((bits >> 31) & 0x7FFFFFFF)
        key_scr[c] = jnp.where(key_idx + ks <= query, key, INT_MIN)
        return 0

    lax.fori_loop(0, nvis, score_body, 0)

    def count(pred):
        def body(c, acc):
            return acc + _fold_rows(pred(key_scr[c], c), jnp.add)
        acc = lax.fori_loop(0, nvis, body, jnp.zeros((SUBLANES, tq), F32))
        return jnp.sum(acc, axis=0, keepdims=True)

    def bis_body(it, carry):
        lo, cnt_lo, cnt_hi = carry
        cand = lo + jnp.left_shift(jnp.int32(1), 31 - it)
        cnt = count(lambda k, c: jnp.where(k >= cand, 1.0, 0.0))
        ok = cnt >= topk
        return jnp.where(ok, cand, lo), jnp.where(ok, cnt, cnt_lo), jnp.where(ok, cnt_hi, cnt)

    n_bits = jnp.where((i + 1) * tq <= topk, 0, 32)
    tau, cnt_ge, cnt_gt = lax.fori_loop(
        0, n_bits, bis_body,
        (jnp.full((1, tq), INT_MIN, jnp.int32),
         jnp.zeros((1, tq), F32) + (nvis * ch).astype(F32),
         jnp.zeros((1, tq), F32)))
    has_tau = tau > INT_MIN
    need = topk - cnt_gt
    jsel_scr[...] = jnp.where(has_tau, seq, -1)
    excess = jnp.where(has_tau, jnp.where(cnt_ge - cnt_gt > need, 1.0, 0.0), 0.0)

    @pl.when(jnp.max(excess) > 0.0)
    def _():
        def idx_body(it, lo):
            cand = lo + jnp.left_shift(jnp.int32(1), seq.bit_length() - 2 - it)
            cnt = count(lambda k, c: jnp.where(
                k == tau, jnp.where(key_idx + c * ch < cand, 1.0, 0.0), 0.0))
            return jnp.where(cnt < need, cand, lo)
        last = lax.fori_loop(0, seq.bit_length() - 1, idx_body, jnp.zeros((1, tq), jnp.int32))
        jsel_scr[...] = jnp.where(has_tau, last, -1)

    jsel = jsel_scr[...]

    def bias_body(c, _):
        k = key_scr[c]
        tie = jnp.where(key_idx + c * ch <= jsel, 0.0, NEG)
        bias_scr[c] = jnp.where(k > tau, 0.0, jnp.where(k == tau, tie, NEG))
        return 0

    lax.fori_loop(0, nvis, bias_body, 0)

    q_heads = []
    for h in range(DSA_HEADS):
        qpair = qb_ref[:, (h // 2) * LANES:(h // 2 + 1) * LANES]
        if h % 2 == 0:
            keep = jnp.where(lane < DSA_HEAD_DIM, 1.0, 0.0).astype(BF16)
        else:
            keep = jnp.where(lane >= DSA_HEAD_DIM, 1.0, 0.0).astype(BF16)
        terms = _bf16_terms(ALIBI_SLOPES[h] * LOG2E, POS_TERMS)
        q_pos = jnp.zeros((tq, LANES), F32)
        for j, term in enumerate(terms):
            q_pos = jnp.where(lane == j, POS_SPLIT * term, jnp.where(lane == POS_TERMS + j, term, q_pos))
        q_heads.append(jnp.concatenate([qpair * keep, q_pos.astype(BF16)], axis=1))

    def prep(c):
        ks = pl.multiple_of(c * ch, ch)
        return ks, bias_scr[c], kpos_ref[pl.ds(ks, ch), :]

    def logits(ctx, h):
        ks, sel, kpos = ctx
        psl = slice((h // 2) * LANES, (h // 2 + 1) * LANES)
        k_aug = jnp.concatenate([kb_ref[pl.ds(ks, ch), psl], kpos], axis=1)
        return _dot_nt(k_aug, q_heads[h]) + sel

    def values_t(c, h):
        return vbt_ref[c, h * DSA_HEAD_DIM:(h + 1) * DSA_HEAD_DIM, :]

    _attention(nvis, DSA_HEADS, prep, logits, values_t, o_ref, scratch)


def _dsa_call(qb, kb, vbt, qi, ki, wi, kpos, tq, ch):
    b, s, w = qb.shape
    nch = s // ch
    topk = min(TOPK_MAX, s // 4)
    full = lambda wdt: pl.BlockSpec((None, s, wdt), lambda bi, i: (bi, 0, 0))
    tile = lambda wdt: pl.BlockSpec((None, tq, wdt), lambda bi, i: (bi, i, 0))
    return pl.pallas_call(
        functools.partial(_dsa_kernel, tq=tq, ch=ch, topk=topk, seq=s),
        grid=(b, s // tq),
        in_specs=[tile(w), full(w), pl.BlockSpec((None, nch, w, ch), lambda bi, i: (bi, 0, 0, 0)),
                  tile(2 * LANES), full(LANES), tile(LANES), full(LANES)],
        out_specs=tile(w),
        out_shape=jax.ShapeDtypeStruct((b, s, w), BF16),
        scratch_shapes=[pltpu.VMEM((nch, ch, tq), jnp.int32), pltpu.VMEM((nch, ch, tq), F32),
                        pltpu.VMEM((1, tq), jnp.int32)]
        + _attn_scratch(DSA_HEADS, DSA_HEAD_DIM, tq, ch),
        compiler_params=pltpu.CompilerParams(
            dimension_semantics=("parallel", "arbitrary"), vmem_limit_bytes=VMEM_LIMIT),
        name="dsa",
    )(qb, kb, vbt, qi, ki, wi, kpos)


def _post_kernel(x_ref, oa_ref, ob_ref, gate_ref, wa_ref, wb_ref, wo_ref, g1_ref, b1_ref,
                 wfi_ref, wfd_ref, g2_ref, b2_ref, out_ref):
    ya = _dot(oa_ref[...], wa_ref[...])
    yb = _dot(ob_ref[...], wb_ref[...])
    mixed = (_sigmoid(gate_ref[:, :D_MODEL].astype(F32)) * ya
             + _sigmoid(gate_ref[:, D_MODEL:].astype(F32)) * yb)
    mix_out = _dot(mixed.astype(BF16), wo_ref[...])
    h1 = _layer_norm(ALPHA * x_ref[...] + mix_out, g1_ref[...], b1_ref[...])
    gu = _dot(h1.astype(BF16), wfi_ref[...])
    gate = gu[:, :D_FF]
    act = gate * _sigmoid(gate) * gu[:, D_FF:]
    f = _dot(act.astype(BF16), wfd_ref[...])
    out_ref[...] = _layer_norm(ALPHA * h1 + f, g2_ref[...], b2_ref[...])


def _post_call(x2, oa, ob, gates, weights, rows):
    n = x2.shape[0]
    row_spec = lambda wdt: pl.BlockSpec((rows, wdt), lambda i: (i, 0))
    return pl.pallas_call(
        _post_kernel,
        grid=(n // rows,),
        in_specs=[row_spec(D_MODEL), row_spec(oa.shape[1]), row_spec(ob.shape[1]),
                  row_spec(2 * D_MODEL)] + [_const_spec(a.shape) for a in weights],
        out_specs=row_spec(D_MODEL),
        out_shape=jax.ShapeDtypeStruct((n, D_MODEL), F32),
        compiler_params=pltpu.CompilerParams(
            dimension_semantics=("parallel",), vmem_limit_bytes=VMEM_LIMIT),
        name="post",
    )(x2, oa, ob, gates, *weights)


def _pad_cols(w, width):
    return jnp.pad(w, ((0, 0), (0, width - w.shape[1])))


def _prep_proj_weights(w_in, q_norm, kv_norm, w_uq, w_ukv):
    sizes = (MLA_Q_RANK, MLA_KV_RANK, MLA_ROPE, 512, 512, 512, IDX_HEADS * IDX_DIM, IDX_DIM,
             IDX_HEADS, D_MODEL, D_MODEL)
    parts, start = [], 0
    for n in sizes:
        parts.append(w_in[:, start:start + n])
        start += n
    w_cq, w_ckv, w_kr, w_qb, w_kb, w_vb, w_qi, w_ki, w_wi, w_ga, w_gb = parts
    half = MLA_ROPE // 2
    d = w_in.shape[0]

    def rope_group(first, second):
        return jnp.concatenate([jnp.zeros((d, MLA_NOPE), F32), first, second,
                                jnp.zeros((d, LANES - MLA_QK), F32)], axis=1)

    wc = jnp.concatenate([w_cq, w_ckv], axis=1)
    wkr = jnp.concatenate([rope_group(w_kr[:, :half], w_kr[:, half:]),
                           rope_group(w_kr[:, half:], w_kr[:, :half])], axis=1)
    wdsa = jnp.concatenate([w_qb * DSA_SCALE, w_kb], axis=1)
    widx = jnp.concatenate([w_qi, w_ki, w_ki, w_ki, w_ki, _pad_cols(w_wi, LANES)], axis=1)
    wgate = jnp.concatenate([w_ga, w_gb], axis=1)

    r = w_uq.shape[0]
    uq = w_uq.reshape(r, MLA_HEADS, MLA_QK)
    zq = lambda n: jnp.zeros((r, MLA_HEADS, n), F32)
    wuqm = jnp.concatenate([uq, zq(LANES - MLA_QK)], axis=2).reshape(r, MLA_HEADS * LANES)
    wuqs = jnp.concatenate([zq(MLA_NOPE), uq[:, :, MLA_NOPE + half:], uq[:, :, MLA_NOPE:MLA_NOPE + half],
                            zq(LANES - MLA_QK)], axis=2).reshape(r, MLA_HEADS * LANES)
    ukv = w_ukv.reshape(r, MLA_HEADS, MLA_NOPE + MLA_V)
    wuk = jnp.concatenate([ukv[:, :, :MLA_NOPE], zq(LANES - MLA_NOPE)], axis=2).reshape(r, MLA_HEADS * LANES)
    wuvt = ukv[:, :, MLA_NOPE:].reshape(r, MLA_HEADS * MLA_V).T

    bf = lambda a: a.astype(BF16)
    return (bf(wc), bf(wkr), bf(wdsa), bf(w_vb.T), bf(widx), bf(wgate), q_norm.reshape(1, -1),
            kv_norm.reshape(1, -1), bf(wuqm), bf(wuqs), bf(wuk), bf(wuvt))


def _rope_tables():
    half = MLA_ROPE // 2
    inv_freq = ROPE_BASE ** (-jnp.arange(half, dtype=F32) / half)
    z = lambda n: jnp.zeros((n,), F32)
    freq = jnp.concatenate([z(MLA_NOPE), inv_freq, inv_freq, z(LANES - MLA_QK)]).reshape(1, LANES)
    sign = jnp.concatenate([z(MLA_NOPE), -jnp.ones((half,), F32), jnp.ones((half,), F32),
                            z(LANES - MLA_QK)]).reshape(1, LANES)
    return freq, sign


def kernel(x, positions, w_in, mla_q_norm, mla_kv_norm, mla_w_uq, mla_w_ukv, w_branch_a, w_branch_b,
           w_out, ln1_g, ln1_b, ffn_w_in, ffn_w_down, ln2_g, ln2_b):
    b, s, d = x.shape
    assert w_in.shape[0] == DEPTH and d == D_MODEL
    assert s & (s - 1) == 0 and s % ATT_TQ == 0 and s % ATT_CH == 0 and PROJ_ROWS % ATT_CH == 0
    assert (b * s) % PROJ_ROWS == 0 and s <= POS_SPLIT * 256
    n = b * s
    h = x.reshape(n, d)
    pos2 = positions.reshape(n, 1)
    freq, sign = _rope_tables()
    for l in range(DEPTH):
        pw = _prep_proj_weights(w_in[l], mla_q_norm[l], mla_kv_norm[l], mla_w_uq[l], mla_w_ukv[l])
        qa, ka, vat, qb, kb, vbt, qi, ki, wi, kpos, gates = _proj_call(
            h, pos2, freq, sign, pw, PROJ_ROWS, ATT_CH)
        r3 = lambda a: a.reshape(b, s, a.shape[1])
        t4 = lambda a: a.reshape(b, s // ATT_CH, a.shape[1], ATT_CH)
        o_a = _mla_call(r3(qa), r3(ka), t4(vat), ATT_TQ, ATT_CH)
        o_b = _dsa_call(r3(qb), r3(kb), t4(vbt), r3(qi), r3(ki), r3(wi), r3(kpos), ATT_TQ, ATT_CH)
        row = lambda v: v.reshape(1, -1)
        post_w = (w_branch_a[l].astype(BF16), w_branch_b[l].astype(BF16), w_out[l].astype(BF16),
                  row(ln1_g[l]), row(ln1_b[l]), ffn_w_in[l].astype(BF16), ffn_w_down[l].astype(BF16),
                  row(ln2_g[l]), row(ln2_b[l]))
        h = _post_call(h, o_a.reshape(n, -1), o_b.reshape(n, -1), gates, post_w, POST_ROWS)
    return h.reshape(b, s, d)
```

```python
import functools
import math
import struct

import jax
import jax.numpy as jnp
from jax import lax
from jax.experimental import pallas as pl
from jax.experimental.pallas import tpu as pltpu

F32 = jnp.float32
BF16 = jnp.bfloat16

D_MODEL = 1024
MLA_HEADS = 8
MLA_Q_RANK = 256
MLA_KV_RANK = 256
MLA_NOPE = 64
MLA_ROPE = 32
MLA_V = 64
ROPE_BASE = 10000.0
DSA_HEADS = 8
DSA_HEAD_DIM = 64
IDX_HEADS = 8
IDX_DIM = 32
TOPK_MAX = 256
D_FF = 2816
DEPTH = 1
ALPHA = (2 * DEPTH) ** 0.25
LN_EPS = 1e-5
RMS_EPS = 1e-6

LANES = 128
SUBLANES = 8
MLA_QK = MLA_NOPE + MLA_ROPE
MLA_SCALE = 1.0 / math.sqrt(MLA_QK)
DSA_SCALE = 1.0 / math.sqrt(DSA_HEAD_DIM)
IDX_SCALE = 1.0 / math.sqrt(IDX_DIM * IDX_HEADS)
ALIBI_SLOPES = tuple(2.0 ** (-8.0 * (i + 1) / DSA_HEADS) for i in range(DSA_HEADS))
POS_SPLIT = 64
POS_TERMS = 3
LOG2E = math.log2(math.e)

NEG = -1e30
INT_MIN = -(2 ** 31)
I16 = jnp.int16
I16_MIN = -(2 ** 15)

PROJ_ROWS = 512
POST_ROWS = 256
ATT_TQ = 256
ATT_CH = 256
VMEM_LIMIT = 56 * 1024 * 1024


def _const_spec(shape):
    zeros = (0,) * len(shape)
    return pl.BlockSpec(shape, lambda *_: zeros, pipeline_mode=pl.Buffered(1))


def _dot(a, b):
    return jnp.dot(a, b, preferred_element_type=F32)


def _dot_nt(a, b):
    return lax.dot_general(a, b, (((1,), (1,)), ((), ())), preferred_element_type=F32)


def _rms_norm(x, g):
    return x * lax.rsqrt(jnp.mean(x * x, axis=-1, keepdims=True) + RMS_EPS) * g


def _layer_norm(x, g, b):
    mu = jnp.mean(x, axis=-1, keepdims=True)
    xc = x - mu
    var = jnp.mean(xc * xc, axis=-1, keepdims=True)
    return xc * lax.rsqrt(var + LN_EPS) * g + b


def _sigmoid(x):
    return 1.0 / (1.0 + jnp.exp(-x))


def _bf16_terms(x, n):
    terms = []
    for _ in range(n):
        bits = struct.unpack("<I", struct.pack("<f", x))[0]
        bits = (bits + 0x7FFF + ((bits >> 16) & 1)) & 0xFFFF0000
        term = struct.unpack("<f", struct.pack("<I", bits))[0]
        terms.append(term)
        x -= term
    return terms


def _slab_rows(dtype):
    return SUBLANES * (4 // jnp.dtype(dtype).itemsize)


def _fold_rows(x, op, ways=4):
    slab = _slab_rows(x.dtype)
    parts = [x[j * slab:(j + 1) * slab] for j in range(x.shape[0] // slab)]
    accs = parts[:ways]
    for j in range(ways, len(parts)):
        accs[j % ways] = op(accs[j % ways], parts[j])
    while len(accs) > 1:
        accs = [op(accs[k], accs[k + 1]) for k in range(0, len(accs) - 1, 2)] + accs[len(accs) & ~1:]
    return accs[0]


def _attention(nvis, heads, prep, logits, values_t, o_ref, scratch):
    s_scr, cmax_scr, m_scr, l_scr, acc_scr = scratch
    m_scr[...] = jnp.full(m_scr.shape, NEG, F32)
    l_scr[...] = jnp.zeros(l_scr.shape, F32)
    acc_scr[...] = jnp.zeros(acc_scr.shape, F32)

    def logits_part(c, slot):
        ctx = prep(jnp.asarray(c, jnp.int32))
        for h in range(heads):
            s_t = logits(ctx, h)
            s_scr[h, slot] = s_t
            cmax_scr[h, slot] = _fold_rows(s_t, jnp.maximum)

    def values_part(c, slot):
        for h in range(heads):
            m_prev = m_scr[h]
            m_new = jnp.maximum(m_prev, jnp.max(cmax_scr[h, slot], axis=0, keepdims=True))
            alpha = jnp.exp2(m_prev - m_new)
            p = jnp.exp2(s_scr[h, slot] - m_new)
            l_scr[h] = alpha * l_scr[h] + _fold_rows(p, jnp.add)
            acc_scr[h] = alpha * acc_scr[h] + _dot(values_t(c, h), p.astype(BF16))
            m_scr[h] = m_new

    logits_part(0, 0)

    def body(c, _):
        slot = c % 2
        values_part(c, slot)
        logits_part(c + 1, 1 - slot)
        return 0

    lax.fori_loop(0, nvis - 1, body, 0)
    values_part(nvis - 1, (nvis - 1) % 2)

    norm = lambda h: acc_scr[h] / jnp.sum(l_scr[h], axis=0, keepdims=True)
    for pair in range(heads // 2):
        o_t = jnp.concatenate([norm(2 * pair), norm(2 * pair + 1)], axis=0)
        o_ref[:, pair * LANES:(pair + 1) * LANES] = o_t.T.astype(BF16)


def _attn_scratch(heads, dv, tq, ch):
    return [pltpu.VMEM((heads, 2, ch, tq), F32), pltpu.VMEM((heads, 2, SUBLANES, tq), F32),
            pltpu.VMEM((heads, 1, tq), F32), pltpu.VMEM((heads, SUBLANES, tq), F32),
            pltpu.VMEM((heads, dv, tq), F32)]


def _proj_kernel(x_ref, pos_ref, freq_ref, sign_ref, wc_ref, wkr_ref, wdsa_ref, wvbt_ref, widx_ref,
                 wgate_ref, qn_ref, kvn_ref, wuqm_ref, wuqs_ref, wuk_ref, wuvt_ref,
                 qa_ref, ka_ref, vat_ref, qb_ref, kb_ref, vbt_ref, qi_ref, ki_ref, wi_ref,
                 kpos_ref, gate_ref, *, ch):
    rows = x_ref.shape[0]
    xb = x_ref[...].astype(BF16)
    pos = pos_ref[...]
    ang = pos.astype(F32) * freq_ref[...]
    cos_t = jnp.cos(ang)
    sin_t = jnp.sin(ang) * sign_ref[...]

    c = _dot(xb, wc_ref[...])
    cq = _rms_norm(c[:, :MLA_Q_RANK], qn_ref[...]).astype(BF16)
    ckv = _rms_norm(c[:, MLA_Q_RANK:], kvn_ref[...]).astype(BF16)

    qm = _dot(cq, wuqm_ref[...])
    qs = _dot(cq, wuqs_ref[...])
    cos_q = cos_t * (MLA_SCALE * LOG2E)
    sin_q = sin_t * (MLA_SCALE * LOG2E)
    for h in range(MLA_HEADS):
        sl = slice(h * LANES, (h + 1) * LANES)
        qa_ref[:, sl] = (qm[:, sl] * cos_q + qs[:, sl] * sin_q).astype(BF16)

    kr = _dot(xb, wkr_ref[...])
    kr_rot = kr[:, :LANES] * cos_t + kr[:, LANES:] * sin_t
    kn = _dot(ckv, wuk_ref[...])
    for h in range(MLA_HEADS):
        sl = slice(h * LANES, (h + 1) * LANES)
        ka_ref[:, sl] = (kn[:, sl] + kr_rot).astype(BF16)

    d = _dot(xb, wdsa_ref[...])
    w = DSA_HEADS * DSA_HEAD_DIM
    qb_ref[...] = (d[:, :w] * LOG2E).astype(BF16)
    kb_ref[...] = d[:, w:].astype(BF16)

    va_t = _dot_nt(wuvt_ref[...], ckv).astype(BF16)
    vb_t = _dot_nt(wvbt_ref[...], xb).astype(BF16)
    for k in range(rows // ch):
        vat_ref[k] = va_t[:, k * ch:(k + 1) * ch]
        vbt_ref[k] = vb_t[:, k * ch:(k + 1) * ch]

    ix = _dot(xb, widx_ref[...])
    qi_ref[...] = ix[:, :2 * LANES].astype(BF16)
    ki_ref[...] = ix[:, 2 * LANES:3 * LANES].astype(BF16)
    wi_ref[...] = ix[:, 3 * LANES:]

    lane = lax.broadcasted_iota(jnp.int32, (rows, LANES), 1)
    hi = (pos >> (POS_SPLIT.bit_length() - 1)).astype(F32)
    lo = (pos & (POS_SPLIT - 1)).astype(F32)
    kpos_ref[...] = jnp.where(lane < POS_TERMS, hi,
                              jnp.where(lane < 2 * POS_TERMS, lo, 0.0)).astype(BF16)

    gate_ref[...] = _dot(xb, wgate_ref[...]).astype(BF16)


def _proj_call(x2, pos2, freq, sign, weights, rows, ch):
    n = x2.shape[0]
    (wc, wkr, wdsa, wvbt, widx, wgate, qn, kvn, wuqm, wuqs, wuk, wuvt) = weights
    row_spec = lambda wdt: pl.BlockSpec((rows, wdt), lambda i: (i, 0))
    row_out = lambda wdt, dt: (row_spec(wdt), jax.ShapeDtypeStruct((n, wdt), dt))
    t_out = lambda wdt: (pl.BlockSpec((rows // ch, wdt, ch), lambda i: (i, 0, 0)),
                         jax.ShapeDtypeStruct((n // ch, wdt, ch), BF16))
    outs = [row_out(8 * LANES, BF16), row_out(8 * LANES, BF16), t_out(MLA_HEADS * MLA_V),
            row_out(512, BF16), row_out(512, BF16), t_out(DSA_HEADS * DSA_HEAD_DIM),
            row_out(2 * LANES, BF16), row_out(LANES, BF16), row_out(LANES, F32),
            row_out(LANES, BF16), row_out(2 * D_MODEL, BF16)]
    consts = (freq, sign, wc, wkr, wdsa, wvbt, widx, wgate, qn, kvn, wuqm, wuqs, wuk, wuvt)
    return pl.pallas_call(
        functools.partial(_proj_kernel, ch=ch),
        grid=(n // rows,),
        in_specs=[row_spec(D_MODEL), row_spec(1)] + [_const_spec(a.shape) for a in consts],
        out_specs=[o[0] for o in outs],
        out_shape=[o[1] for o in outs],
        compiler_params=pltpu.CompilerParams(
            dimension_semantics=("parallel",), vmem_limit_bytes=VMEM_LIMIT),
        name="proj",
    )(x2, pos2, *consts)


def _mla_kernel(q_ref, k_ref, vt_ref, o_ref, *scratch, tq, ch):
    i = pl.program_id(1)
    nvis = ((i + 1) * tq + ch - 1) // ch
    key = lax.broadcasted_iota(jnp.int32, (ch, tq), 0)
    query = i * tq + lax.broadcasted_iota(jnp.int32, (ch, tq), 1)

    def prep(c):
        ks = pl.multiple_of(c * ch, ch)
        return ks, jnp.where(key + ks <= query, 0.0, NEG)

    def logits(ctx, h):
        ks, causal = ctx
        hsl = slice(h * LANES, (h + 1) * LANES)
        return _dot_nt(k_ref[pl.ds(ks, ch), hsl], q_ref[:, hsl]) + causal

    def values_t(c, h):
        return vt_ref[c, h * MLA_V:(h + 1) * MLA_V, :]

    _attention(nvis, MLA_HEADS, prep, logits, values_t, o_ref, scratch)


def _mla_call(qa, ka, vat, tq, ch):
    b, s, _ = qa.shape
    wv = MLA_HEADS * MLA_V
    return pl.pallas_call(
        functools.partial(_mla_kernel, tq=tq, ch=ch),
        grid=(b, s // tq),
        in_specs=[
            pl.BlockSpec((None, tq, 8 * LANES), lambda bi, i: (bi, i, 0)),
            pl.BlockSpec((None, s, 8 * LANES), lambda bi, i: (bi, 0, 0)),
            pl.BlockSpec((None, s // ch, wv, ch), lambda bi, i: (bi, 0, 0, 0)),
        ],
        out_specs=pl.BlockSpec((None, tq, wv), lambda bi, i: (bi, i, 0)),
        out_shape=jax.ShapeDtypeStruct((b, s, wv), BF16),
        scratch_shapes=_attn_scratch(MLA_HEADS, MLA_V, tq, ch),
        compiler_params=pltpu.CompilerParams(
            dimension_semantics=("parallel", "arbitrary"), vmem_limit_bytes=VMEM_LIMIT),
        name="mla",
    )(qa, ka, vat)


def _dsa_kernel(qb_ref, kb_ref, vbt_ref, qi_ref, ki_ref, wi_ref, kpos_ref, o_ref,
                key_scr, hi_scr, lo_scr, bias_scr, jsel_scr, *scratch, tq, ch, topk, seq):
    i = pl.program_id(1)
    nvis = ((i + 1) * tq + ch - 1) // ch
    key_idx = lax.broadcasted_iota(jnp.int32, (ch, tq), 0)
    query = i * tq + lax.broadcasted_iota(jnp.int32, (ch, tq), 1)
    lane = lax.broadcasted_iota(jnp.int32, (tq, LANES), 1)

    qi = qi_ref[...]
    wi_t = wi_ref[...].T
    qi_heads = []
    for h in range(IDX_HEADS):
        group = qi[:, (h // 4) * LANES:(h // 4 + 1) * LANES]
        lo = IDX_DIM * (h % 4)
        keep = jnp.where(lane >= lo, jnp.where(lane < lo + IDX_DIM, 1.0, 0.0), 0.0).astype(BF16)
        qi_heads.append(group * keep)

    def score_body(c, _):
        ks = pl.multiple_of(c * ch, ch)
        kic = ki_ref[pl.ds(ks, ch), :]
        sc = jnp.zeros((ch, tq), F32)
        for h in range(IDX_HEADS):
            sc = sc + wi_t[h:h + 1, :] * jnp.maximum(_dot_nt(kic, qi_heads[h]), 0.0)
        sc = sc * IDX_SCALE
        sc = jnp.where(sc == 0.0, 0.0, sc)
        bits = pltpu.bitcast(sc, jnp.int32)
        key = bits ^ ((bits >> 31) & 0x7FFFFFFF)
        key = jnp.where(key_idx + ks <= query, key, INT_MIN)
        key_scr[c] = key
        hi_scr[c] = (key >> 16).astype(I16)
        lo_scr[c] = (((key ^ 0x8000) << 16) >> 16).astype(I16)
        return 0

    lax.fori_loop(0, nvis, score_body, 0)

    def count(pred):
        def body(c, acc):
            return acc + _fold_rows(pred(key_scr[c], c), jnp.add)
        acc = lax.fori_loop(0, nvis, body, jnp.zeros((SUBLANES, tq), F32))
        return jnp.sum(acc, axis=0, keepdims=True)

    def search16(half_scr, n_bits, target, cnt_all):
        def count16(cand):
            cand16 = cand.astype(I16)
            def body(c, acc):
                ones = jnp.where(half_scr[c] >= cand16, jnp.int16(1), jnp.int16(0))
                return acc + _fold_rows(ones, jnp.add)
            acc = lax.fori_loop(0, nvis, body, jnp.zeros((_slab_rows(I16), tq), I16))
            return jnp.sum(acc.astype(F32), axis=0, keepdims=True)

        def body(it, carry):
            lo, cnt_lo, cnt_hi = carry
            cand = lo + jnp.left_shift(jnp.int32(1), 15 - it)
            cnt = count16(cand)
            ok = cnt >= target
            return jnp.where(ok, cand, lo), jnp.where(ok, cnt, cnt_lo), jnp.where(ok, cnt_hi, cnt)

        return lax.fori_loop(0, n_bits, body,
                             (jnp.full((1, tq), I16_MIN, jnp.int32), cnt_all, jnp.zeros((1, tq), F32)))

    n_bits = jnp.where((i + 1) * tq <= topk, 0, 16)
    all_keys = jnp.zeros((1, tq), F32) + (nvis * ch).astype(F32)
    tau_hi, cnt_ge_hi, cnt_gt_hi = search16(hi_scr, n_bits, float(topk), all_keys)
    tau_hi16 = tau_hi.astype(I16)

    def bucket_body(c, _):
        lo_scr[c] = jnp.where(hi_scr[c] == tau_hi16, lo_scr[c], jnp.int16(I16_MIN))
        return 0

    lax.fori_loop(0, nvis, bucket_body, 0)
    tau_lo, cnt_ge_lo, cnt_gt_lo = search16(lo_scr, n_bits, topk - cnt_gt_hi, cnt_ge_hi - cnt_gt_hi)
    tau = (tau_hi << 16) + (tau_lo - I16_MIN)
    cnt_ge = cnt_gt_hi + cnt_ge_lo
    cnt_gt = cnt_gt_hi + cnt_gt_lo
    has_tau = tau > INT_MIN
    need = topk - cnt_gt
    jsel_scr[...] = jnp.where(has_tau, seq, -1)
    excess = jnp.where(has_tau, jnp.where(cnt_ge - cnt_gt > need, 1.0, 0.0), 0.0)

    @pl.when(jnp.max(excess) > 0.0)
    def _():
        def idx_body(it, lo):
            cand = lo + jnp.left_shift(jnp.int32(1), seq.bit_length() - 2 - it)
            cnt = count(lambda k, c: jnp.where(
                k == tau, jnp.where(key_idx + c * ch < cand, 1.0, 0.0), 0.0))
            return jnp.where(cnt < need, cand, lo)
        last = lax.fori_loop(0, seq.bit_length() - 1, idx_body, jnp.zeros((1, tq), jnp.int32))
        jsel_scr[...] = jnp.where(has_tau, last, -1)

    jsel = jsel_scr[...]

    def bias_body(c, _):
        k = key_scr[c]
        tie = jnp.where(key_idx + c * ch <= jsel, 0.0, NEG)
        bias_scr[c] = jnp.where(k > tau, 0.0, jnp.where(k == tau, tie, NEG))
        return 0

    lax.fori_loop(0, nvis, bias_body, 0)

    q_heads = []
    for h in range(DSA_HEADS):
        qpair = qb_ref[:, (h // 2) * LANES:(h // 2 + 1) * LANES]
        if h % 2 == 0:
            keep = jnp.where(lane < DSA_HEAD_DIM, 1.0, 0.0).astype(BF16)
        else:
            keep = jnp.where(lane >= DSA_HEAD_DIM, 1.0, 0.0).astype(BF16)
        terms = _bf16_terms(ALIBI_SLOPES[h] * LOG2E, POS_TERMS)
        q_pos = jnp.zeros((tq, LANES), F32)
        for j, term in enumerate(terms):
            q_pos = jnp.where(lane == j, POS_SPLIT * term, jnp.where(lane == POS_TERMS + j, term, q_pos))
        q_heads.append(jnp.concatenate([qpair * keep, q_pos.astype(BF16)], axis=1))

    def prep(c):
        ks = pl.multiple_of(c * ch, ch)
        return ks, bias_scr[c], kpos_ref[pl.ds(ks, ch), :]

    def logits(ctx, h):
        ks, sel, kpos = ctx
        psl = slice((h // 2) * LANES, (h // 2 + 1) * LANES)
        k_aug = jnp.concatenate([kb_ref[pl.ds(ks, ch), psl], kpos], axis=1)
        return _dot_nt(k_aug, q_heads[h]) + sel

    def values_t(c, h):
        return vbt_ref[c, h * DSA_HEAD_DIM:(h + 1) * DSA_HEAD_DIM, :]

    _attention(nvis, DSA_HEADS, prep, logits, values_t, o_ref, scratch)


def _dsa_call(qb, kb, vbt, qi, ki, wi, kpos, tq, ch):
    b, s, w = qb.shape
    nch = s // ch
    topk = min(TOPK_MAX, s // 4)
    full = lambda wdt: pl.BlockSpec((None, s, wdt), lambda bi, i: (bi, 0, 0))
    tile = lambda wdt: pl.BlockSpec((None, tq, wdt), lambda bi, i: (bi, i, 0))
    return pl.pallas_call(
        functools.partial(_dsa_kernel, tq=tq, ch=ch, topk=topk, seq=s),
        grid=(b, s // tq),
        in_specs=[tile(w), full(w), pl.BlockSpec((None, nch, w, ch), lambda bi, i: (bi, 0, 0, 0)),
                  tile(2 * LANES), full(LANES), tile(LANES), full(LANES)],
        out_specs=tile(w),
        out_shape=jax.ShapeDtypeStruct((b, s, w), BF16),
        scratch_shapes=[pltpu.VMEM((nch, ch, tq), jnp.int32), pltpu.VMEM((nch, ch, tq), I16),
                        pltpu.VMEM((nch, ch, tq), I16), pltpu.VMEM((nch, ch, tq), F32),
                        pltpu.VMEM((1, tq), jnp.int32)]
        + _attn_scratch(DSA_HEADS, DSA_HEAD_DIM, tq, ch),
        compiler_params=pltpu.CompilerParams(
            dimension_semantics=("parallel", "arbitrary"), vmem_limit_bytes=VMEM_LIMIT),
        name="dsa",
    )(qb, kb, vbt, qi, ki, wi, kpos)


def _post_kernel(x_ref, oa_ref, ob_ref, gate_ref, wa_ref, wb_ref, wo_ref, g1_ref, b1_ref,
                 wfi_ref, wfd_ref, g2_ref, b2_ref, out_ref):
    ya = _dot(oa_ref[...], wa_ref[...])
    yb = _dot(ob_ref[...], wb_ref[...])
    mixed = (_sigmoid(gate_ref[:, :D_MODEL].astype(F32)) * ya
             + _sigmoid(gate_ref[:, D_MODEL:].astype(F32)) * yb)
    mix_out = _dot(mixed.astype(BF16), wo_ref[...])
    h1 = _layer_norm(ALPHA * x_ref[...] + mix_out, g1_ref[...], b1_ref[...])
    gu = _dot(h1.astype(BF16), wfi_ref[...])
    gate = gu[:, :D_FF]
    act = gate * _sigmoid(gate) * gu[:, D_FF:]
    f = _dot(act.astype(BF16), wfd_ref[...])
    out_ref[...] = _layer_norm(ALPHA * h1 + f, g2_ref[...], b2_ref[...])


def _post_call(x2, oa, ob, gates, weights, rows):
    n = x2.shape[0]
    row_spec = lambda wdt: pl.BlockSpec((rows, wdt), lambda i: (i, 0))
    return pl.pallas_call(
        _post_kernel,
        grid=(n // rows,),
        in_specs=[row_spec(D_MODEL), row_spec(oa.shape[1]), row_spec(ob.shape[1]),
                  row_spec(2 * D_MODEL)] + [_const_spec(a.shape) for a in weights],
        out_specs=row_spec(D_MODEL),
        out_shape=jax.ShapeDtypeStruct((n, D_MODEL), F32),
        compiler_params=pltpu.CompilerParams(
            dimension_semantics=("parallel",), vmem_limit_bytes=VMEM_LIMIT),
        name="post",
    )(x2, oa, ob, gates, *weights)


def _pad_cols(w, width):
    return jnp.pad(w, ((0, 0), (0, width - w.shape[1])))


def _prep_proj_weights(w_in, q_norm, kv_norm, w_uq, w_ukv):
    sizes = (MLA_Q_RANK, MLA_KV_RANK, MLA_ROPE, 512, 512, 512, IDX_HEADS * IDX_DIM, IDX_DIM,
             IDX_HEADS, D_MODEL, D_MODEL)
    parts, start = [], 0
    for n in sizes:
        parts.append(w_in[:, start:start + n])
        start += n
    w_cq, w_ckv, w_kr, w_qb, w_kb, w_vb, w_qi, w_ki, w_wi, w_ga, w_gb = parts
    half = MLA_ROPE // 2
    d = w_in.shape[0]

    def rope_group(first, second):
        return jnp.concatenate([jnp.zeros((d, MLA_NOPE), F32), first, second,
                                jnp.zeros((d, LANES - MLA_QK), F32)], axis=1)

    wc = jnp.concatenate([w_cq, w_ckv], axis=1)
    wkr = jnp.concatenate([rope_group(w_kr[:, :half], w_kr[:, half:]),
                           rope_group(w_kr[:, half:], w_kr[:, :half])], axis=1)
    wdsa = jnp.concatenate([w_qb * DSA_SCALE, w_kb], axis=1)
    widx = jnp.concatenate([w_qi, w_ki, w_ki, w_ki, w_ki, _pad_cols(w_wi, LANES)], axis=1)
    wgate = jnp.concatenate([w_ga, w_gb], axis=1)

    r = w_uq.shape[0]
    uq = w_uq.reshape(r, MLA_HEADS, MLA_QK)
    zq = lambda n: jnp.zeros((r, MLA_HEADS, n), F32)
    wuqm = jnp.concatenate([uq, zq(LANES - MLA_QK)], axis=2).reshape(r, MLA_HEADS * LANES)
    wuqs = jnp.concatenate([zq(MLA_NOPE), uq[:, :, MLA_NOPE + half:], uq[:, :, MLA_NOPE:MLA_NOPE + half],
                            zq(LANES - MLA_QK)], axis=2).reshape(r, MLA_HEADS * LANES)
    ukv = w_ukv.reshape(r, MLA_HEADS, MLA_NOPE + MLA_V)
    wuk = jnp.concatenate([ukv[:, :, :MLA_NOPE], zq(LANES - MLA_NOPE)], axis=2).reshape(r, MLA_HEADS * LANES)
    wuvt = ukv[:, :, MLA_NOPE:].reshape(r, MLA_HEADS * MLA_V).T

    bf = lambda a: a.astype(BF16)
    return (bf(wc), bf(wkr), bf(wdsa), bf(w_vb.T), bf(widx), bf(wgate), q_norm.reshape(1, -1),
            kv_norm.reshape(1, -1), bf(wuqm), bf(wuqs), bf(wuk), bf(wuvt))


def _rope_tables():
    half = MLA_ROPE // 2
    inv_freq = ROPE_BASE ** (-jnp.arange(half, dtype=F32) / half)
    z = lambda n: jnp.zeros((n,), F32)
    freq = jnp.concatenate([z(MLA_NOPE), inv_freq, inv_freq, z(LANES - MLA_QK)]).reshape(1, LANES)
    sign = jnp.concatenate([z(MLA_NOPE), -jnp.ones((half,), F32), jnp.ones((half,), F32),
                            z(LANES - MLA_QK)]).reshape(1, LANES)
    return freq, sign


def kernel(x, positions, w_in, mla_q_norm, mla_kv_norm, mla_w_uq, mla_w_ukv, w_branch_a, w_branch_b,
           w_out, ln1_g, ln1_b, ffn_w_in, ffn_w_down, ln2_g, ln2_b):
    b, s, d = x.shape
    assert w_in.shape[0] == DEPTH and d == D_MODEL
    assert s & (s - 1) == 0 and s % ATT_TQ == 0 and s % ATT_CH == 0 and PROJ_ROWS % ATT_CH == 0
    assert (b * s) % PROJ_ROWS == 0 and s <= POS_SPLIT * 256
    n = b * s
    h = x.reshape(n, d)
    pos2 = positions.reshape(n, 1)
    freq, sign = _rope_tables()
    for l in range(DEPTH):
        pw = _prep_proj_weights(w_in[l], mla_q_norm[l], mla_kv_norm[l], mla_w_uq[l], mla_w_ukv[l])
        qa, ka, vat, qb, kb, vbt, qi, ki, wi, kpos, gates = _proj_call(
            h, pos2, freq, sign, pw, PROJ_ROWS, ATT_CH)
        r3 = lambda a: a.reshape(b, s, a.shape[1])
        t4 = lambda a: a.reshape(b, s // ATT_CH, a.shape[1], ATT_CH)
        o_a = _mla_call(r3(qa), r3(ka), t4(vat), ATT_TQ, ATT_CH)
        o_b = _dsa_call(r3(qb), r3(kb), t4(vbt), r3(qi), r3(ki), r3(wi), r3(kpos), ATT_TQ, ATT_CH)
        row = lambda v: v.reshape(1, -1)
        post_w = (w_branch_a[l].astype(BF16), w_branch_b[l].astype(BF16), w_out[l].astype(BF16),
                  row(ln1_g[l]), row(ln1_b[l]), ffn_w_in[l].astype(BF16), ffn_w_down[l].astype(BF16),
                  row(ln2_g[l]), row(ln2_b[l]))
        h = _post_call(h, o_a.reshape(n, -1), o_b.reshape(n, -1), gates, post_w, POST_ROWS)
    return h.reshape(b, s, d)
```

```python
import functools
import math
import struct

import jax
import jax.numpy as jnp
from jax import lax
from jax.experimental import pallas as pl
from jax.experimental.pallas import tpu as pltpu

F32 = jnp.float32
BF16 = jnp.bfloat16

D_MODEL = 1024
MLA_HEADS = 8
MLA_Q_RANK = 256
MLA_KV_RANK = 256
MLA_NOPE = 64
MLA_ROPE = 32
MLA_V = 64
ROPE_BASE = 10000.0
DSA_HEADS = 8
DSA_HEAD_DIM = 64
IDX_HEADS = 8
IDX_DIM = 32
TOPK_MAX = 256
D_FF = 2816
DEPTH = 1
ALPHA = (2 * DEPTH) ** 0.25
LN_EPS = 1e-5
RMS_EPS = 1e-6

LANES = 128
SUBLANES = 8
MLA_QK = MLA_NOPE + MLA_ROPE
MLA_SCALE = 1.0 / math.sqrt(MLA_QK)
DSA_SCALE = 1.0 / math.sqrt(DSA_HEAD_DIM)
IDX_SCALE = 1.0 / math.sqrt(IDX_DIM * IDX_HEADS)
ALIBI_SLOPES = tuple(2.0 ** (-8.0 * (i + 1) / DSA_HEADS) for i in range(DSA_HEADS))
POS_SPLIT = 64
POS_TERMS = 3
LOG2E = math.log2(math.e)

NEG = -1e30
INT_MIN = -(2 ** 31)
I16 = jnp.int16
I16_MIN = -(2 ** 15)

PROJ_ROWS = 512
POST_ROWS = 512
ATT_TQ = 256
ATT_CH = 256
VMEM_LIMIT = 56 * 1024 * 1024


def _const_spec(shape):
    zeros = (0,) * len(shape)
    return pl.BlockSpec(shape, lambda *_: zeros, pipeline_mode=pl.Buffered(1))


def _dot(a, b):
    return jnp.dot(a, b, preferred_element_type=F32)


def _dot_nt(a, b):
    return lax.dot_general(a, b, (((1,), (1,)), ((), ())), preferred_element_type=F32)


def _rms_norm(x, g):
    return x * lax.rsqrt(jnp.mean(x * x, axis=-1, keepdims=True) + RMS_EPS) * g


def _layer_norm(x, g, b):
    mu = jnp.mean(x, axis=-1, keepdims=True)
    xc = x - mu
    var = jnp.mean(xc * xc, axis=-1, keepdims=True)
    return xc * lax.rsqrt(var + LN_EPS) * g + b


def _sigmoid(x):
    return 1.0 / (1.0 + jnp.exp(-x))


def _bf16_terms(x, n):
    terms = []
    for _ in range(n):
        bits = struct.unpack("<I", struct.pack("<f", x))[0]
        bits = (bits + 0x7FFF + ((bits >> 16) & 1)) & 0xFFFF0000
        term = struct.unpack("<f", struct.pack("<I", bits))[0]
        terms.append(term)
        x -= term
    return terms


def _slab_rows(dtype):
    return SUBLANES * (4 // jnp.dtype(dtype).itemsize)


def _fold_rows(x, op, ways=4):
    slab = _slab_rows(x.dtype)
    parts = [x[j * slab:(j + 1) * slab] for j in range(x.shape[0] // slab)]
    accs = parts[:ways]
    for j in range(ways, len(parts)):
        accs[j % ways] = op(accs[j % ways], parts[j])
    while len(accs) > 1:
        accs = [op(accs[k], accs[k + 1]) for k in range(0, len(accs) - 1, 2)] + accs[len(accs) & ~1:]
    return accs[0]


def _attention(nvis, heads, prep, logits, values_t, o_ref, scratch):
    s_scr, cmax_scr, m_scr, l_scr, acc_scr = scratch
    m_scr[...] = jnp.full(m_scr.shape, NEG, F32)
    l_scr[...] = jnp.zeros(l_scr.shape, F32)
    acc_scr[...] = jnp.zeros(acc_scr.shape, F32)

    def logits_part(c, slot):
        ctx = prep(jnp.asarray(c, jnp.int32))
        for h in range(heads):
            s_t = logits(ctx, h)
            s_scr[h, slot] = s_t
            cmax_scr[h, slot] = _fold_rows(s_t, jnp.maximum)

    def values_part(c, slot):
        for h in range(heads):
            m_prev = m_scr[h]
            m_new = jnp.maximum(m_prev, jnp.max(cmax_scr[h, slot], axis=0, keepdims=True))
            alpha = jnp.exp2(m_prev - m_new)
            p = jnp.exp2(s_scr[h, slot] - m_new)
            l_scr[h] = alpha * l_scr[h] + _fold_rows(p, jnp.add)
            acc_scr[h] = alpha * acc_scr[h] + _dot(values_t(c, h), p.astype(BF16))
            m_scr[h] = m_new

    logits_part(0, 0)

    def step(c, slot):
        values_part(c, slot)
        logits_part(c + 1, 1 - slot)

    def body(pair, _):
        step(2 * pair, 0)
        step(2 * pair + 1, 1)
        return 0

    lax.fori_loop(0, (nvis - 1) // 2, body, 0)

    @pl.when(nvis % 2 == 0)
    def _():
        step(nvis - 2, 0)
        values_part(nvis - 1, 1)

    @pl.when(nvis % 2 == 1)
    def _():
        values_part(nvis - 1, 0)

    norm = lambda h: acc_scr[h] / jnp.sum(l_scr[h], axis=0, keepdims=True)
    for pair in range(heads // 2):
        o_t = jnp.concatenate([norm(2 * pair), norm(2 * pair + 1)], axis=0)
        o_ref[:, pair * LANES:(pair + 1) * LANES] = o_t.T.astype(BF16)


def _attn_scratch(heads, dv, tq, ch):
    return [pltpu.VMEM((heads, 2, ch, tq), F32), pltpu.VMEM((heads, 2, SUBLANES, tq), F32),
            pltpu.VMEM((heads, 1, tq), F32), pltpu.VMEM((heads, SUBLANES, tq), F32),
            pltpu.VMEM((heads, dv, tq), F32)]


def _proj_kernel(x_ref, pos_ref, freq_ref, sign_ref, wc_ref, wkr_ref, wdsa_ref, wvbt_ref, widx_ref,
                 wgate_ref, qn_ref, kvn_ref, wuqm_ref, wuqs_ref, wuk_ref, wuvt_ref,
                 qa_ref, ka_ref, vat_ref, qb_ref, kb_ref, vbt_ref, qi_ref, ki_ref, wi_ref,
                 kpos_ref, gate_ref, *, ch):
    rows = x_ref.shape[0]
    xb = x_ref[...].astype(BF16)
    pos = pos_ref[...]
    ang = pos.astype(F32) * freq_ref[...]
    cos_t = jnp.cos(ang)
    sin_t = jnp.sin(ang) * sign_ref[...]

    c = _dot(xb, wc_ref[...])
    cq = _rms_norm(c[:, :MLA_Q_RANK], qn_ref[...]).astype(BF16)
    ckv = _rms_norm(c[:, MLA_Q_RANK:], kvn_ref[...]).astype(BF16)

    qm = _dot(cq, wuqm_ref[...])
    qs = _dot(cq, wuqs_ref[...])
    cos_q = cos_t * (MLA_SCALE * LOG2E)
    sin_q = sin_t * (MLA_SCALE * LOG2E)
    for h in range(MLA_HEADS):
        sl = slice(h * LANES, (h + 1) * LANES)
        qa_ref[:, sl] = (qm[:, sl] * cos_q + qs[:, sl] * sin_q).astype(BF16)

    kr = _dot(xb, wkr_ref[...])
    kr_rot = kr[:, :LANES] * cos_t + kr[:, LANES:] * sin_t
    kn = _dot(ckv, wuk_ref[...])
    for h in range(MLA_HEADS):
        sl = slice(h * LANES, (h + 1) * LANES)
        ka_ref[:, sl] = (kn[:, sl] + kr_rot).astype(BF16)

    d = _dot(xb, wdsa_ref[...])
    w = DSA_HEADS * DSA_HEAD_DIM
    qb_ref[...] = (d[:, :w] * LOG2E).astype(BF16)
    kb_ref[...] = d[:, w:].astype(BF16)

    va_t = _dot_nt(wuvt_ref[...], ckv).astype(BF16)
    vb_t = _dot_nt(wvbt_ref[...], xb).astype(BF16)
    for k in range(rows // ch):
        vat_ref[k] = va_t[:, k * ch:(k + 1) * ch]
        vbt_ref[k] = vb_t[:, k * ch:(k + 1) * ch]

    ix = _dot(xb, widx_ref[...])
    qi_ref[...] = ix[:, :2 * LANES].astype(BF16)
    ki_ref[...] = ix[:, 2 * LANES:3 * LANES].astype(BF16)
    wi_ref[...] = ix[:, 3 * LANES:]

    lane = lax.broadcasted_iota(jnp.int32, (rows, LANES), 1)
    hi = (pos >> (POS_SPLIT.bit_length() - 1)).astype(F32)
    lo = (pos & (POS_SPLIT - 1)).astype(F32)
    kpos_ref[...] = jnp.where(lane < POS_TERMS, hi,
                              jnp.where(lane < 2 * POS_TERMS, lo, 0.0)).astype(BF16)

    gate_ref[...] = _dot(xb, wgate_ref[...]).astype(BF16)


def _proj_call(x2, pos2, freq, sign, weights, rows, ch):
    n = x2.shape[0]
    (wc, wkr, wdsa, wvbt, widx, wgate, qn, kvn, wuqm, wuqs, wuk, wuvt) = weights
    row_spec = lambda wdt: pl.BlockSpec((rows, wdt), lambda i: (i, 0))
    row_out = lambda wdt, dt: (row_spec(wdt), jax.ShapeDtypeStruct((n, wdt), dt))
    t_out = lambda wdt: (pl.BlockSpec((rows // ch, wdt, ch), lambda i: (i, 0, 0)),
                         jax.ShapeDtypeStruct((n // ch, wdt, ch), BF16))
    outs = [row_out(8 * LANES, BF16), row_out(8 * LANES, BF16), t_out(MLA_HEADS * MLA_V),
            row_out(512, BF16), row_out(512, BF16), t_out(DSA_HEADS * DSA_HEAD_DIM),
            row_out(2 * LANES, BF16), row_out(LANES, BF16), row_out(LANES, F32),
            row_out(LANES, BF16), row_out(2 * D_MODEL, BF16)]
    consts = (freq, sign, wc, wkr, wdsa, wvbt, widx, wgate, qn, kvn, wuqm, wuqs, wuk, wuvt)
    return pl.pallas_call(
        functools.partial(_proj_kernel, ch=ch),
        grid=(n // rows,),
        in_specs=[row_spec(D_MODEL), row_spec(1)] + [_const_spec(a.shape) for a in consts],
        out_specs=[o[0] for o in outs],
        out_shape=[o[1] for o in outs],
        compiler_params=pltpu.CompilerParams(
            dimension_semantics=("parallel",), vmem_limit_bytes=VMEM_LIMIT),
        name="proj",
    )(x2, pos2, *consts)


def _mla_kernel(q_ref, k_ref, vt_ref, o_ref, *scratch, tq, ch):
    i = pl.program_id(1)
    nvis = ((i + 1) * tq + ch - 1) // ch
    key = lax.broadcasted_iota(jnp.int32, (ch, tq), 0)
    query = i * tq + lax.broadcasted_iota(jnp.int32, (ch, tq), 1)

    def prep(c):
        ks = pl.multiple_of(c * ch, ch)
        return ks, jnp.where(key + ks <= query, 0.0, NEG)

    def logits(ctx, h):
        ks, causal = ctx
        hsl = slice(h * LANES, (h + 1) * LANES)
        return _dot_nt(k_ref[pl.ds(ks, ch), hsl], q_ref[:, hsl]) + causal

    def values_t(c, h):
        return vt_ref[c, h * MLA_V:(h + 1) * MLA_V, :]

    _attention(nvis, MLA_HEADS, prep, logits, values_t, o_ref, scratch)


def _mla_call(qa, ka, vat, tq, ch):
    b, s, _ = qa.shape
    wv = MLA_HEADS * MLA_V
    return pl.pallas_call(
        functools.partial(_mla_kernel, tq=tq, ch=ch),
        grid=(b, s // tq),
        in_specs=[
            pl.BlockSpec((None, tq, 8 * LANES), lambda bi, i: (bi, i, 0)),
            pl.BlockSpec((None, s, 8 * LANES), lambda bi, i: (bi, 0, 0)),
            pl.BlockSpec((None, s // ch, wv, ch), lambda bi, i: (bi, 0, 0, 0)),
        ],
        out_specs=pl.BlockSpec((None, tq, wv), lambda bi, i: (bi, i, 0)),
        out_shape=jax.ShapeDtypeStruct((b, s, wv), BF16),
        scratch_shapes=_attn_scratch(MLA_HEADS, MLA_V, tq, ch),
        compiler_params=pltpu.CompilerParams(
            dimension_semantics=("parallel", "arbitrary"), vmem_limit_bytes=VMEM_LIMIT),
        name="mla",
    )(qa, ka, vat)


def _dsa_kernel(qb_ref, kb_ref, vbt_ref, qi_ref, ki_ref, wi_ref, kpos_ref, o_ref,
                qih_scr, wih_scr, key_scr, hi_scr, lo_scr, bias_scr, jsel_scr, *scratch,
                tq, ch, topk, seq):
    i = pl.program_id(1)
    nvis = ((i + 1) * tq + ch - 1) // ch
    key_idx = lax.broadcasted_iota(jnp.int32, (ch, tq), 0)
    query = i * tq + lax.broadcasted_iota(jnp.int32, (ch, tq), 1)
    lane = lax.broadcasted_iota(jnp.int32, (tq, LANES), 1)

    qi = qi_ref[...]
    wi_t = wi_ref[...].T * IDX_SCALE
    for h in range(IDX_HEADS):
        group = qi[:, (h // 4) * LANES:(h // 4 + 1) * LANES]
        lo = IDX_DIM * (h % 4)
        keep = jnp.where(lane >= lo, jnp.where(lane < lo + IDX_DIM, 1.0, 0.0), 0.0).astype(BF16)
        qih_scr[h] = group * keep
        wih_scr[h] = wi_t[h:h + 1, :]

    def score_chunk(c):
        ks = pl.multiple_of(c * ch, ch)
        kic = ki_ref[pl.ds(ks, ch), :]
        sc = jnp.zeros((ch, tq), F32)
        for h in range(IDX_HEADS):
            sc = sc + wih_scr[h] * jnp.maximum(_dot_nt(kic, qih_scr[h]), 0.0)
        sc = jnp.where(sc == 0.0, 0.0, sc)
        bits = pltpu.bitcast(sc, jnp.int32)
        key = bits ^ ((bits >> 31) & 0x7FFFFFFF)
        key = jnp.where(key_idx + ks <= query, key, INT_MIN)
        key_scr[c] = key
        hi_scr[c] = (key >> 16).astype(I16)
        lo_scr[c] = (((key ^ 0x8000) << 16) >> 16).astype(I16)

    def score_pair(pair, _):
        score_chunk(2 * pair)
        score_chunk(2 * pair + 1)
        return 0

    lax.fori_loop(0, nvis // 2, score_pair, 0)

    @pl.when(nvis % 2 == 1)
    def _():
        score_chunk(nvis - 1)

    def count(pred):
        def body(c, acc):
            return acc + _fold_rows(pred(key_scr[c], c), jnp.add)
        acc = lax.fori_loop(0, nvis, body, jnp.zeros((SUBLANES, tq), F32))
        return jnp.sum(acc, axis=0, keepdims=True)

    def search16(half_scr, n_bits, target, cnt_all):
        def count16(cand):
            cand16 = cand.astype(I16)
            def body(c, acc):
                ones = jnp.where(half_scr[c] >= cand16, jnp.int16(1), jnp.int16(0))
                return acc + _fold_rows(ones, jnp.add)
            acc = lax.fori_loop(0, nvis, body, jnp.zeros((_slab_rows(I16), tq), I16))
            return jnp.sum(acc.astype(F32), axis=0, keepdims=True)

        def body(it, carry):
            lo, cnt_lo, cnt_hi = carry
            cand = lo + jnp.left_shift(jnp.int32(1), 15 - it)
            cnt = count16(cand)
            ok = cnt >= target
            return jnp.where(ok, cand, lo), jnp.where(ok, cnt, cnt_lo), jnp.where(ok, cnt_hi, cnt)

        return lax.fori_loop(0, n_bits, body,
                             (jnp.full((1, tq), I16_MIN, jnp.int32), cnt_all, jnp.zeros((1, tq), F32)))

    n_bits = jnp.where((i + 1) * tq <= topk, 0, 16)
    all_keys = jnp.zeros((1, tq), F32) + (nvis * ch).astype(F32)
    tau_hi, cnt_ge_hi, cnt_gt_hi = search16(hi_scr, n_bits, float(topk), all_keys)
    tau_hi16 = tau_hi.astype(I16)

    def bucket_body(c, _):
        lo_scr[c] = jnp.where(hi_scr[c] == tau_hi16, lo_scr[c], jnp.int16(I16_MIN))
        return 0

    lax.fori_loop(0, nvis, bucket_body, 0)
    tau_lo, cnt_ge_lo, cnt_gt_lo = search16(lo_scr, n_bits, topk - cnt_gt_hi, cnt_ge_hi - cnt_gt_hi)
    tau = (tau_hi << 16) + (tau_lo - I16_MIN)
    cnt_ge = cnt_gt_hi + cnt_ge_lo
    cnt_gt = cnt_gt_hi + cnt_gt_lo
    has_tau = tau > INT_MIN
    need = topk - cnt_gt
    jsel_scr[...] = jnp.where(has_tau, seq, -1)
    excess = jnp.where(has_tau, jnp.where(cnt_ge - cnt_gt > need, 1.0, 0.0), 0.0)

    @pl.when(jnp.max(excess) > 0.0)
    def _():
        def idx_body(it, lo):
            cand = lo + jnp.left_shift(jnp.int32(1), seq.bit_length() - 2 - it)
            cnt = count(lambda k, c: jnp.where(
                k == tau, jnp.where(key_idx + c * ch < cand, 1.0, 0.0), 0.0))
            return jnp.where(cnt < need, cand, lo)
        last = lax.fori_loop(0, seq.bit_length() - 1, idx_body, jnp.zeros((1, tq), jnp.int32))
        jsel_scr[...] = jnp.where(has_tau, last, -1)

    jsel = jsel_scr[...]

    def bias_body(c, _):
        k = key_scr[c]
        tie = jnp.where(key_idx + c * ch <= jsel, 0.0, NEG)
        bias_scr[c] = jnp.where(k > tau, 0.0, jnp.where(k == tau, tie, NEG))
        return 0

    lax.fori_loop(0, nvis, bias_body, 0)

    q_heads = []
    for h in range(DSA_HEADS):
        qpair = qb_ref[:, (h // 2) * LANES:(h // 2 + 1) * LANES]
        if h % 2 == 0:
            keep = jnp.where(lane < DSA_HEAD_DIM, 1.0, 0.0).astype(BF16)
        else:
            keep = jnp.where(lane >= DSA_HEAD_DIM, 1.0, 0.0).astype(BF16)
        terms = _bf16_terms(ALIBI_SLOPES[h] * LOG2E, POS_TERMS)
        q_pos = jnp.zeros((tq, LANES), F32)
        for j, term in enumerate(terms):
            q_pos = jnp.where(lane == j, POS_SPLIT * term, jnp.where(lane == POS_TERMS + j, term, q_pos))
        q_heads.append(jnp.concatenate([qpair * keep, q_pos.astype(BF16)], axis=1))

    def prep(c):
        ks = pl.multiple_of(c * ch, ch)
        return ks, bias_scr[c], kpos_ref[pl.ds(ks, ch), :]

    def logits(ctx, h):
        ks, sel, kpos = ctx
        psl = slice((h // 2) * LANES, (h // 2 + 1) * LANES)
        k_aug = jnp.concatenate([kb_ref[pl.ds(ks, ch), psl], kpos], axis=1)
        return _dot_nt(k_aug, q_heads[h]) + sel

    def values_t(c, h):
        return vbt_ref[c, h * DSA_HEAD_DIM:(h + 1) * DSA_HEAD_DIM, :]

    _attention(nvis, DSA_HEADS, prep, logits, values_t, o_ref, scratch)


def _dsa_call(qb, kb, vbt, qi, ki, wi, kpos, tq, ch):
    b, s, w = qb.shape
    nch = s // ch
    topk = min(TOPK_MAX, s // 4)
    full = lambda wdt: pl.BlockSpec((None, s, wdt), lambda bi, i: (bi, 0, 0))
    tile = lambda wdt: pl.BlockSpec((None, tq, wdt), lambda bi, i: (bi, i, 0))
    return pl.pallas_call(
        functools.partial(_dsa_kernel, tq=tq, ch=ch, topk=topk, seq=s),
        grid=(b, s // tq),
        in_specs=[tile(w), full(w), pl.BlockSpec((None, nch, w, ch), lambda bi, i: (bi, 0, 0, 0)),
                  tile(2 * LANES), full(LANES), tile(LANES), full(LANES)],
        out_specs=tile(w),
        out_shape=jax.ShapeDtypeStruct((b, s, w), BF16),
        scratch_shapes=[pltpu.VMEM((IDX_HEADS, tq, LANES), BF16), pltpu.VMEM((IDX_HEADS, 1, tq), F32),
                        pltpu.VMEM((nch, ch, tq), jnp.int32), pltpu.VMEM((nch, ch, tq), I16),
                        pltpu.VMEM((nch, ch, tq), I16), pltpu.VMEM((nch, ch, tq), F32),
                        pltpu.VMEM((1, tq), jnp.int32)]
        + _attn_scratch(DSA_HEADS, DSA_HEAD_DIM, tq, ch),
        compiler_params=pltpu.CompilerParams(
            dimension_semantics=("parallel", "arbitrary"), vmem_limit_bytes=VMEM_LIMIT),
        name="dsa",
    )(qb, kb, vbt, qi, ki, wi, kpos)


def _post_kernel(x_ref, oa_ref, ob_ref, gate_ref, wa_ref, wb_ref, wo_ref, g1_ref, b1_ref,
                 wfi_ref, wfd_ref, g2_ref, b2_ref, out_ref):
    ya = _dot(oa_ref[...], wa_ref[...])
    yb = _dot(ob_ref[...], wb_ref[...])
    mixed = (_sigmoid(gate_ref[:, :D_MODEL].astype(F32)) * ya
             + _sigmoid(gate_ref[:, D_MODEL:].astype(F32)) * yb)
    mix_out = _dot(mixed.astype(BF16), wo_ref[...])
    h1 = _layer_norm(ALPHA * x_ref[...] + mix_out, g1_ref[...], b1_ref[...])
    gu = _dot(h1.astype(BF16), wfi_ref[...])
    gate = gu[:, :D_FF]
    act = gate * _sigmoid(gate) * gu[:, D_FF:]
    f = _dot(act.astype(BF16), wfd_ref[...])
    out_ref[...] = _layer_norm(ALPHA * h1 + f, g2_ref[...], b2_ref[...])


def _post_call(x2, oa, ob, gates, weights, rows):
    n = x2.shape[0]
    row_spec = lambda wdt: pl.BlockSpec((rows, wdt), lambda i: (i, 0))
    return pl.pallas_call(
        _post_kernel,
        grid=(n // rows,),
        in_specs=[row_spec(D_MODEL), row_spec(oa.shape[1]), row_spec(ob.shape[1]),
                  row_spec(2 * D_MODEL)] + [_const_spec(a.shape) for a in weights],
        out_specs=row_spec(D_MODEL),
        out_shape=jax.ShapeDtypeStruct((n, D_MODEL), F32),
        compiler_params=pltpu.CompilerParams(
            dimension_semantics=("parallel",), vmem_limit_bytes=VMEM_LIMIT),
        name="post",
    )(x2, oa, ob, gates, *weights)


def _pad_cols(w, width):
    return jnp.pad(w, ((0, 0), (0, width - w.shape[1])))


def _prep_proj_weights(w_in, q_norm, kv_norm, w_uq, w_ukv):
    sizes = (MLA_Q_RANK, MLA_KV_RANK, MLA_ROPE, 512, 512, 512, IDX_HEADS * IDX_DIM, IDX_DIM,
             IDX_HEADS, D_MODEL, D_MODEL)
    parts, start = [], 0
    for n in sizes:
        parts.append(w_in[:, start:start + n])
        start += n
    w_cq, w_ckv, w_kr, w_qb, w_kb, w_vb, w_qi, w_ki, w_wi, w_ga, w_gb = parts
    half = MLA_ROPE // 2
    d = w_in.shape[0]

    def rope_group(first, second):
        return jnp.concatenate([jnp.zeros((d, MLA_NOPE), F32), first, second,
                                jnp.zeros((d, LANES - MLA_QK), F32)], axis=1)

    wc = jnp.concatenate([w_cq, w_ckv], axis=1)
    wkr = jnp.concatenate([rope_group(w_kr[:, :half], w_kr[:, half:]),
                           rope_group(w_kr[:, half:], w_kr[:, :half])], axis=1)
    wdsa = jnp.concatenate([w_qb * DSA_SCALE, w_kb], axis=1)
    widx = jnp.concatenate([w_qi, w_ki, w_ki, w_ki, w_ki, _pad_cols(w_wi, LANES)], axis=1)
    wgate = jnp.concatenate([w_ga, w_gb], axis=1)

    r = w_uq.shape[0]
    uq = w_uq.reshape(r, MLA_HEADS, MLA_QK)
    zq = lambda n: jnp.zeros((r, MLA_HEADS, n), F32)
    wuqm = jnp.concatenate([uq, zq(LANES - MLA_QK)], axis=2).reshape(r, MLA_HEADS * LANES)
    wuqs = jnp.concatenate([zq(MLA_NOPE), uq[:, :, MLA_NOPE + half:], uq[:, :, MLA_NOPE:MLA_NOPE + half],
                            zq(LANES - MLA_QK)], axis=2).reshape(r, MLA_HEADS * LANES)
    ukv = w_ukv.reshape(r, MLA_HEADS, MLA_NOPE + MLA_V)
    wuk = jnp.concatenate([ukv[:, :, :MLA_NOPE], zq(LANES - MLA_NOPE)], axis=2).reshape(r, MLA_HEADS * LANES)
    wuvt = ukv[:, :, MLA_NOPE:].reshape(r, MLA_HEADS * MLA_V).T

    bf = lambda a: a.astype(BF16)
    return (bf(wc), bf(wkr), bf(wdsa), bf(w_vb.T), bf(widx), bf(wgate), q_norm.reshape(1, -1),
            kv_norm.reshape(1, -1), bf(wuqm), bf(wuqs), bf(wuk), bf(wuvt))


def _rope_tables():
    half = MLA_ROPE // 2
    inv_freq = ROPE_BASE ** (-jnp.arange(half, dtype=F32) / half)
    z = lambda n: jnp.zeros((n,), F32)
    freq = jnp.concatenate([z(MLA_NOPE), inv_freq, inv_freq, z(LANES - MLA_QK)]).reshape(1, LANES)
    sign = jnp.concatenate([z(MLA_NOPE), -jnp.ones((half,), F32), jnp.ones((half,), F32),
                            z(LANES - MLA_QK)]).reshape(1, LANES)
    return freq, sign


def kernel(x, positions, w_in, mla_q_norm, mla_kv_norm, mla_w_uq, mla_w_ukv, w_branch_a, w_branch_b,
           w_out, ln1_g, ln1_b, ffn_w_in, ffn_w_down, ln2_g, ln2_b):
    b, s, d = x.shape
    assert w_in.shape[0] == DEPTH and d == D_MODEL
    assert s & (s - 1) == 0 and s % ATT_TQ == 0 and s % ATT_CH == 0 and PROJ_ROWS % ATT_CH == 0
    assert (b * s) % PROJ_ROWS == 0 and s <= POS_SPLIT * 256
    n = b * s
    h = x.reshape(n, d)
    pos2 = positions.reshape(n, 1)
    freq, sign = _rope_tables()
    for l in range(DEPTH):
        pw = _prep_proj_weights(w_in[l], mla_q_norm[l], mla_kv_norm[l], mla_w_uq[l], mla_w_ukv[l])
        qa, ka, vat, qb, kb, vbt, qi, ki, wi, kpos, gates = _proj_call(
            h, pos2, freq, sign, pw, PROJ_ROWS, ATT_CH)
        r3 = lambda a: a.reshape(b, s, a.shape[1])
        t4 = lambda a: a.reshape(b, s // ATT_CH, a.shape[1], ATT_CH)
        o_a = _mla_call(r3(qa), r3(ka), t4(vat), ATT_TQ, ATT_CH)
        o_b = _dsa_call(r3(qb), r3(kb), t4(vbt), r3(qi), r3(ki), r3(wi), r3(kpos), ATT_TQ, ATT_CH)
        row = lambda v: v.reshape(1, -1)
        post_w = (w_branch_a[l].astype(BF16), w_branch_b[l].astype(BF16), w_out[l].astype(BF16),
                  row(ln1_g[l]), row(ln1_b[l]), ffn_w_in[l].astype(BF16), ffn_w_down[l].astype(BF16),
                  row(ln2_g[l]), row(ln2_b[l]))
        h = _post_call(h, o_a.reshape(n, -1), o_b.reshape(n, -1), gates, post_w, POST_ROWS)
    return h.reshape(b, s, d)
```

```python
import functools
import math
import struct

import jax
import jax.numpy as jnp
from jax import lax
from jax.experimental import pallas as pl
from jax.experimental.pallas import tpu as pltpu

F32 = jnp.float32
BF16 = jnp.bfloat16

D_MODEL = 1024
MLA_HEADS = 8
MLA_Q_RANK = 256
MLA_KV_RANK = 256
MLA_NOPE = 64
MLA_ROPE = 32
MLA_V = 64
ROPE_BASE = 10000.0
DSA_HEADS = 8
DSA_HEAD_DIM = 64
IDX_HEADS = 8
IDX_DIM = 32
TOPK_MAX = 256
D_FF = 2816
DEPTH = 1
ALPHA = (2 * DEPTH) ** 0.25
LN_EPS = 1e-5
RMS_EPS = 1e-6

LANES = 128
SUBLANES = 8
MLA_QK = MLA_NOPE + MLA_ROPE
MLA_SCALE = 1.0 / math.sqrt(MLA_QK)
DSA_SCALE = 1.0 / math.sqrt(DSA_HEAD_DIM)
IDX_SCALE = 1.0 / math.sqrt(IDX_DIM * IDX_HEADS)
ALIBI_SLOPES = tuple(2.0 ** (-8.0 * (i + 1) / DSA_HEADS) for i in range(DSA_HEADS))
POS_SPLIT = 64
POS_TERMS = 3
LOG2E = math.log2(math.e)

NEG = -1e30
INT_MIN = -(2 ** 31)
I16 = jnp.int16
I16_MIN = -(2 ** 15)

PROJ_ROWS = 512
POST_ROWS = 512
ATT_TQ = 256
ATT_CH = 256
VMEM_LIMIT = 56 * 1024 * 1024


def _const_spec(shape):
    zeros = (0,) * len(shape)
    return pl.BlockSpec(shape, lambda *_: zeros, pipeline_mode=pl.Buffered(1))


def _dot(a, b):
    return jnp.dot(a, b, preferred_element_type=F32)


def _dot_nt(a, b):
    return lax.dot_general(a, b, (((1,), (1,)), ((), ())), preferred_element_type=F32)


def _rms_norm(x, g):
    return x * lax.rsqrt(jnp.mean(x * x, axis=-1, keepdims=True) + RMS_EPS) * g


def _layer_norm(x, g, b):
    mu = jnp.mean(x, axis=-1, keepdims=True)
    xc = x - mu
    var = jnp.mean(xc * xc, axis=-1, keepdims=True)
    return xc * lax.rsqrt(var + LN_EPS) * g + b


def _sigmoid(x):
    return 1.0 / (1.0 + jnp.exp(-x))


def _bf16_terms(x, n):
    terms = []
    for _ in range(n):
        bits = struct.unpack("<I", struct.pack("<f", x))[0]
        bits = (bits + 0x7FFF + ((bits >> 16) & 1)) & 0xFFFF0000
        term = struct.unpack("<f", struct.pack("<I", bits))[0]
        terms.append(term)
        x -= term
    return terms


def _slab_rows(dtype):
    return SUBLANES * (4 // jnp.dtype(dtype).itemsize)


def _fold_rows(x, op, ways=4):
    slab = _slab_rows(x.dtype)
    parts = [x[j * slab:(j + 1) * slab] for j in range(x.shape[0] // slab)]
    accs = parts[:ways]
    for j in range(ways, len(parts)):
        accs[j % ways] = op(accs[j % ways], parts[j])
    while len(accs) > 1:
        accs = [op(accs[k], accs[k + 1]) for k in range(0, len(accs) - 1, 2)] + accs[len(accs) & ~1:]
    return accs[0]


def _attention(nvis, heads, prep, logits, values_t, o_ref, scratch):
    s_scr, cmax_scr, m_scr, acc_scr = scratch
    dv = acc_scr.shape[1] - _slab_rows(BF16)
    ones = jnp.ones((_slab_rows(BF16), s_scr.shape[2]), BF16)
    m_scr[...] = jnp.full(m_scr.shape, NEG, F32)
    acc_scr[...] = jnp.zeros(acc_scr.shape, F32)

    def logits_part(c, slot):
        ctx = prep(jnp.asarray(c, jnp.int32))
        for h in range(heads):
            s_t = logits(ctx, h)
            s_scr[h, slot] = s_t
            cmax_scr[h, slot] = _fold_rows(s_t, jnp.maximum)

    def values_part(c, slot):
        for h in range(heads):
            m_prev = m_scr[h]
            m_new = jnp.maximum(m_prev, jnp.max(cmax_scr[h, slot], axis=0, keepdims=True))
            alpha = jnp.exp2(m_prev - m_new)
            p = jnp.exp2(s_scr[h, slot] - m_new)
            v_aug = jnp.concatenate([values_t(c, h), ones], axis=0)
            acc_scr[h] = alpha * acc_scr[h] + _dot(v_aug, p.astype(BF16))
            m_scr[h] = m_new

    logits_part(0, 0)

    def step(c, slot):
        values_part(c, slot)
        logits_part(c + 1, 1 - slot)

    def body(pair, _):
        step(2 * pair, 0)
        step(2 * pair + 1, 1)
        return 0

    lax.fori_loop(0, (nvis - 1) // 2, body, 0)

    @pl.when(nvis % 2 == 0)
    def _():
        step(nvis - 2, 0)
        values_part(nvis - 1, 1)

    @pl.when(nvis % 2 == 1)
    def _():
        values_part(nvis - 1, 0)

    norm = lambda h: acc_scr[h, :dv] / acc_scr[h, dv:dv + 1]
    for pair in range(heads // 2):
        o_t = jnp.concatenate([norm(2 * pair), norm(2 * pair + 1)], axis=0)
        o_ref[:, pair * LANES:(pair + 1) * LANES] = o_t.T.astype(BF16)


def _attn_scratch(heads, dv, tq, ch):
    return [pltpu.VMEM((heads, 2, ch, tq), F32), pltpu.VMEM((heads, 2, SUBLANES, tq), F32),
            pltpu.VMEM((heads, 1, tq), F32), pltpu.VMEM((heads, dv + _slab_rows(BF16), tq), F32)]


def _proj_kernel(x_ref, pos_ref, freq_ref, sign_ref, wc_ref, wkr_ref, wdsa_ref, wvbt_ref, widx_ref,
                 wgate_ref, qn_ref, kvn_ref, wuqm_ref, wuqs_ref, wuk_ref, wuvt_ref,
                 qa_ref, ka_ref, vat_ref, qb_ref, kb_ref, vbt_ref, qi_ref, ki_ref, wi_ref,
                 kpos_ref, gate_ref, *, ch):
    rows = x_ref.shape[0]
    xb = x_ref[...].astype(BF16)
    pos = pos_ref[...]
    ang = pos.astype(F32) * freq_ref[...]
    cos_t = jnp.cos(ang)
    sin_t = jnp.sin(ang) * sign_ref[...]

    c = _dot(xb, wc_ref[...])
    cq = _rms_norm(c[:, :MLA_Q_RANK], qn_ref[...]).astype(BF16)
    ckv = _rms_norm(c[:, MLA_Q_RANK:], kvn_ref[...]).astype(BF16)

    qm = _dot(cq, wuqm_ref[...])
    qs = _dot(cq, wuqs_ref[...])
    cos_q = cos_t * (MLA_SCALE * LOG2E)
    sin_q = sin_t * (MLA_SCALE * LOG2E)
    for h in range(MLA_HEADS):
        sl = slice(h * LANES, (h + 1) * LANES)
        qa_ref[:, sl] = (qm[:, sl] * cos_q + qs[:, sl] * sin_q).astype(BF16)

    kr = _dot(xb, wkr_ref[...])
    kr_rot = kr[:, :LANES] * cos_t + kr[:, LANES:] * sin_t
    kn = _dot(ckv, wuk_ref[...])
    for h in range(MLA_HEADS):
        sl = slice(h * LANES, (h + 1) * LANES)
        ka_ref[:, sl] = (kn[:, sl] + kr_rot).astype(BF16)

    d = _dot(xb, wdsa_ref[...])
    w = DSA_HEADS * DSA_HEAD_DIM
    qb_ref[...] = (d[:, :w] * LOG2E).astype(BF16)
    kb_ref[...] = d[:, w:].astype(BF16)

    va_t = _dot_nt(wuvt_ref[...], ckv).astype(BF16)
    vb_t = _dot_nt(wvbt_ref[...], xb).astype(BF16)
    for k in range(rows // ch):
        vat_ref[k] = va_t[:, k * ch:(k + 1) * ch]
        vbt_ref[k] = vb_t[:, k * ch:(k + 1) * ch]

    ix = _dot(xb, widx_ref[...])
    qi_ref[...] = ix[:, :2 * LANES].astype(BF16)
    ki_ref[...] = ix[:, 2 * LANES:3 * LANES].astype(BF16)
    wi_ref[...] = ix[:, 3 * LANES:]

    lane = lax.broadcasted_iota(jnp.int32, (rows, LANES), 1)
    hi = (pos >> (POS_SPLIT.bit_length() - 1)).astype(F32)
    lo = (pos & (POS_SPLIT - 1)).astype(F32)
    kpos_ref[...] = jnp.where(lane < POS_TERMS, hi,
                              jnp.where(lane < 2 * POS_TERMS, lo, 0.0)).astype(BF16)

    gate_ref[...] = _dot(xb, wgate_ref[...]).astype(BF16)


def _proj_call(x2, pos2, freq, sign, weights, rows, ch):
    n = x2.shape[0]
    (wc, wkr, wdsa, wvbt, widx, wgate, qn, kvn, wuqm, wuqs, wuk, wuvt) = weights
    row_spec = lambda wdt: pl.BlockSpec((rows, wdt), lambda i: (i, 0))
    row_out = lambda wdt, dt: (row_spec(wdt), jax.ShapeDtypeStruct((n, wdt), dt))
    t_out = lambda wdt: (pl.BlockSpec((rows // ch, wdt, ch), lambda i: (i, 0, 0)),
                         jax.ShapeDtypeStruct((n // ch, wdt, ch), BF16))
    outs = [row_out(8 * LANES, BF16), row_out(8 * LANES, BF16), t_out(MLA_HEADS * MLA_V),
            row_out(512, BF16), row_out(512, BF16), t_out(DSA_HEADS * DSA_HEAD_DIM),
            row_out(2 * LANES, BF16), row_out(LANES, BF16), row_out(LANES, F32),
            row_out(LANES, BF16), row_out(2 * D_MODEL, BF16)]
    consts = (freq, sign, wc, wkr, wdsa, wvbt, widx, wgate, qn, kvn, wuqm, wuqs, wuk, wuvt)
    return pl.pallas_call(
        functools.partial(_proj_kernel, ch=ch),
        grid=(n // rows,),
        in_specs=[row_spec(D_MODEL), row_spec(1)] + [_const_spec(a.shape) for a in consts],
        out_specs=[o[0] for o in outs],
        out_shape=[o[1] for o in outs],
        compiler_params=pltpu.CompilerParams(
            dimension_semantics=("parallel",), vmem_limit_bytes=VMEM_LIMIT),
        name="proj",
    )(x2, pos2, *consts)


def _mla_kernel(q_ref, k_ref, vt_ref, o_ref, *scratch, tq, ch):
    i = pl.program_id(1)
    nvis = ((i + 1) * tq + ch - 1) // ch
    key = lax.broadcasted_iota(jnp.int32, (ch, tq), 0)
    query = i * tq + lax.broadcasted_iota(jnp.int32, (ch, tq), 1)

    def prep(c):
        ks = pl.multiple_of(c * ch, ch)
        return ks, jnp.where(key + ks <= query, 0.0, NEG)

    def logits(ctx, h):
        ks, causal = ctx
        hsl = slice(h * LANES, (h + 1) * LANES)
        return _dot_nt(k_ref[pl.ds(ks, ch), hsl], q_ref[:, hsl]) + causal

    def values_t(c, h):
        return vt_ref[c, h * MLA_V:(h + 1) * MLA_V, :]

    _attention(nvis, MLA_HEADS, prep, logits, values_t, o_ref, scratch)


def _mla_call(qa, ka, vat, tq, ch):
    b, s, _ = qa.shape
    wv = MLA_HEADS * MLA_V
    return pl.pallas_call(
        functools.partial(_mla_kernel, tq=tq, ch=ch),
        grid=(b, s // tq),
        in_specs=[
            pl.BlockSpec((None, tq, 8 * LANES), lambda bi, i: (bi, i, 0)),
            pl.BlockSpec((None, s, 8 * LANES), lambda bi, i: (bi, 0, 0)),
            pl.BlockSpec((None, s // ch, wv, ch), lambda bi, i: (bi, 0, 0, 0)),
        ],
        out_specs=pl.BlockSpec((None, tq, wv), lambda bi, i: (bi, i, 0)),
        out_shape=jax.ShapeDtypeStruct((b, s, wv), BF16),
        scratch_shapes=_attn_scratch(MLA_HEADS, MLA_V, tq, ch),
        compiler_params=pltpu.CompilerParams(
            dimension_semantics=("parallel", "arbitrary"), vmem_limit_bytes=VMEM_LIMIT),
        name="mla",
    )(qa, ka, vat)


def _dsa_kernel(qb_ref, kb_ref, vbt_ref, qi_ref, ki_ref, wi_ref, kpos_ref, o_ref,
                qih_scr, wih_scr, key_scr, hi_scr, lo_scr, bias_scr, jsel_scr, *scratch,
                tq, ch, topk, seq):
    i = pl.program_id(1)
    nvis = ((i + 1) * tq + ch - 1) // ch
    key_idx = lax.broadcasted_iota(jnp.int32, (ch, tq), 0)
    query = i * tq + lax.broadcasted_iota(jnp.int32, (ch, tq), 1)
    lane = lax.broadcasted_iota(jnp.int32, (tq, LANES), 1)

    qi = qi_ref[...]
    wi_t = wi_ref[...].T * IDX_SCALE
    for h in range(IDX_HEADS):
        group = qi[:, (h // 4) * LANES:(h // 4 + 1) * LANES]
        lo = IDX_DIM * (h % 4)
        keep = jnp.where(lane >= lo, jnp.where(lane < lo + IDX_DIM, 1.0, 0.0), 0.0).astype(BF16)
        qih_scr[h] = group * keep
        wih_scr[h] = wi_t[h:h + 1, :]

    def score_chunk(c):
        ks = pl.multiple_of(c * ch, ch)
        kic = ki_ref[pl.ds(ks, ch), :]
        sc = jnp.zeros((ch, tq), F32)
        for h in range(IDX_HEADS):
            sc = sc + wih_scr[h] * jnp.maximum(_dot_nt(kic, qih_scr[h]), 0.0)
        sc = jnp.where(sc == 0.0, 0.0, sc)
        bits = pltpu.bitcast(sc, jnp.int32)
        key = bits ^ ((bits >> 31) & 0x7FFFFFFF)
        key = jnp.where(key_idx + ks <= query, key, INT_MIN)
        key_scr[c] = key
        hi_scr[c] = (key >> 16).astype(I16)
        lo_scr[c] = (((key ^ 0x8000) << 16) >> 16).astype(I16)

    def score_pair(pair, _):
        score_chunk(2 * pair)
        score_chunk(2 * pair + 1)
        return 0

    lax.fori_loop(0, nvis // 2, score_pair, 0)

    @pl.when(nvis % 2 == 1)
    def _():
        score_chunk(nvis - 1)

    def count(pred):
        def body(c, acc):
            return acc + _fold_rows(pred(key_scr[c], c), jnp.add)
        acc = lax.fori_loop(0, nvis, body, jnp.zeros((SUBLANES, tq), F32))
        return jnp.sum(acc, axis=0, keepdims=True)

    def search16(half_scr, n_bits, target, cnt_all):
        def count16(cand):
            cand16 = cand.astype(I16)
            def body(c, acc):
                ones = jnp.where(half_scr[c] >= cand16, jnp.int16(1), jnp.int16(0))
                return acc + _fold_rows(ones, jnp.add)
            acc = lax.fori_loop(0, nvis, body, jnp.zeros((_slab_rows(I16), tq), I16))
            return jnp.sum(acc.astype(F32), axis=0, keepdims=True)

        def body(it, carry):
            lo, cnt_lo, cnt_hi = carry
            cand = lo + jnp.left_shift(jnp.int32(1), 15 - it)
            cnt = count16(cand)
            ok = cnt >= target
            return jnp.where(ok, cand, lo), jnp.where(ok, cnt, cnt_lo), jnp.where(ok, cnt_hi, cnt)

        return lax.fori_loop(0, n_bits, body,
                             (jnp.full((1, tq), I16_MIN, jnp.int32), cnt_all, jnp.zeros((1, tq), F32)))

    n_bits = jnp.where((i + 1) * tq <= topk, 0, 16)
    all_keys = jnp.zeros((1, tq), F32) + (nvis * ch).astype(F32)
    tau_hi, cnt_ge_hi, cnt_gt_hi = search16(hi_scr, n_bits, float(topk), all_keys)
    tau_hi16 = tau_hi.astype(I16)

    def bucket_body(c, _):
        lo_scr[c] = jnp.where(hi_scr[c] == tau_hi16, lo_scr[c], jnp.int16(I16_MIN))
        return 0

    lax.fori_loop(0, nvis, bucket_body, 0)
    tau_lo, cnt_ge_lo, cnt_gt_lo = search16(lo_scr, n_bits, topk - cnt_gt_hi, cnt_ge_hi - cnt_gt_hi)
    tau = (tau_hi << 16) + (tau_lo - I16_MIN)
    cnt_ge = cnt_gt_hi + cnt_ge_lo
    cnt_gt = cnt_gt_hi + cnt_gt_lo
    has_tau = tau > INT_MIN
    need = topk - cnt_gt
    jsel_scr[...] = jnp.where(has_tau, seq, -1)
    excess = jnp.where(has_tau, jnp.where(cnt_ge - cnt_gt > need, 1.0, 0.0), 0.0)

    @pl.when(jnp.max(excess) > 0.0)
    def _():
        def idx_body(it, lo):
            cand = lo + jnp.left_shift(jnp.int32(1), seq.bit_length() - 2 - it)
            cnt = count(lambda k, c: jnp.where(
                k == tau, jnp.where(key_idx + c * ch < cand, 1.0, 0.0), 0.0))
            return jnp.where(cnt < need, cand, lo)
        last = lax.fori_loop(0, seq.bit_length() - 1, idx_body, jnp.zeros((1, tq), jnp.int32))
        jsel_scr[...] = jnp.where(has_tau, last, -1)

    jsel = jsel_scr[...]

    def bias_body(c, _):
        k = key_scr[c]
        tie = jnp.where(key_idx + c * ch <= jsel, 0.0, NEG)
        bias_scr[c] = jnp.where(k > tau, 0.0, jnp.where(k == tau, tie, NEG))
        return 0

    lax.fori_loop(0, nvis, bias_body, 0)

    q_heads = []
    for h in range(DSA_HEADS):
        qpair = qb_ref[:, (h // 2) * LANES:(h // 2 + 1) * LANES]
        if h % 2 == 0:
            keep = jnp.where(lane < DSA_HEAD_DIM, 1.0, 0.0).astype(BF16)
        else:
            keep = jnp.where(lane >= DSA_HEAD_DIM, 1.0, 0.0).astype(BF16)
        terms = _bf16_terms(ALIBI_SLOPES[h] * LOG2E, POS_TERMS)
        q_pos = jnp.zeros((tq, LANES), F32)
        for j, term in enumerate(terms):
            q_pos = jnp.where(lane == j, POS_SPLIT * term, jnp.where(lane == POS_TERMS + j, term, q_pos))
        q_heads.append(jnp.concatenate([qpair * keep, q_pos.astype(BF16)], axis=1))

    def prep(c):
        ks = pl.multiple_of(c * ch, ch)
        return ks, bias_scr[c], kpos_ref[pl.ds(ks, ch), :]

    def logits(ctx, h):
        ks, sel, kpos = ctx
        psl = slice((h // 2) * LANES, (h // 2 + 1) * LANES)
        k_aug = jnp.concatenate([kb_ref[pl.ds(ks, ch), psl], kpos], axis=1)
        return _dot_nt(k_aug, q_heads[h]) + sel

    def values_t(c, h):
        return vbt_ref[c, h * DSA_HEAD_DIM:(h + 1) * DSA_HEAD_DIM, :]

    _attention(nvis, DSA_HEADS, prep, logits, values_t, o_ref, scratch)


def _dsa_call(qb, kb, vbt, qi, ki, wi, kpos, tq, ch):
    b, s, w = qb.shape
    nch = s // ch
    topk = min(TOPK_MAX, s // 4)
    full = lambda wdt: pl.BlockSpec((None, s, wdt), lambda bi, i: (bi, 0, 0))
    tile = lambda wdt: pl.BlockSpec((None, tq, wdt), lambda bi, i: (bi, i, 0))
    return pl.pallas_call(
        functools.partial(_dsa_kernel, tq=tq, ch=ch, topk=topk, seq=s),
        grid=(b, s // tq),
        in_specs=[tile(w), full(w), pl.BlockSpec((None, nch, w, ch), lambda bi, i: (bi, 0, 0, 0)),
                  tile(2 * LANES), full(LANES), tile(LANES), full(LANES)],
        out_specs=tile(w),
        out_shape=jax.ShapeDtypeStruct((b, s, w), BF16),
        scratch_shapes=[pltpu.VMEM((IDX_HEADS, tq, LANES), BF16), pltpu.VMEM((IDX_HEADS, 1, tq), F32),
                        pltpu.VMEM((nch, ch, tq), jnp.int32), pltpu.VMEM((nch, ch, tq), I16),
                        pltpu.VMEM((nch, ch, tq), I16), pltpu.VMEM((nch, ch, tq), F32),
                        pltpu.VMEM((1, tq), jnp.int32)]
        + _attn_scratch(DSA_HEADS, DSA_HEAD_DIM, tq, ch),
        compiler_params=pltpu.CompilerParams(
            dimension_semantics=("parallel", "arbitrary"), vmem_limit_bytes=VMEM_LIMIT),
        name="dsa",
    )(qb, kb, vbt, qi, ki, wi, kpos)


def _post_kernel(x_ref, oa_ref, ob_ref, gate_ref, wa_ref, wb_ref, wo_ref, g1_ref, b1_ref,
                 wfi_ref, wfd_ref, g2_ref, b2_ref, out_ref):
    ya = _dot(oa_ref[...], wa_ref[...])
    yb = _dot(ob_ref[...], wb_ref[...])
    mixed = (_sigmoid(gate_ref[:, :D_MODEL].astype(F32)) * ya
             + _sigmoid(gate_ref[:, D_MODEL:].astype(F32)) * yb)
    mix_out = _dot(mixed.astype(BF16), wo_ref[...])
    h1 = _layer_norm(ALPHA * x_ref[...] + mix_out, g1_ref[...], b1_ref[...])
    gu = _dot(h1.astype(BF16), wfi_ref[...])
    gate = gu[:, :D_FF]
    act = gate * _sigmoid(gate) * gu[:, D_FF:]
    f = _dot(act.astype(BF16), wfd_ref[...])
    out_ref[...] = _layer_norm(ALPHA * h1 + f, g2_ref[...], b2_ref[...])


def _post_call(x2, oa, ob, gates, weights, rows):
    n = x2.shape[0]
    row_spec = lambda wdt: pl.BlockSpec((rows, wdt), lambda i: (i, 0))
    return pl.pallas_call(
        _post_kernel,
        grid=(n // rows,),
        in_specs=[row_spec(D_MODEL), row_spec(oa.shape[1]), row_spec(ob.shape[1]),
                  row_spec(2 * D_MODEL)] + [_const_spec(a.shape) for a in weights],
        out_specs=row_spec(D_MODEL),
        out_shape=jax.ShapeDtypeStruct((n, D_MODEL), F32),
        compiler_params=pltpu.CompilerParams(
            dimension_semantics=("parallel",), vmem_limit_bytes=VMEM_LIMIT),
        name="post",
    )(x2, oa, ob, gates, *weights)


def _pad_cols(w, width):
    return jnp.pad(w, ((0, 0), (0, width - w.shape[1])))


def _prep_proj_weights(w_in, q_norm, kv_norm, w_uq, w_ukv):
    sizes = (MLA_Q_RANK, MLA_KV_RANK, MLA_ROPE, 512, 512, 512, IDX_HEADS * IDX_DIM, IDX_DIM,
             IDX_HEADS, D_MODEL, D_MODEL)
    parts, start = [], 0
    for n in sizes:
        parts.append(w_in[:, start:start + n])
        start += n
    w_cq, w_ckv, w_kr, w_qb, w_kb, w_vb, w_qi, w_ki, w_wi, w_ga, w_gb = parts
    half = MLA_ROPE // 2
    d = w_in.shape[0]

    def rope_group(first, second):
        return jnp.concatenate([jnp.zeros((d, MLA_NOPE), F32), first, second,
                                jnp.zeros((d, LANES - MLA_QK), F32)], axis=1)

    wc = jnp.concatenate([w_cq, w_ckv], axis=1)
    wkr = jnp.concatenate([rope_group(w_kr[:, :half], w_kr[:, half:]),
                           rope_group(w_kr[:, half:], w_kr[:, :half])], axis=1)
    wdsa = jnp.concatenate([w_qb * DSA_SCALE, w_kb], axis=1)
    widx = jnp.concatenate([w_qi, w_ki, w_ki, w_ki, w_ki, _pad_cols(w_wi, LANES)], axis=1)
    wgate = jnp.concatenate([w_ga, w_gb], axis=1)

    r = w_uq.shape[0]
    uq = w_uq.reshape(r, MLA_HEADS, MLA_QK)
    zq = lambda n: jnp.zeros((r, MLA_HEADS, n), F32)
    wuqm = jnp.concatenate([uq, zq(LANES - MLA_QK)], axis=2).reshape(r, MLA_HEADS * LANES)
    wuqs = jnp.concatenate([zq(MLA_NOPE), uq[:, :, MLA_NOPE + half:], uq[:, :, MLA_NOPE:MLA_NOPE + half],
                            zq(LANES - MLA_QK)], axis=2).reshape(r, MLA_HEADS * LANES)
    ukv = w_ukv.reshape(r, MLA_HEADS, MLA_NOPE + MLA_V)
    wuk = jnp.concatenate([ukv[:, :, :MLA_NOPE], zq(LANES - MLA_NOPE)], axis=2).reshape(r, MLA_HEADS * LANES)
    wuvt = ukv[:, :, MLA_NOPE:].reshape(r, MLA_HEADS * MLA_V).T

    bf = lambda a: a.astype(BF16)
    return (bf(wc), bf(wkr), bf(wdsa), bf(w_vb.T), bf(widx), bf(wgate), q_norm.reshape(1, -1),
            kv_norm.reshape(1, -1), bf(wuqm), bf(wuqs), bf(wuk), bf(wuvt))


def _rope_tables():
    half = MLA_ROPE // 2
    inv_freq = ROPE_BASE ** (-jnp.arange(half, dtype=F32) / half)
    z = lambda n: jnp.zeros((n,), F32)
    freq = jnp.concatenate([z(MLA_NOPE), inv_freq, inv_freq, z(LANES - MLA_QK)]).reshape(1, LANES)
    sign = jnp.concatenate([z(MLA_NOPE), -jnp.ones((half,), F32), jnp.ones((half,), F32),
                            z(LANES - MLA_QK)]).reshape(1, LANES)
    return freq, sign


def kernel(x, positions, w_in, mla_q_norm, mla_kv_norm, mla_w_uq, mla_w_ukv, w_branch_a, w_branch_b,
           w_out, ln1_g, ln1_b, ffn_w_in, ffn_w_down, ln2_g, ln2_b):
    b, s, d = x.shape
    assert w_in.shape[0] == DEPTH and d == D_MODEL
    assert s & (s - 1) == 0 and s % ATT_TQ == 0 and s % ATT_CH == 0 and PROJ_ROWS % ATT_CH == 0
    assert (b * s) % PROJ_ROWS == 0 and s <= POS_SPLIT * 256
    n = b * s
    h = x.reshape(n, d)
    pos2 = positions.reshape(n, 1)
    freq, sign = _rope_tables()
    for l in range(DEPTH):
        pw = _prep_proj_weights(w_in[l], mla_q_norm[l], mla_kv_norm[l], mla_w_uq[l], mla_w_ukv[l])
        qa, ka, vat, qb, kb, vbt, qi, ki, wi, kpos, gates = _proj_call(
            h, pos2, freq, sign, pw, PROJ_ROWS, ATT_CH)
        r3 = lambda a: a.reshape(b, s, a.shape[1])
        t4 = lambda a: a.reshape(b, s // ATT_CH, a.shape[1], ATT_CH)
        o_a = _mla_call(r3(qa), r3(ka), t4(vat), ATT_TQ, ATT_CH)
        o_b = _dsa_call(r3(qb), r3(kb), t4(vbt), r3(qi), r3(ki), r3(wi), r3(kpos), ATT_TQ, ATT_CH)
        row = lambda v: v.reshape(1, -1)
        post_w = (w_branch_a[l].astype(BF16), w_branch_b[l].astype(BF16), w_out[l].astype(BF16),
                  row(ln1_g[l]), row(ln1_b[l]), ffn_w_in[l].astype(BF16), ffn_w_down[l].astype(BF16),
                  row(ln2_g[l]), row(ln2_b[l]))
        h = _post_call(h, o_a.reshape(n, -1), o_b.reshape(n, -1), gates, post_w, POST_ROWS)
    return h.reshape(b, s, d)
```

```python
import functools
import math
import struct

import jax
import jax.numpy as jnp
from jax import lax
from jax.experimental import pallas as pl
from jax.experimental.pallas import tpu as pltpu

F32 = jnp.float32
BF16 = jnp.bfloat16

D_MODEL = 1024
MLA_HEADS = 8
MLA_Q_RANK = 256
MLA_KV_RANK = 256
MLA_NOPE = 64
MLA_ROPE = 32
MLA_V = 64
ROPE_BASE = 10000.0
DSA_HEADS = 8
DSA_HEAD_DIM = 64
IDX_HEADS = 8
IDX_DIM = 32
TOPK_MAX = 256
D_FF = 2816
DEPTH = 1
ALPHA = (2 * DEPTH) ** 0.25
LN_EPS = 1e-5
RMS_EPS = 1e-6

LANES = 128
SUBLANES = 8
MLA_QK = MLA_NOPE + MLA_ROPE
MLA_SCALE = 1.0 / math.sqrt(MLA_QK)
DSA_SCALE = 1.0 / math.sqrt(DSA_HEAD_DIM)
IDX_SCALE = 1.0 / math.sqrt(IDX_DIM * IDX_HEADS)
ALIBI_SLOPES = tuple(2.0 ** (-8.0 * (i + 1) / DSA_HEADS) for i in range(DSA_HEADS))
POS_SPLIT = 64
POS_TERMS = 3
LOG2E = math.log2(math.e)

NEG = -1e30
INT_MIN = -(2 ** 31)
I16 = jnp.int16
I16_MIN = -(2 ** 15)

PROJ_ROWS = 512
POST_ROWS = 512
ATT_TQ = 256
ATT_CH = 256
VMEM_LIMIT = 56 * 1024 * 1024


def _const_spec(shape):
    zeros = (0,) * len(shape)
    return pl.BlockSpec(shape, lambda *_: zeros, pipeline_mode=pl.Buffered(1))


def _dot(a, b):
    return jnp.dot(a, b, preferred_element_type=F32)


def _dot_nt(a, b):
    return lax.dot_general(a, b, (((1,), (1,)), ((), ())), preferred_element_type=F32)


def _rms_norm(x, g):
    return x * lax.rsqrt(jnp.mean(x * x, axis=-1, keepdims=True) + RMS_EPS) * g


def _layer_norm(x, g, b):
    mu = jnp.mean(x, axis=-1, keepdims=True)
    xc = x - mu
    var = jnp.mean(xc * xc, axis=-1, keepdims=True)
    return xc * lax.rsqrt(var + LN_EPS) * g + b


def _sigmoid(x):
    return 1.0 / (1.0 + jnp.exp(-x))


def _bf16_terms(x, n):
    terms = []
    for _ in range(n):
        bits = struct.unpack("<I", struct.pack("<f", x))[0]
        bits = (bits + 0x7FFF + ((bits >> 16) & 1)) & 0xFFFF0000
        term = struct.unpack("<f", struct.pack("<I", bits))[0]
        terms.append(term)
        x -= term
    return terms


def _slab_rows(dtype):
    return SUBLANES * (4 // jnp.dtype(dtype).itemsize)


def _fold_rows(x, op, ways=4):
    slab = _slab_rows(x.dtype)
    parts = [x[j * slab:(j + 1) * slab] for j in range(x.shape[0] // slab)]
    accs = parts[:ways]
    for j in range(ways, len(parts)):
        accs[j % ways] = op(accs[j % ways], parts[j])
    while len(accs) > 1:
        accs = [op(accs[k], accs[k + 1]) for k in range(0, len(accs) - 1, 2)] + accs[len(accs) & ~1:]
    return accs[0]


def _tall(x, n):
    return x if n == 1 else jnp.concatenate([x] * n, axis=0)


def _attention(nvis, heads, prep, logits, values_t, o_ref, scratch):
    s_scr, cmax_scr, m_scr, acc_scr = scratch
    dv = acc_scr.shape[1] - _slab_rows(BF16)
    ones = jnp.ones((_slab_rows(BF16), s_scr.shape[2]), BF16)
    m_scr[...] = jnp.full(m_scr.shape, NEG, F32)
    acc_scr[...] = jnp.zeros(acc_scr.shape, F32)

    def logits_part(c, slot):
        ctx = prep(jnp.asarray(c, jnp.int32))
        for h in range(heads):
            s_t = logits(ctx, h)
            s_scr[h, slot] = s_t
            cmax_scr[h, slot] = _fold_rows(s_t, jnp.maximum)

    def values_part(c, slot):
        for h in range(heads):
            m_prev = m_scr[h]
            m_new = cmax_scr[h, slot]
            for shift in (4, 2, 1):
                m_new = jnp.maximum(m_new, pltpu.roll(m_new, shift, axis=0))
            m_new = jnp.maximum(m_prev, m_new)
            alpha = jnp.exp2(m_prev - m_new)
            p = jnp.exp2(s_scr[h, slot] - _tall(m_new, s_scr.shape[2] // SUBLANES))
            v_aug = jnp.concatenate([values_t(c, h), ones], axis=0)
            acc_scr[h] = (_tall(alpha, acc_scr.shape[1] // SUBLANES) * acc_scr[h]
                          + _dot(v_aug, p.astype(BF16)))
            m_scr[h] = m_new

    logits_part(0, 0)

    def step(c, slot):
        values_part(c, slot)
        logits_part(c + 1, 1 - slot)

    def body(pair, _):
        step(2 * pair, 0)
        step(2 * pair + 1, 1)
        return 0

    lax.fori_loop(0, (nvis - 1) // 2, body, 0)

    @pl.when(nvis % 2 == 0)
    def _():
        step(nvis - 2, 0)
        values_part(nvis - 1, 1)

    @pl.when(nvis % 2 == 1)
    def _():
        values_part(nvis - 1, 0)

    norm = lambda h: acc_scr[h, :dv] / acc_scr[h, dv:dv + 1]
    for pair in range(heads // 2):
        o_t = jnp.concatenate([norm(2 * pair), norm(2 * pair + 1)], axis=0)
        o_ref[:, pair * LANES:(pair + 1) * LANES] = o_t.T.astype(BF16)


def _attn_scratch(heads, dv, tq, ch):
    return [pltpu.VMEM((heads, 2, ch, tq), F32), pltpu.VMEM((heads, 2, SUBLANES, tq), F32),
            pltpu.VMEM((heads, SUBLANES, tq), F32),
            pltpu.VMEM((heads, dv + _slab_rows(BF16), tq), F32)]


def _proj_kernel(x_ref, pos_ref, posr_ref, freq_ref, wc_ref, wkr_ref, wdsa_ref, wvbt_ref, widx_ref,
                 wgate_ref, qn_ref, kvn_ref, wuqm_ref, wuqs_ref, wuk_ref, wuvt_ref,
                 qa_ref, ka_ref, vat_ref, qb_ref, kb_ref, vbt_ref, qi_ref, ki_ref, wi_ref,
                 kpos_ref, gate_ref, *, ch):
    rows = x_ref.shape[0]
    xb = x_ref[...].astype(BF16)
    pos = pos_ref[...]
    ang = freq_ref[...] * posr_ref[...].astype(F32)
    cos_c, sin_c = jnp.cos(ang), jnp.sin(ang)
    cos_t = jnp.concatenate([jnp.ones((MLA_NOPE, rows), F32), cos_c, cos_c,
                             jnp.ones((LANES - MLA_QK, rows), F32)], axis=0).T
    sin_t = jnp.concatenate([jnp.zeros((MLA_NOPE, rows), F32), -sin_c, sin_c,
                             jnp.zeros((LANES - MLA_QK, rows), F32)], axis=0).T

    c = _dot(xb, wc_ref[...])
    cq = _rms_norm(c[:, :MLA_Q_RANK], qn_ref[...]).astype(BF16)
    ckv = _rms_norm(c[:, MLA_Q_RANK:], kvn_ref[...]).astype(BF16)

    gate_ref[...] = _dot(xb, wgate_ref[...]).astype(BF16)

    d = _dot(xb, wdsa_ref[...])
    w = DSA_HEADS * DSA_HEAD_DIM
    qb_ref[...] = (d[:, :w] * LOG2E).astype(BF16)
    kb_ref[...] = d[:, w:].astype(BF16)

    va_t = _dot_nt(wuvt_ref[...], ckv).astype(BF16)
    vb_t = _dot_nt(wvbt_ref[...], xb).astype(BF16)
    for k in range(rows // ch):
        vat_ref[k] = va_t[:, k * ch:(k + 1) * ch]
        vbt_ref[k] = vb_t[:, k * ch:(k + 1) * ch]

    ix = _dot(xb, widx_ref[...])
    qi_ref[...] = ix[:, :2 * LANES].astype(BF16)
    ki_ref[...] = ix[:, 2 * LANES:3 * LANES].astype(BF16)
    wi_ref[...] = ix[:, 3 * LANES:]

    lane = lax.broadcasted_iota(jnp.int32, (rows, LANES), 1)
    hi = (pos >> (POS_SPLIT.bit_length() - 1)).astype(F32)
    lo = (pos & (POS_SPLIT - 1)).astype(F32)
    kpos_ref[...] = jnp.where(lane < POS_TERMS, hi,
                              jnp.where(lane < 2 * POS_TERMS, lo, 0.0)).astype(BF16)

    qm = _dot(cq, wuqm_ref[...])
    qs = _dot(cq, wuqs_ref[...])
    cos_q = cos_t * (MLA_SCALE * LOG2E)
    sin_q = sin_t * (MLA_SCALE * LOG2E)
    for h in range(MLA_HEADS):
        sl = slice(h * LANES, (h + 1) * LANES)
        qa_ref[:, sl] = (qm[:, sl] * cos_q + qs[:, sl] * sin_q).astype(BF16)

    kr = _dot(xb, wkr_ref[...])
    kr_rot = kr[:, :LANES] * cos_t + kr[:, LANES:] * sin_t
    kn = _dot(ckv, wuk_ref[...])
    for h in range(MLA_HEADS):
        sl = slice(h * LANES, (h + 1) * LANES)
        ka_ref[:, sl] = (kn[:, sl] + kr_rot).astype(BF16)


def _proj_call(x2, pos2, posr, freq, weights, rows, ch):
    n = x2.shape[0]
    (wc, wkr, wdsa, wvbt, widx, wgate, qn, kvn, wuqm, wuqs, wuk, wuvt) = weights
    row_spec = lambda wdt: pl.BlockSpec((rows, wdt), lambda i: (i, 0))
    row_out = lambda wdt, dt: (row_spec(wdt), jax.ShapeDtypeStruct((n, wdt), dt))
    t_out = lambda wdt: (pl.BlockSpec((rows // ch, wdt, ch), lambda i: (i, 0, 0)),
                         jax.ShapeDtypeStruct((n // ch, wdt, ch), BF16))
    outs = [row_out(8 * LANES, BF16), row_out(8 * LANES, BF16), t_out(MLA_HEADS * MLA_V),
            row_out(512, BF16), row_out(512, BF16), t_out(DSA_HEADS * DSA_HEAD_DIM),
            row_out(2 * LANES, BF16), row_out(LANES, BF16), row_out(LANES, F32),
            row_out(LANES, BF16), row_out(2 * D_MODEL, BF16)]
    consts = (freq, wc, wkr, wdsa, wvbt, widx, wgate, qn, kvn, wuqm, wuqs, wuk, wuvt)
    return pl.pallas_call(
        functools.partial(_proj_kernel, ch=ch),
        grid=(n // rows,),
        in_specs=[row_spec(D_MODEL), row_spec(1), pl.BlockSpec((None, 1, rows), lambda i: (i, 0, 0))]
        + [_const_spec(a.shape) for a in consts],
        out_specs=[o[0] for o in outs],
        out_shape=[o[1] for o in outs],
        compiler_params=pltpu.CompilerParams(
            dimension_semantics=("parallel",), vmem_limit_bytes=VMEM_LIMIT),
        name="proj",
    )(x2, pos2, posr, *consts)


def _mla_kernel(q_ref, k_ref, vt_ref, o_ref, *scratch, tq, ch):
    i = pl.program_id(1)
    nvis = ((i + 1) * tq + ch - 1) // ch
    key = lax.broadcasted_iota(jnp.int32, (ch, tq), 0)
    query = i * tq + lax.broadcasted_iota(jnp.int32, (ch, tq), 1)

    def prep(c):
        ks = pl.multiple_of(c * ch, ch)
        return ks, jnp.where(key + ks <= query, 0.0, NEG)

    def logits(ctx, h):
        ks, causal = ctx
        hsl = slice(h * LANES, (h + 1) * LANES)
        return _dot_nt(k_ref[pl.ds(ks, ch), hsl], q_ref[:, hsl]) + causal

    def values_t(c, h):
        return vt_ref[c, h * MLA_V:(h + 1) * MLA_V, :]

    _attention(nvis, MLA_HEADS, prep, logits, values_t, o_ref, scratch)


def _mla_call(qa, ka, vat, tq, ch):
    b, s, _ = qa.shape
    wv = MLA_HEADS * MLA_V
    return pl.pallas_call(
        functools.partial(_mla_kernel, tq=tq, ch=ch),
        grid=(b, s // tq),
        in_specs=[
            pl.BlockSpec((None, tq, 8 * LANES), lambda bi, i: (bi, i, 0)),
            pl.BlockSpec((None, s, 8 * LANES), lambda bi, i: (bi, 0, 0)),
            pl.BlockSpec((None, s // ch, wv, ch), lambda bi, i: (bi, 0, 0, 0)),
        ],
        out_specs=pl.BlockSpec((None, tq, wv), lambda bi, i: (bi, i, 0)),
        out_shape=jax.ShapeDtypeStruct((b, s, wv), BF16),
        scratch_shapes=_attn_scratch(MLA_HEADS, MLA_V, tq, ch),
        compiler_params=pltpu.CompilerParams(
            dimension_semantics=("parallel", "arbitrary"), vmem_limit_bytes=VMEM_LIMIT),
        name="mla",
    )(qa, ka, vat)


def _dsa_kernel(qb_ref, kb_ref, vbt_ref, qi_ref, ki_ref, wi_ref, kpos_ref, o_ref,
                qih_scr, wih_scr, key_scr, hi_scr, lo_scr, bias_scr, jsel_scr, *scratch,
                tq, ch, topk, seq):
    i = pl.program_id(1)
    nvis = ((i + 1) * tq + ch - 1) // ch
    key_idx = lax.broadcasted_iota(jnp.int32, (ch, tq), 0)
    query = i * tq + lax.broadcasted_iota(jnp.int32, (ch, tq), 1)
    lane = lax.broadcasted_iota(jnp.int32, (tq, LANES), 1)

    qi = qi_ref[...]
    wi_t = wi_ref[...].T * IDX_SCALE
    for h in range(IDX_HEADS):
        group = qi[:, (h // 4) * LANES:(h // 4 + 1) * LANES]
        lo = IDX_DIM * (h % 4)
        keep = jnp.where(lane >= lo, jnp.where(lane < lo + IDX_DIM, 1.0, 0.0), 0.0).astype(BF16)
        qih_scr[h] = group * keep
        wih_scr[h] = wi_t[h:h + 1, :]

    def score_chunk(c):
        ks = pl.multiple_of(c * ch, ch)
        kic = ki_ref[pl.ds(ks, ch), :]
        sc = jnp.zeros((ch, tq), F32)
        for h in range(IDX_HEADS):
            sc = sc + wih_scr[h] * jnp.maximum(_dot_nt(kic, qih_scr[h]), 0.0)
        sc = jnp.where(sc == 0.0, 0.0, sc)
        bits = pltpu.bitcast(sc, jnp.int32)
        key = bits ^ ((bits >> 31) & 0x7FFFFFFF)
        key = jnp.where(key_idx + ks <= query, key, INT_MIN)
        key_scr[c] = key
        hi_scr[c] = (key >> 16).astype(I16)
        lo_scr[c] = (((key ^ 0x8000) << 16) >> 16).astype(I16)

    def score_pair(pair, _):
        score_chunk(2 * pair)
        score_chunk(2 * pair + 1)
        return 0

    lax.fori_loop(0, nvis // 2, score_pair, 0)

    @pl.when(nvis % 2 == 1)
    def _():
        score_chunk(nvis - 1)

    def count(pred):
        def body(c, acc):
            return acc + _fold_rows(pred(key_scr[c], c), jnp.add)
        acc = lax.fori_loop(0, nvis, body, jnp.zeros((SUBLANES, tq), F32))
        return jnp.sum(acc, axis=0, keepdims=True)

    def search16(half_scr, n_bits, target, cnt_all):
        def count16(cand):
            cand16 = cand.astype(I16)
            def body(c, acc):
                ones = jnp.where(half_scr[c] >= cand16, jnp.int16(1), jnp.int16(0))
                return acc + _fold_rows(ones, jnp.add)
            acc = lax.fori_loop(0, nvis, body, jnp.zeros((_slab_rows(I16), tq), I16))
            return jnp.sum(acc.astype(F32), axis=0, keepdims=True)

        def body(it, carry):
            lo, cnt_lo, cnt_hi = carry
            cand = lo + jnp.left_shift(jnp.int32(1), 15 - it)
            cnt = count16(cand)
            ok = cnt >= target
            return jnp.where(ok, cand, lo), jnp.where(ok, cnt, cnt_lo), jnp.where(ok, cnt_hi, cnt)

        return lax.fori_loop(0, n_bits, body,
                             (jnp.full((1, tq), I16_MIN, jnp.int32), cnt_all, jnp.zeros((1, tq), F32)))

    n_bits = jnp.where((i + 1) * tq <= topk, 0, 16)
    all_keys = jnp.zeros((1, tq), F32) + (nvis * ch).astype(F32)
    tau_hi, cnt_ge_hi, cnt_gt_hi = search16(hi_scr, n_bits, float(topk), all_keys)
    tau_hi16 = tau_hi.astype(I16)

    def bucket_body(c, _):
        lo_scr[c] = jnp.where(hi_scr[c] == tau_hi16, lo_scr[c], jnp.int16(I16_MIN))
        return 0

    lax.fori_loop(0, nvis, bucket_body, 0)
    tau_lo, cnt_ge_lo, cnt_gt_lo = search16(lo_scr, n_bits, topk - cnt_gt_hi, cnt_ge_hi - cnt_gt_hi)
    tau = (tau_hi << 16) + (tau_lo - I16_MIN)
    cnt_ge = cnt_gt_hi + cnt_ge_lo
    cnt_gt = cnt_gt_hi + cnt_gt_lo
    has_tau = tau > INT_MIN
    need = topk - cnt_gt
    jsel_scr[...] = jnp.where(has_tau, seq, -1)
    excess = jnp.where(has_tau, jnp.where(cnt_ge - cnt_gt > need, 1.0, 0.0), 0.0)

    @pl.when(jnp.max(excess) > 0.0)
    def _():
        def idx_body(it, lo):
            cand = lo + jnp.left_shift(jnp.int32(1), seq.bit_length() - 2 - it)
            cnt = count(lambda k, c: jnp.where(
                k == tau, jnp.where(key_idx + c * ch < cand, 1.0, 0.0), 0.0))
            return jnp.where(cnt < need, cand, lo)
        last = lax.fori_loop(0, seq.bit_length() - 1, idx_body, jnp.zeros((1, tq), jnp.int32))
        jsel_scr[...] = jnp.where(has_tau, last, -1)

    jsel = jsel_scr[...]

    def bias_body(c, _):
        k = key_scr[c]
        tie = jnp.where(key_idx + c * ch <= jsel, 0.0, NEG)
        bias_scr[c] = jnp.where(k > tau, 0.0, jnp.where(k == tau, tie, NEG))
        return 0

    lax.fori_loop(0, nvis, bias_body, 0)

    q_heads = []
    for h in range(DSA_HEADS):
        qpair = qb_ref[:, (h // 2) * LANES:(h // 2 + 1) * LANES]
        if h % 2 == 0:
            keep = jnp.where(lane < DSA_HEAD_DIM, 1.0, 0.0).astype(BF16)
        else:
            keep = jnp.where(lane >= DSA_HEAD_DIM, 1.0, 0.0).astype(BF16)
        terms = _bf16_terms(ALIBI_SLOPES[h] * LOG2E, POS_TERMS)
        q_pos = jnp.zeros((tq, LANES), F32)
        for j, term in enumerate(terms):
            q_pos = jnp.where(lane == j, POS_SPLIT * term, jnp.where(lane == POS_TERMS + j, term, q_pos))
        q_heads.append(jnp.concatenate([qpair * keep, q_pos.astype(BF16)], axis=1))

    def prep(c):
        ks = pl.multiple_of(c * ch, ch)
        return ks, bias_scr[c], kpos_ref[pl.ds(ks, ch), :]

    def logits(ctx, h):
        ks, sel, kpos = ctx
        psl = slice((h // 2) * LANES, (h // 2 + 1) * LANES)
        k_aug = jnp.concatenate([kb_ref[pl.ds(ks, ch), psl], kpos], axis=1)
        return _dot_nt(k_aug, q_heads[h]) + sel

    def values_t(c, h):
        return vbt_ref[c, h * DSA_HEAD_DIM:(h + 1) * DSA_HEAD_DIM, :]

    _attention(nvis, DSA_HEADS, prep, logits, values_t, o_ref, scratch)


def _dsa_call(qb, kb, vbt, qi, ki, wi, kpos, tq, ch):
    b, s, w = qb.shape
    nch = s // ch
    topk = min(TOPK_MAX, s // 4)
    full = lambda wdt: pl.BlockSpec((None, s, wdt), lambda bi, i: (bi, 0, 0))
    tile = lambda wdt: pl.BlockSpec((None, tq, wdt), lambda bi, i: (bi, i, 0))
    return pl.pallas_call(
        functools.partial(_dsa_kernel, tq=tq, ch=ch, topk=topk, seq=s),
        grid=(b, s // tq),
        in_specs=[tile(w), full(w), pl.BlockSpec((None, nch, w, ch), lambda bi, i: (bi, 0, 0, 0)),
                  tile(2 * LANES), full(LANES), tile(LANES), full(LANES)],
        out_specs=tile(w),
        out_shape=jax.ShapeDtypeStruct((b, s, w), BF16),
        scratch_shapes=[pltpu.VMEM((IDX_HEADS, tq, LANES), BF16), pltpu.VMEM((IDX_HEADS, 1, tq), F32),
                        pltpu.VMEM((nch, ch, tq), jnp.int32), pltpu.VMEM((nch, ch, tq), I16),
                        pltpu.VMEM((nch, ch, tq), I16), pltpu.VMEM((nch, ch, tq), F32),
                        pltpu.VMEM((1, tq), jnp.int32)]
        + _attn_scratch(DSA_HEADS, DSA_HEAD_DIM, tq, ch),
        compiler_params=pltpu.CompilerParams(
            dimension_semantics=("parallel", "arbitrary"), vmem_limit_bytes=VMEM_LIMIT),
        name="dsa",
    )(qb, kb, vbt, qi, ki, wi, kpos)


def _post_kernel(x_ref, oa_ref, ob_ref, gate_ref, wa_ref, wb_ref, wo_ref, g1_ref, b1_ref,
                 wfi_ref, wfd_ref, g2_ref, b2_ref, out_ref):
    ya = _dot(oa_ref[...], wa_ref[...])
    yb = _dot(ob_ref[...], wb_ref[...])
    mixed = (_sigmoid(gate_ref[:, :D_MODEL].astype(F32)) * ya
             + _sigmoid(gate_ref[:, D_MODEL:].astype(F32)) * yb)
    mix_out = _dot(mixed.astype(BF16), wo_ref[...])
    h1 = _layer_norm(ALPHA * x_ref[...] + mix_out, g1_ref[...], b1_ref[...])
    gu = _dot(h1.astype(BF16), wfi_ref[...])
    gate = gu[:, :D_FF]
    act = gate * _sigmoid(gate) * gu[:, D_FF:]
    f = _dot(act.astype(BF16), wfd_ref[...])
    out_ref[...] = _layer_norm(ALPHA * h1 + f, g2_ref[...], b2_ref[...])


def _post_call(x2, oa, ob, gates, weights, rows):
    n = x2.shape[0]
    row_spec = lambda wdt: pl.BlockSpec((rows, wdt), lambda i: (i, 0))
    return pl.pallas_call(
        _post_kernel,
        grid=(n // rows,),
        in_specs=[row_spec(D_MODEL), row_spec(oa.shape[1]), row_spec(ob.shape[1]),
                  row_spec(2 * D_MODEL)] + [_const_spec(a.shape) for a in weights],
        out_specs=row_spec(D_MODEL),
        out_shape=jax.ShapeDtypeStruct((n, D_MODEL), F32),
        compiler_params=pltpu.CompilerParams(
            dimension_semantics=("parallel",), vmem_limit_bytes=VMEM_LIMIT),
        name="post",
    )(x2, oa, ob, gates, *weights)


def _pad_cols(w, width):
    return jnp.pad(w, ((0, 0), (0, width - w.shape[1])))


def _prep_proj_weights(w_in, q_norm, kv_norm, w_uq, w_ukv):
    sizes = (MLA_Q_RANK, MLA_KV_RANK, MLA_ROPE, 512, 512, 512, IDX_HEADS * IDX_DIM, IDX_DIM,
             IDX_HEADS, D_MODEL, D_MODEL)
    parts, start = [], 0
    for n in sizes:
        parts.append(w_in[:, start:start + n])
        start += n
    w_cq, w_ckv, w_kr, w_qb, w_kb, w_vb, w_qi, w_ki, w_wi, w_ga, w_gb = parts
    half = MLA_ROPE // 2
    d = w_in.shape[0]

    def rope_group(first, second):
        return jnp.concatenate([jnp.zeros((d, MLA_NOPE), F32), first, second,
                                jnp.zeros((d, LANES - MLA_QK), F32)], axis=1)

    wc = jnp.concatenate([w_cq, w_ckv], axis=1)
    wkr = jnp.concatenate([rope_group(w_kr[:, :half], w_kr[:, half:]),
                           rope_group(w_kr[:, half:], w_kr[:, :half])], axis=1)
    wdsa = jnp.concatenate([w_qb * DSA_SCALE, w_kb], axis=1)
    widx = jnp.concatenate([w_qi, w_ki, w_ki, w_ki, w_ki, _pad_cols(w_wi, LANES)], axis=1)
    wgate = jnp.concatenate([w_ga, w_gb], axis=1)

    r = w_uq.shape[0]
    uq = w_uq.reshape(r, MLA_HEADS, MLA_QK)
    zq = lambda n: jnp.zeros((r, MLA_HEADS, n), F32)
    wuqm = jnp.concatenate([uq, zq(LANES - MLA_QK)], axis=2).reshape(r, MLA_HEADS * LANES)
    wuqs = jnp.concatenate([zq(MLA_NOPE), uq[:, :, MLA_NOPE + half:], uq[:, :, MLA_NOPE:MLA_NOPE + half],
                            zq(LANES - MLA_QK)], axis=2).reshape(r, MLA_HEADS * LANES)
    ukv = w_ukv.reshape(r, MLA_HEADS, MLA_NOPE + MLA_V)
    wuk = jnp.concatenate([ukv[:, :, :MLA_NOPE], zq(LANES - MLA_NOPE)], axis=2).reshape(r, MLA_HEADS * LANES)
    wuvt = ukv[:, :, MLA_NOPE:].reshape(r, MLA_HEADS * MLA_V).T

    bf = lambda a: a.astype(BF16)
    return (bf(wc), bf(wkr), bf(wdsa), bf(w_vb.T), bf(widx), bf(wgate), q_norm.reshape(1, -1),
            kv_norm.reshape(1, -1), bf(wuqm), bf(wuqs), bf(wuk), bf(wuvt))


def _rope_freq():
    half = MLA_ROPE // 2
    return (ROPE_BASE ** (-jnp.arange(half, dtype=F32) / half)).reshape(half, 1)


def kernel(x, positions, w_in, mla_q_norm, mla_kv_norm, mla_w_uq, mla_w_ukv, w_branch_a, w_branch_b,
           w_out, ln1_g, ln1_b, ffn_w_in, ffn_w_down, ln2_g, ln2_b):
    b, s, d = x.shape
    assert w_in.shape[0] == DEPTH and d == D_MODEL
    assert s & (s - 1) == 0 and s % ATT_TQ == 0 and s % ATT_CH == 0 and PROJ_ROWS % ATT_CH == 0
    assert (b * s) % PROJ_ROWS == 0 and s <= POS_SPLIT * 256
    n = b * s
    h = x.reshape(n, d)
    pos2 = positions.reshape(n, 1)
    posr = positions.reshape(n // PROJ_ROWS, 1, PROJ_ROWS)
    freq = _rope_freq()
    for l in range(DEPTH):
        pw = _prep_proj_weights(w_in[l], mla_q_norm[l], mla_kv_norm[l], mla_w_uq[l], mla_w_ukv[l])
        qa, ka, vat, qb, kb, vbt, qi, ki, wi, kpos, gates = _proj_call(
            h, pos2, posr, freq, pw, PROJ_ROWS, ATT_CH)
        r3 = lambda a: a.reshape(b, s, a.shape[1])
        t4 = lambda a: a.reshape(b, s // ATT_CH, a.shape[1], ATT_CH)
        o_a = _mla_call(r3(qa), r3(ka), t4(vat), ATT_TQ, ATT_CH)
        o_b = _dsa_call(r3(qb), r3(kb), t4(vbt), r3(qi), r3(ki), r3(wi), r3(kpos), ATT_TQ, ATT_CH)
        row = lambda v: v.reshape(1, -1)
        post_w = (w_branch_a[l].astype(BF16), w_branch_b[l].astype(BF16), w_out[l].astype(BF16),
                  row(ln1_g[l]), row(ln1_b[l]), ffn_w_in[l].astype(BF16), ffn_w_down[l].astype(BF16),
                  row(ln2_g[l]), row(ln2_b[l]))
        h = _post_call(h, o_a.reshape(n, -1), o_b.reshape(n, -1), gates, post_w, POST_ROWS)
    return h.reshape(b, s, d)
```

```python
import functools
import math
import struct

import jax
import jax.numpy as jnp
from jax import lax
from jax.experimental import pallas as pl
from jax.experimental.pallas import tpu as pltpu

F32 = jnp.float32
BF16 = jnp.bfloat16

D_MODEL = 1024
MLA_HEADS = 8
MLA_Q_RANK = 256
MLA_KV_RANK = 256
MLA_NOPE = 64
MLA_ROPE = 32
MLA_V = 64
ROPE_BASE = 10000.0
DSA_HEADS = 8
DSA_HEAD_DIM = 64
IDX_HEADS = 8
IDX_DIM = 32
TOPK_MAX = 256
D_FF = 2816
DEPTH = 1
ALPHA = (2 * DEPTH) ** 0.25
LN_EPS = 1e-5
RMS_EPS = 1e-6

LANES = 128
SUBLANES = 8
MLA_QK = MLA_NOPE + MLA_ROPE
MLA_SCALE = 1.0 / math.sqrt(MLA_QK)
DSA_SCALE = 1.0 / math.sqrt(DSA_HEAD_DIM)
IDX_SCALE = 1.0 / math.sqrt(IDX_DIM * IDX_HEADS)
ALIBI_SLOPES = tuple(2.0 ** (-8.0 * (i + 1) / DSA_HEADS) for i in range(DSA_HEADS))
POS_SPLIT = 64
POS_TERMS = 3
LOG2E = math.log2(math.e)

NEG = -1e30
INT_MIN = -(2 ** 31)
I16 = jnp.int16
I16_MIN = -(2 ** 15)

PROJ_ROWS = 512
POST_ROWS = 512
ATT_TQ = 256
ATT_CH = 256
VMEM_LIMIT = 56 * 1024 * 1024


def _const_spec(shape):
    zeros = (0,) * len(shape)
    return pl.BlockSpec(shape, lambda *_: zeros, pipeline_mode=pl.Buffered(1))


def _dot(a, b):
    return jnp.dot(a, b, preferred_element_type=F32)


def _dot_nt(a, b):
    return lax.dot_general(a, b, (((1,), (1,)), ((), ())), preferred_element_type=F32)


def _rms_norm(x, g):
    return x * lax.rsqrt(jnp.mean(x * x, axis=-1, keepdims=True) + RMS_EPS) * g


def _layer_norm(x, g, b):
    mu = jnp.mean(x, axis=-1, keepdims=True)
    xc = x - mu
    var = jnp.mean(xc * xc, axis=-1, keepdims=True)
    return xc * lax.rsqrt(var + LN_EPS) * g + b


def _sigmoid(x):
    return 1.0 / (1.0 + jnp.exp(-x))


def _bf16_terms(x, n):
    terms = []
    for _ in range(n):
        bits = struct.unpack("<I", struct.pack("<f", x))[0]
        bits = (bits + 0x7FFF + ((bits >> 16) & 1)) & 0xFFFF0000
        term = struct.unpack("<f", struct.pack("<I", bits))[0]
        terms.append(term)
        x -= term
    return terms


def _slab_rows(dtype):
    return SUBLANES * (4 // jnp.dtype(dtype).itemsize)


def _fold_rows(x, op, ways=4):
    slab = _slab_rows(x.dtype)
    parts = [x[j * slab:(j + 1) * slab] for j in range(x.shape[0] // slab)]
    accs = parts[:ways]
    for j in range(ways, len(parts)):
        accs[j % ways] = op(accs[j % ways], parts[j])
    while len(accs) > 1:
        accs = [op(accs[k], accs[k + 1]) for k in range(0, len(accs) - 1, 2)] + accs[len(accs) & ~1:]
    return accs[0]


def _tall(x, n):
    return x if n == 1 else jnp.concatenate([x] * n, axis=0)


def _attention(nvis, heads, prep, logits, values_t, o_ref, scratch):
    s_scr, cmax_scr, m_scr, acc_scr = scratch
    dv = acc_scr.shape[1] - _slab_rows(BF16)
    ones = jnp.ones((_slab_rows(BF16), s_scr.shape[2]), BF16)
    m_scr[...] = jnp.full(m_scr.shape, NEG, F32)
    acc_scr[...] = jnp.zeros(acc_scr.shape, F32)

    def logits_part(c, slot):
        ctx = prep(jnp.asarray(c, jnp.int32))
        for h in range(heads):
            s_t = logits(ctx, h)
            s_scr[h, slot] = s_t
            cmax_scr[h, slot] = _fold_rows(s_t, jnp.maximum)

    def values_part(c, slot):
        for h in range(heads):
            m_prev = m_scr[h]
            m_new = cmax_scr[h, slot]
            for shift in (4, 2, 1):
                m_new = jnp.maximum(m_new, pltpu.roll(m_new, shift, axis=0))
            m_new = jnp.maximum(m_prev, m_new)
            alpha = jnp.exp2(m_prev - m_new)
            p = jnp.exp2(s_scr[h, slot] - _tall(m_new, s_scr.shape[2] // SUBLANES))
            v_aug = jnp.concatenate([values_t(c, h), ones], axis=0)
            acc_scr[h] = (_tall(alpha, acc_scr.shape[1] // SUBLANES) * acc_scr[h]
                          + _dot(v_aug, p.astype(BF16)))
            m_scr[h] = m_new

    logits_part(0, 0)

    def step(c, slot):
        values_part(c, slot)
        logits_part(c + 1, 1 - slot)

    def body(pair, _):
        step(2 * pair, 0)
        step(2 * pair + 1, 1)
        return 0

    lax.fori_loop(0, (nvis - 1) // 2, body, 0)

    @pl.when(nvis % 2 == 0)
    def _():
        step(nvis - 2, 0)
        values_part(nvis - 1, 1)

    @pl.when(nvis % 2 == 1)
    def _():
        values_part(nvis - 1, 0)

    norm = lambda h: acc_scr[h, :dv] / acc_scr[h, dv:dv + 1]
    for pair in range(heads // 2):
        o_t = jnp.concatenate([norm(2 * pair), norm(2 * pair + 1)], axis=0)
        o_ref[:, pair * LANES:(pair + 1) * LANES] = o_t.T.astype(BF16)


def _attn_scratch(heads, dv, tq, ch):
    return [pltpu.VMEM((heads, 2, ch, tq), F32), pltpu.VMEM((heads, 2, SUBLANES, tq), F32),
            pltpu.VMEM((heads, SUBLANES, tq), F32),
            pltpu.VMEM((heads, dv + _slab_rows(BF16), tq), F32)]


def _proj_kernel(x_ref, pos_ref, posr_ref, freq_ref, wc_ref, wkr_ref, wdsa_ref, wvbt_ref, widx_ref,
                 wgate_ref, qn_ref, kvn_ref, wuqm_ref, wuqs_ref, wuk_ref, wuvt_ref,
                 qa_ref, ka_ref, vat_ref, qb_ref, kb_ref, vbt_ref, qi_ref, ki_ref, wi_ref,
                 kpos_ref, gate_ref, *, ch):
    rows = x_ref.shape[0]
    xb = x_ref[...].astype(BF16)
    pos = pos_ref[...]
    ang = freq_ref[...] * posr_ref[...].astype(F32)
    cos_c, sin_c = jnp.cos(ang), jnp.sin(ang)
    cos_t = jnp.concatenate([jnp.ones((MLA_NOPE, rows), F32), cos_c, cos_c,
                             jnp.ones((LANES - MLA_QK, rows), F32)], axis=0).T
    sin_t = jnp.concatenate([jnp.zeros((MLA_NOPE, rows), F32), -sin_c, sin_c,
                             jnp.zeros((LANES - MLA_QK, rows), F32)], axis=0).T

    c = _dot(xb, wc_ref[...])
    cq = _rms_norm(c[:, :MLA_Q_RANK], qn_ref[...]).astype(BF16)
    ckv = _rms_norm(c[:, MLA_Q_RANK:], kvn_ref[...]).astype(BF16)

    gate_ref[...] = _dot(xb, wgate_ref[...]).astype(BF16)

    d = _dot(xb, wdsa_ref[...])
    w = DSA_HEADS * DSA_HEAD_DIM
    qb_ref[...] = (d[:, :w] * LOG2E).astype(BF16)
    kb_ref[...] = d[:, w:].astype(BF16)

    va_t = _dot_nt(wuvt_ref[...], ckv).astype(BF16)
    vb_t = _dot_nt(wvbt_ref[...], xb).astype(BF16)
    for k in range(rows // ch):
        vat_ref[k] = va_t[:, k * ch:(k + 1) * ch]
        vbt_ref[k] = vb_t[:, k * ch:(k + 1) * ch]

    ix = _dot(xb, widx_ref[...])
    qi_ref[...] = ix[:, :2 * LANES].astype(BF16)
    ki_ref[...] = ix[:, 2 * LANES:3 * LANES].astype(BF16)
    wi_ref[...] = ix[:, 3 * LANES:]

    lane = lax.broadcasted_iota(jnp.int32, (rows, LANES), 1)
    hi = (pos >> (POS_SPLIT.bit_length() - 1)).astype(F32)
    lo = (pos & (POS_SPLIT - 1)).astype(F32)
    kpos_ref[...] = jnp.where(lane < POS_TERMS, hi,
                              jnp.where(lane < 2 * POS_TERMS, lo, 0.0)).astype(BF16)

    qm = _dot(cq, wuqm_ref[...])
    qs = _dot(cq, wuqs_ref[...])
    cos_q = cos_t * (MLA_SCALE * LOG2E)
    sin_q = sin_t * (MLA_SCALE * LOG2E)
    for h in range(MLA_HEADS):
        sl = slice(h * LANES, (h + 1) * LANES)
        qa_ref[:, sl] = (qm[:, sl] * cos_q + qs[:, sl] * sin_q).astype(BF16)

    kr = _dot(xb, wkr_ref[...])
    kr_rot = kr[:, :LANES] * cos_t + kr[:, LANES:] * sin_t
    kn = _dot(ckv, wuk_ref[...])
    for h in range(MLA_HEADS):
        sl = slice(h * LANES, (h + 1) * LANES)
        ka_ref[:, sl] = (kn[:, sl] + kr_rot).astype(BF16)


def _proj_call(x2, pos2, posr, freq, weights, rows, ch):
    n = x2.shape[0]
    (wc, wkr, wdsa, wvbt, widx, wgate, qn, kvn, wuqm, wuqs, wuk, wuvt) = weights
    row_spec = lambda wdt: pl.BlockSpec((rows, wdt), lambda i: (i, 0))
    row_out = lambda wdt, dt: (row_spec(wdt), jax.ShapeDtypeStruct((n, wdt), dt))
    t_out = lambda wdt: (pl.BlockSpec((rows // ch, wdt, ch), lambda i: (i, 0, 0)),
                         jax.ShapeDtypeStruct((n // ch, wdt, ch), BF16))
    outs = [row_out(8 * LANES, BF16), row_out(8 * LANES, BF16), t_out(MLA_HEADS * MLA_V),
            row_out(512, BF16), row_out(512, BF16), t_out(DSA_HEADS * DSA_HEAD_DIM),
            row_out(2 * LANES, BF16), row_out(LANES, BF16), row_out(LANES, F32),
            row_out(LANES, BF16), row_out(2 * D_MODEL, BF16)]
    consts = (freq, wc, wkr, wdsa, wvbt, widx, wgate, qn, kvn, wuqm, wuqs, wuk, wuvt)
    return pl.pallas_call(
        functools.partial(_proj_kernel, ch=ch),
        grid=(n // rows,),
        in_specs=[row_spec(D_MODEL), row_spec(1), pl.BlockSpec((None, 1, rows), lambda i: (i, 0, 0))]
        + [_const_spec(a.shape) for a in consts],
        out_specs=[o[0] for o in outs],
        out_shape=[o[1] for o in outs],
        compiler_params=pltpu.CompilerParams(
            dimension_semantics=("parallel",), vmem_limit_bytes=VMEM_LIMIT),
        name="proj",
    )(x2, pos2, posr, *consts)


def _mla_kernel(q_ref, k_ref, vt_ref, o_ref, *scratch, tq, ch):
    i = pl.program_id(1)
    nvis = ((i + 1) * tq + ch - 1) // ch
    key = lax.broadcasted_iota(jnp.int32, (ch, tq), 0)
    query = i * tq + lax.broadcasted_iota(jnp.int32, (ch, tq), 1)

    def prep(c):
        ks = pl.multiple_of(c * ch, ch)
        return ks, jnp.where(key + ks <= query, 0.0, NEG)

    def logits(ctx, h):
        ks, causal = ctx
        hsl = slice(h * LANES, (h + 1) * LANES)
        return _dot_nt(k_ref[pl.ds(ks, ch), hsl], q_ref[:, hsl]) + causal

    def values_t(c, h):
        return vt_ref[c, h * MLA_V:(h + 1) * MLA_V, :]

    _attention(nvis, MLA_HEADS, prep, logits, values_t, o_ref, scratch)


def _mla_call(qa, ka, vat, tq, ch):
    b, s, _ = qa.shape
    wv = MLA_HEADS * MLA_V
    return pl.pallas_call(
        functools.partial(_mla_kernel, tq=tq, ch=ch),
        grid=(b, s // tq),
        in_specs=[
            pl.BlockSpec((None, tq, 8 * LANES), lambda bi, i: (bi, i, 0)),
            pl.BlockSpec((None, s, 8 * LANES), lambda bi, i: (bi, 0, 0)),
            pl.BlockSpec((None, s // ch, wv, ch), lambda bi, i: (bi, 0, 0, 0)),
        ],
        out_specs=pl.BlockSpec((None, tq, wv), lambda bi, i: (bi, i, 0)),
        out_shape=jax.ShapeDtypeStruct((b, s, wv), BF16),
        scratch_shapes=_attn_scratch(MLA_HEADS, MLA_V, tq, ch),
        compiler_params=pltpu.CompilerParams(
            dimension_semantics=("parallel", "arbitrary"), vmem_limit_bytes=VMEM_LIMIT),
        name="mla",
    )(qa, ka, vat)


def _dsa_kernel(qb_ref, kb_ref, vbt_ref, qi_ref, ki_ref, wi_ref, kpos_ref, o_ref,
                qih_scr, wih_scr, key_scr, hi_scr, lo_scr, bias_scr, jsel_scr, tau_scr, cnt_scr,
                *scratch,
                tq, ch, topk, seq):
    i = pl.program_id(1)
    nvis = ((i + 1) * tq + ch - 1) // ch
    key_idx = lax.broadcasted_iota(jnp.int32, (ch, tq), 0)
    query = i * tq + lax.broadcasted_iota(jnp.int32, (ch, tq), 1)
    lane = lax.broadcasted_iota(jnp.int32, (tq, LANES), 1)

    qi = qi_ref[...]
    wi_t = wi_ref[...].T * IDX_SCALE
    for h in range(IDX_HEADS):
        group = qi[:, (h // 4) * LANES:(h // 4 + 1) * LANES]
        lo = IDX_DIM * (h % 4)
        keep = jnp.where(lane >= lo, jnp.where(lane < lo + IDX_DIM, 1.0, 0.0), 0.0).astype(BF16)
        qih_scr[h] = group * keep
        wih_scr[h] = wi_t[h:h + 1, :]

    def score_chunk(c):
        ks = pl.multiple_of(c * ch, ch)
        kic = ki_ref[pl.ds(ks, ch), :]
        sc = jnp.zeros((ch, tq), F32)
        for h in range(IDX_HEADS):
            sc = sc + wih_scr[h] * jnp.maximum(_dot_nt(kic, qih_scr[h]), 0.0)
        sc = jnp.where(sc == 0.0, 0.0, sc)
        bits = pltpu.bitcast(sc, jnp.int32)
        key = bits ^ ((bits >> 31) & 0x7FFFFFFF)
        key = jnp.where(key_idx + ks <= query, key, INT_MIN)
        key_scr[c] = key
        hi_scr[c] = (key >> 16).astype(I16)
        lo_scr[c] = (((key ^ 0x8000) << 16) >> 16).astype(I16)

    def score_pair(pair, _):
        score_chunk(2 * pair)
        score_chunk(2 * pair + 1)
        return 0

    lax.fori_loop(0, nvis // 2, score_pair, 0)

    @pl.when(nvis % 2 == 1)
    def _():
        score_chunk(nvis - 1)

    def count(pred):
        def body(c, acc):
            return acc + _fold_rows(pred(key_scr[c], c), jnp.add)
        acc = lax.fori_loop(0, nvis, body, jnp.zeros((SUBLANES, tq), F32))
        return jnp.sum(acc, axis=0, keepdims=True)

    def search16(half_scr, nscan, target, cnt_all):
        slab = _slab_rows(I16)

        def count16(cand):
            cand16 = _tall(_tall(cand, slab // SUBLANES).astype(I16), ch // slab)
            acc = jnp.zeros((slab, tq), I16)
            for c in range(nscan):
                ones = jnp.where(half_scr[c] >= cand16, jnp.ones((), I16), jnp.zeros((), I16))
                acc = acc + _fold_rows(ones, jnp.add)
            cnt = _fold_rows(acc.astype(F32), jnp.add)
            for shift in (4, 2, 1):
                cnt = cnt + pltpu.roll(cnt, shift, axis=0)
            return cnt

        def body(_, carry):
            lo, step, cnt_lo, cnt_hi = carry
            cand = lo + step
            cnt = count16(cand)
            ok = cnt >= target
            return (jnp.where(ok, cand, lo), step >> 1,
                    jnp.where(ok, cnt, cnt_lo), jnp.where(ok, cnt_hi, cnt))

        rep = lambda x: jnp.broadcast_to(x, (SUBLANES, tq))
        out = lax.fori_loop(0, 16, body, (jnp.full((SUBLANES, tq), I16_MIN, jnp.int32),
                                          jnp.full((SUBLANES, tq), -I16_MIN, jnp.int32),
                                          rep(cnt_all), jnp.zeros((SUBLANES, tq), F32)))
        return out[0][:1], out[2][:1], out[3][:1]

    tau_scr[...] = jnp.full((1, tq), INT_MIN, jnp.int32)
    cnt_scr[...] = jnp.zeros(cnt_scr.shape, F32)

    @pl.when(nvis % 2 == 1)
    def _():
        hi_scr[nvis] = jnp.full((ch, tq), I16_MIN, I16)
        lo_scr[nvis] = jnp.full((ch, tq), I16_MIN, I16)

    for pairs in range(1, key_scr.shape[0] // 2 + 1):
        @pl.when(jnp.logical_and((i + 1) * tq > topk, (nvis + 1) // 2 == pairs))
        def _(nscan=2 * pairs):
            all_keys = jnp.full((1, tq), float(nscan * ch), F32)
            tau_hi, cnt_ge_hi, cnt_gt_hi = search16(hi_scr, nscan, float(topk), all_keys)
            tau_hi16 = tau_hi.astype(I16)
            for c in range(nscan):
                lo_scr[c] = jnp.where(hi_scr[c] == tau_hi16, lo_scr[c], jnp.full((), I16_MIN, I16))
            tau_lo, cnt_ge_lo, cnt_gt_lo = search16(lo_scr, nscan, topk - cnt_gt_hi,
                                                    cnt_ge_hi - cnt_gt_hi)
            tau_scr[...] = (tau_hi << 16) + (tau_lo - I16_MIN)
            cnt_scr[0] = cnt_gt_hi + cnt_ge_lo
            cnt_scr[1] = cnt_gt_hi + cnt_gt_lo

    tau = tau_scr[...]
    cnt_ge = cnt_scr[0]
    cnt_gt = cnt_scr[1]
    has_tau = tau > INT_MIN
    need = topk - cnt_gt
    jsel_scr[...] = jnp.where(has_tau, seq, -1)
    excess = jnp.where(has_tau, jnp.where(cnt_ge - cnt_gt > need, 1.0, 0.0), 0.0)

    @pl.when(jnp.max(excess) > 0.0)
    def _():
        def idx_body(it, lo):
            cand = lo + jnp.left_shift(jnp.int32(1), seq.bit_length() - 2 - it)
            cnt = count(lambda k, c: jnp.where(
                k == tau, jnp.where(key_idx + c * ch < cand, 1.0, 0.0), 0.0))
            return jnp.where(cnt < need, cand, lo)
        last = lax.fori_loop(0, seq.bit_length() - 1, idx_body, jnp.zeros((1, tq), jnp.int32))
        jsel_scr[...] = jnp.where(has_tau, last, -1)

    jsel = jsel_scr[...]

    def bias_body(c, _):
        k = key_scr[c]
        tie = jnp.where(key_idx + c * ch <= jsel, 0.0, NEG)
        bias_scr[c] = jnp.where(k > tau, 0.0, jnp.where(k == tau, tie, NEG))
        return 0

    lax.fori_loop(0, nvis, bias_body, 0)

    q_heads = []
    for h in range(DSA_HEADS):
        qpair = qb_ref[:, (h // 2) * LANES:(h // 2 + 1) * LANES]
        if h % 2 == 0:
            keep = jnp.where(lane < DSA_HEAD_DIM, 1.0, 0.0).astype(BF16)
        else:
            keep = jnp.where(lane >= DSA_HEAD_DIM, 1.0, 0.0).astype(BF16)
        terms = _bf16_terms(ALIBI_SLOPES[h] * LOG2E, POS_TERMS)
        q_pos = jnp.zeros((tq, LANES), F32)
        for j, term in enumerate(terms):
            q_pos = jnp.where(lane == j, POS_SPLIT * term, jnp.where(lane == POS_TERMS + j, term, q_pos))
        q_heads.append(jnp.concatenate([qpair * keep, q_pos.astype(BF16)], axis=1))

    def prep(c):
        ks = pl.multiple_of(c * ch, ch)
        return ks, bias_scr[c], kpos_ref[pl.ds(ks, ch), :]

    def logits(ctx, h):
        ks, sel, kpos = ctx
        psl = slice((h // 2) * LANES, (h // 2 + 1) * LANES)
        k_aug = jnp.concatenate([kb_ref[pl.ds(ks, ch), psl], kpos], axis=1)
        return _dot_nt(k_aug, q_heads[h]) + sel

    def values_t(c, h):
        return vbt_ref[c, h * DSA_HEAD_DIM:(h + 1) * DSA_HEAD_DIM, :]

    _attention(nvis, DSA_HEADS, prep, logits, values_t, o_ref, scratch)


def _dsa_call(qb, kb, vbt, qi, ki, wi, kpos, tq, ch):
    b, s, w = qb.shape
    nch = s // ch
    topk = min(TOPK_MAX, s // 4)
    full = lambda wdt: pl.BlockSpec((None, s, wdt), lambda bi, i: (bi, 0, 0))
    tile = lambda wdt: pl.BlockSpec((None, tq, wdt), lambda bi, i: (bi, i, 0))
    return pl.pallas_call(
        functools.partial(_dsa_kernel, tq=tq, ch=ch, topk=topk, seq=s),
        grid=(b, s // tq),
        in_specs=[tile(w), full(w), pl.BlockSpec((None, nch, w, ch), lambda bi, i: (bi, 0, 0, 0)),
                  tile(2 * LANES), full(LANES), tile(LANES), full(LANES)],
        out_specs=tile(w),
        out_shape=jax.ShapeDtypeStruct((b, s, w), BF16),
        scratch_shapes=[pltpu.VMEM((IDX_HEADS, tq, LANES), BF16), pltpu.VMEM((IDX_HEADS, 1, tq), F32),
                        pltpu.VMEM((nch, ch, tq), jnp.int32), pltpu.VMEM((nch, ch, tq), I16),
                        pltpu.VMEM((nch, ch, tq), I16), pltpu.VMEM((nch, ch, tq), F32),
                        pltpu.VMEM((1, tq), jnp.int32), pltpu.VMEM((1, tq), jnp.int32),
                        pltpu.VMEM((2, 1, tq), F32)]
        + _attn_scratch(DSA_HEADS, DSA_HEAD_DIM, tq, ch),
        compiler_params=pltpu.CompilerParams(
            dimension_semantics=("parallel", "arbitrary"), vmem_limit_bytes=VMEM_LIMIT),
        name="dsa",
    )(qb, kb, vbt, qi, ki, wi, kpos)


def _post_kernel(x_ref, oa_ref, ob_ref, gate_ref, wa_ref, wb_ref, wo_ref, g1_ref, b1_ref,
                 wfi_ref, wfd_ref, g2_ref, b2_ref, out_ref):
    ya = _dot(oa_ref[...], wa_ref[...])
    yb = _dot(ob_ref[...], wb_ref[...])
    mixed = (_sigmoid(gate_ref[:, :D_MODEL].astype(F32)) * ya
             + _sigmoid(gate_ref[:, D_MODEL:].astype(F32)) * yb)
    mix_out = _dot(mixed.astype(BF16), wo_ref[...])
    h1 = _layer_norm(ALPHA * x_ref[...] + mix_out, g1_ref[...], b1_ref[...])
    gu = _dot(h1.astype(BF16), wfi_ref[...])
    gate = gu[:, :D_FF]
    act = gate * _sigmoid(gate) * gu[:, D_FF:]
    f = _dot(act.astype(BF16), wfd_ref[...])
    out_ref[...] = _layer_norm(ALPHA * h1 + f, g2_ref[...], b2_ref[...])


def _post_call(x2, oa, ob, gates, weights, rows):
    n = x2.shape[0]
    row_spec = lambda wdt: pl.BlockSpec((rows, wdt), lambda i: (i, 0))
    return pl.pallas_call(
        _post_kernel,
        grid=(n // rows,),
        in_specs=[row_spec(D_MODEL), row_spec(oa.shape[1]), row_spec(ob.shape[1]),
                  row_spec(2 * D_MODEL)] + [_const_spec(a.shape) for a in weights],
        out_specs=row_spec(D_MODEL),
        out_shape=jax.ShapeDtypeStruct((n, D_MODEL), F32),
        compiler_params=pltpu.CompilerParams(
            dimension_semantics=("parallel",), vmem_limit_bytes=VMEM_LIMIT),
        name="post",
    )(x2, oa, ob, gates, *weights)


def _pad_cols(w, width):
    return jnp.pad(w, ((0, 0), (0, width - w.shape[1])))


def _prep_proj_weights(w_in, q_norm, kv_norm, w_uq, w_ukv):
    sizes = (MLA_Q_RANK, MLA_KV_RANK, MLA_ROPE, 512, 512, 512, IDX_HEADS * IDX_DIM, IDX_DIM,
             IDX_HEADS, D_MODEL, D_MODEL)
    parts, start = [], 0
    for n in sizes:
        parts.append(w_in[:, start:start + n])
        start += n
    w_cq, w_ckv, w_kr, w_qb, w_kb, w_vb, w_qi, w_ki, w_wi, w_ga, w_gb = parts
    half = MLA_ROPE // 2
    d = w_in.shape[0]

    def rope_group(first, second):
        return jnp.concatenate([jnp.zeros((d, MLA_NOPE), F32), first, second,
                                jnp.zeros((d, LANES - MLA_QK), F32)], axis=1)

    wc = jnp.concatenate([w_cq, w_ckv], axis=1)
    wkr = jnp.concatenate([rope_group(w_kr[:, :half], w_kr[:, half:]),
                           rope_group(w_kr[:, half:], w_kr[:, :half])], axis=1)
    wdsa = jnp.concatenate([w_qb * DSA_SCALE, w_kb], axis=1)
    widx = jnp.concatenate([w_qi, w_ki, w_ki, w_ki, w_ki, _pad_cols(w_wi, LANES)], axis=1)
    wgate = jnp.concatenate([w_ga, w_gb], axis=1)

    r = w_uq.shape[0]
    uq = w_uq.reshape(r, MLA_HEADS, MLA_QK)
    zq = lambda n: jnp.zeros((r, MLA_HEADS, n), F32)
    wuqm = jnp.concatenate([uq, zq(LANES - MLA_QK)], axis=2).reshape(r, MLA_HEADS * LANES)
    wuqs = jnp.concatenate([zq(MLA_NOPE), uq[:, :, MLA_NOPE + half:], uq[:, :, MLA_NOPE:MLA_NOPE + half],
                            zq(LANES - MLA_QK)], axis=2).reshape(r, MLA_HEADS * LANES)
    ukv = w_ukv.reshape(r, MLA_HEADS, MLA_NOPE + MLA_V)
    wuk = jnp.concatenate([ukv[:, :, :MLA_NOPE], zq(LANES - MLA_NOPE)], axis=2).reshape(r, MLA_HEADS * LANES)
    wuvt = ukv[:, :, MLA_NOPE:].reshape(r, MLA_HEADS * MLA_V).T

    bf = lambda a: a.astype(BF16)
    return (bf(wc), bf(wkr), bf(wdsa), bf(w_vb.T), bf(widx), bf(wgate), q_norm.reshape(1, -1),
            kv_norm.reshape(1, -1), bf(wuqm), bf(wuqs), bf(wuk), bf(wuvt))


def _rope_freq():
    half = MLA_ROPE // 2
    return (ROPE_BASE ** (-jnp.arange(half, dtype=F32) / half)).reshape(half, 1)


def kernel(x, positions, w_in, mla_q_norm, mla_kv_norm, mla_w_uq, mla_w_ukv, w_branch_a, w_branch_b,
           w_out, ln1_g, ln1_b, ffn_w_in, ffn_w_down, ln2_g, ln2_b):
    b, s, d = x.shape
    assert w_in.shape[0] == DEPTH and d == D_MODEL
    assert s & (s - 1) == 0 and s % ATT_TQ == 0 and s % (2 * ATT_CH) == 0 and PROJ_ROWS % ATT_CH == 0
    assert (b * s) % PROJ_ROWS == 0 and s <= POS_SPLIT * 256
    n = b * s
    h = x.reshape(n, d)
    pos2 = positions.reshape(n, 1)
    posr = positions.reshape(n // PROJ_ROWS, 1, PROJ_ROWS)
    freq = _rope_freq()
    for l in range(DEPTH):
        pw = _prep_proj_weights(w_in[l], mla_q_norm[l], mla_kv_norm[l], mla_w_uq[l], mla_w_ukv[l])
        qa, ka, vat, qb, kb, vbt, qi, ki, wi, kpos, gates = _proj_call(
            h, pos2, posr, freq, pw, PROJ_ROWS, ATT_CH)
        r3 = lambda a: a.reshape(b, s, a.shape[1])
        t4 = lambda a: a.reshape(b, s // ATT_CH, a.shape[1], ATT_CH)
        o_a = _mla_call(r3(qa), r3(ka), t4(vat), ATT_TQ, ATT_CH)
        o_b = _dsa_call(r3(qb), r3(kb), t4(vbt), r3(qi), r3(ki), r3(wi), r3(kpos), ATT_TQ, ATT_CH)
        row = lambda v: v.reshape(1, -1)
        post_w = (w_branch_a[l].astype(BF16), w_branch_b[l].astype(BF16), w_out[l].astype(BF16),
                  row(ln1_g[l]), row(ln1_b[l]), ffn_w_in[l].astype(BF16), ffn_w_down[l].astype(BF16),
                  row(ln2_g[l]), row(ln2_b[l]))
        h = _post_call(h, o_a.reshape(n, -1), o_b.reshape(n, -1), gates, post_w, POST_ROWS)
    return h.reshape(b, s, d)
```

```python
import functools
import math
import struct

import jax
import jax.numpy as jnp
from jax import lax
from jax.experimental import pallas as pl
from jax.experimental.pallas import tpu as pltpu

F32 = jnp.float32
BF16 = jnp.bfloat16

D_MODEL = 1024
MLA_HEADS = 8
MLA_Q_RANK = 256
MLA_KV_RANK = 256
MLA_NOPE = 64
MLA_ROPE = 32
MLA_V = 64
ROPE_BASE = 10000.0
DSA_HEADS = 8
DSA_HEAD_DIM = 64
IDX_HEADS = 8
IDX_DIM = 32
TOPK_MAX = 256
D_FF = 2816
DEPTH = 1
ALPHA = (2 * DEPTH) ** 0.25
LN_EPS = 1e-5
RMS_EPS = 1e-6

LANES = 128
SUBLANES = 8
MLA_QK = MLA_NOPE + MLA_ROPE
MLA_SCALE = 1.0 / math.sqrt(MLA_QK)
DSA_SCALE = 1.0 / math.sqrt(DSA_HEAD_DIM)
IDX_SCALE = 1.0 / math.sqrt(IDX_DIM * IDX_HEADS)
ALIBI_SLOPES = tuple(2.0 ** (-8.0 * (i + 1) / DSA_HEADS) for i in range(DSA_HEADS))
POS_SPLIT = 64
POS_TERMS = 3
LOG2E = math.log2(math.e)

NEG = -1e30
INT_MIN = -(2 ** 31)
I16 = jnp.int16
I16_MIN = -(2 ** 15)

PROJ_ROWS = 512
POST_ROWS = 512
ATT_TQ = 256
ATT_CH = 256
SEARCH_UNROLL = 4
VMEM_LIMIT = 56 * 1024 * 1024


def _const_spec(shape):
    zeros = (0,) * len(shape)
    return pl.BlockSpec(shape, lambda *_: zeros, pipeline_mode=pl.Buffered(1))


def _dot(a, b):
    return jnp.dot(a, b, preferred_element_type=F32)


def _dot_nt(a, b):
    return lax.dot_general(a, b, (((1,), (1,)), ((), ())), preferred_element_type=F32)


def _rms_norm(x, g):
    return x * lax.rsqrt(jnp.mean(x * x, axis=-1, keepdims=True) + RMS_EPS) * g


def _layer_norm(x, g, b):
    mu = jnp.mean(x, axis=-1, keepdims=True)
    xc = x - mu
    var = jnp.mean(xc * xc, axis=-1, keepdims=True)
    return xc * lax.rsqrt(var + LN_EPS) * g + b


def _sigmoid(x):
    return 1.0 / (1.0 + jnp.exp(-x))


def _bf16_terms(x, n):
    terms = []
    for _ in range(n):
        bits = struct.unpack("<I", struct.pack("<f", x))[0]
        bits = (bits + 0x7FFF + ((bits >> 16) & 1)) & 0xFFFF0000
        term = struct.unpack("<f", struct.pack("<I", bits))[0]
        terms.append(term)
        x -= term
    return terms


def _slab_rows(dtype):
    return SUBLANES * (4 // jnp.dtype(dtype).itemsize)


def _fold_rows(x, op, ways=4):
    slab = _slab_rows(x.dtype)
    parts = [x[j * slab:(j + 1) * slab] for j in range(x.shape[0] // slab)]
    accs = parts[:ways]
    for j in range(ways, len(parts)):
        accs[j % ways] = op(accs[j % ways], parts[j])
    while len(accs) > 1:
        accs = [op(accs[k], accs[k + 1]) for k in range(0, len(accs) - 1, 2)] + accs[len(accs) & ~1:]
    return accs[0]


def _tall(x, n):
    return x if n == 1 else jnp.concatenate([x] * n, axis=0)


def _attention(nvis, heads, prep, logits, values_t, o_ref, scratch):
    s_scr, cmax_scr, m_scr, acc_scr = scratch
    dv = acc_scr.shape[1] - _slab_rows(BF16)
    ones = jnp.ones((_slab_rows(BF16), s_scr.shape[2]), BF16)
    m_scr[...] = jnp.full(m_scr.shape, NEG, F32)
    acc_scr[...] = jnp.zeros(acc_scr.shape, F32)

    def logits_part(c, slot):
        ctx = prep(jnp.asarray(c, jnp.int32))
        for h in range(heads):
            s_t = logits(ctx, h)
            s_scr[h, slot] = s_t
            cmax_scr[h, slot] = _fold_rows(s_t, jnp.maximum)

    def values_part(c, slot):
        for h in range(heads):
            m_prev = m_scr[h]
            m_new = cmax_scr[h, slot]
            for shift in (4, 2, 1):
                m_new = jnp.maximum(m_new, pltpu.roll(m_new, shift, axis=0))
            m_new = jnp.maximum(m_prev, m_new)
            alpha = jnp.exp2(m_prev - m_new)
            p = jnp.exp2(s_scr[h, slot] - _tall(m_new, s_scr.shape[2] // SUBLANES))
            v_aug = jnp.concatenate([values_t(c, h), ones], axis=0)
            acc_scr[h] = (_tall(alpha, acc_scr.shape[1] // SUBLANES) * acc_scr[h]
                          + _dot(v_aug, p.astype(BF16)))
            m_scr[h] = m_new

    logits_part(0, 0)

    def step(c, slot):
        values_part(c, slot)
        logits_part(c + 1, 1 - slot)

    def body(pair, _):
        step(2 * pair, 0)
        step(2 * pair + 1, 1)
        return 0

    lax.fori_loop(0, (nvis - 1) // 2, body, 0)

    @pl.when(nvis % 2 == 0)
    def _():
        step(nvis - 2, 0)
        values_part(nvis - 1, 1)

    @pl.when(nvis % 2 == 1)
    def _():
        values_part(nvis - 1, 0)

    norm = lambda h: acc_scr[h, :dv] / acc_scr[h, dv:dv + 1]
    for pair in range(heads // 2):
        o_t = jnp.concatenate([norm(2 * pair), norm(2 * pair + 1)], axis=0)
        o_ref[:, pair * LANES:(pair + 1) * LANES] = o_t.T.astype(BF16)


def _attn_scratch(heads, dv, tq, ch):
    return [pltpu.VMEM((heads, 2, ch, tq), F32), pltpu.VMEM((heads, 2, SUBLANES, tq), F32),
            pltpu.VMEM((heads, SUBLANES, tq), F32),
            pltpu.VMEM((heads, dv + _slab_rows(BF16), tq), F32)]


def _proj_kernel(x_ref, pos_ref, posr_ref, freq_ref, wc_ref, wkr_ref, wdsa_ref, wvbt_ref, widx_ref,
                 wgate_ref, qn_ref, kvn_ref, wuqm_ref, wuqs_ref, wuk_ref, wuvt_ref,
                 qa_ref, ka_ref, vat_ref, qb_ref, kb_ref, vbt_ref, qi_ref, ki_ref, wi_ref,
                 kpos_ref, gate_ref, *, ch):
    rows = x_ref.shape[0]
    xb = x_ref[...].astype(BF16)
    pos = pos_ref[...]
    ang = freq_ref[...] * posr_ref[...].astype(F32)
    cos_c, sin_c = jnp.cos(ang), jnp.sin(ang)
    cos_t = jnp.concatenate([jnp.ones((MLA_NOPE, rows), F32), cos_c, cos_c,
                             jnp.ones((LANES - MLA_QK, rows), F32)], axis=0).T
    sin_t = jnp.concatenate([jnp.zeros((MLA_NOPE, rows), F32), -sin_c, sin_c,
                             jnp.zeros((LANES - MLA_QK, rows), F32)], axis=0).T

    c = _dot(xb, wc_ref[...])
    cq = _rms_norm(c[:, :MLA_Q_RANK], qn_ref[...]).astype(BF16)
    ckv = _rms_norm(c[:, MLA_Q_RANK:], kvn_ref[...]).astype(BF16)

    gate_ref[...] = _dot(xb, wgate_ref[...]).astype(BF16)

    d = _dot(xb, wdsa_ref[...])
    w = DSA_HEADS * DSA_HEAD_DIM
    qb_ref[...] = (d[:, :w] * LOG2E).astype(BF16)
    kb_ref[...] = d[:, w:].astype(BF16)

    va_t = _dot_nt(wuvt_ref[...], ckv).astype(BF16)
    vb_t = _dot_nt(wvbt_ref[...], xb).astype(BF16)
    for k in range(rows // ch):
        vat_ref[k] = va_t[:, k * ch:(k + 1) * ch]
        vbt_ref[k] = vb_t[:, k * ch:(k + 1) * ch]

    ix = _dot(xb, widx_ref[...])
    qi_ref[...] = ix[:, :2 * LANES].astype(BF16)
    ki_ref[...] = ix[:, 2 * LANES:3 * LANES].astype(BF16)
    wi_ref[...] = ix[:, 3 * LANES:]

    lane = lax.broadcasted_iota(jnp.int32, (rows, LANES), 1)
    hi = (pos >> (POS_SPLIT.bit_length() - 1)).astype(F32)
    lo = (pos & (POS_SPLIT - 1)).astype(F32)
    kpos_ref[...] = jnp.where(lane < POS_TERMS, hi,
                              jnp.where(lane < 2 * POS_TERMS, lo, 0.0)).astype(BF16)

    qm = _dot(cq, wuqm_ref[...])
    qs = _dot(cq, wuqs_ref[...])
    cos_q = cos_t * (MLA_SCALE * LOG2E)
    sin_q = sin_t * (MLA_SCALE * LOG2E)
    for h in range(MLA_HEADS):
        sl = slice(h * LANES, (h + 1) * LANES)
        qa_ref[:, sl] = (qm[:, sl] * cos_q + qs[:, sl] * sin_q).astype(BF16)

    kr = _dot(xb, wkr_ref[...])
    kr_rot = kr[:, :LANES] * cos_t + kr[:, LANES:] * sin_t
    kn = _dot(ckv, wuk_ref[...])
    for h in range(MLA_HEADS):
        sl = slice(h * LANES, (h + 1) * LANES)
        ka_ref[:, sl] = (kn[:, sl] + kr_rot).astype(BF16)


def _proj_call(x2, pos2, posr, freq, weights, rows, ch):
    n = x2.shape[0]
    (wc, wkr, wdsa, wvbt, widx, wgate, qn, kvn, wuqm, wuqs, wuk, wuvt) = weights
    row_spec = lambda wdt: pl.BlockSpec((rows, wdt), lambda i: (i, 0))
    row_out = lambda wdt, dt: (row_spec(wdt), jax.ShapeDtypeStruct((n, wdt), dt))
    t_out = lambda wdt: (pl.BlockSpec((rows // ch, wdt, ch), lambda i: (i, 0, 0)),
                         jax.ShapeDtypeStruct((n // ch, wdt, ch), BF16))
    outs = [row_out(8 * LANES, BF16), row_out(8 * LANES, BF16), t_out(MLA_HEADS * MLA_V),
            row_out(512, BF16), row_out(512, BF16), t_out(DSA_HEADS * DSA_HEAD_DIM),
            row_out(2 * LANES, BF16), row_out(LANES, BF16), row_out(LANES, F32),
            row_out(LANES, BF16), row_out(2 * D_MODEL, BF16)]
    consts = (freq, wc, wkr, wdsa, wvbt, widx, wgate, qn, kvn, wuqm, wuqs, wuk, wuvt)
    return pl.pallas_call(
        functools.partial(_proj_kernel, ch=ch),
        grid=(n // rows,),
        in_specs=[row_spec(D_MODEL), row_spec(1), pl.BlockSpec((None, 1, rows), lambda i: (i, 0, 0))]
        + [_const_spec(a.shape) for a in consts],
        out_specs=[o[0] for o in outs],
        out_shape=[o[1] for o in outs],
        compiler_params=pltpu.CompilerParams(
            dimension_semantics=("parallel",), vmem_limit_bytes=VMEM_LIMIT),
        name="proj",
    )(x2, pos2, posr, *consts)


def _mla_kernel(q_ref, k_ref, vt_ref, o_ref, *scratch, tq, ch):
    i = pl.program_id(1)
    nvis = ((i + 1) * tq + ch - 1) // ch
    key = lax.broadcasted_iota(jnp.int32, (ch, tq), 0)
    query = i * tq + lax.broadcasted_iota(jnp.int32, (ch, tq), 1)

    def prep(c):
        ks = pl.multiple_of(c * ch, ch)
        return ks, jnp.where(key + ks <= query, 0.0, NEG)

    def logits(ctx, h):
        ks, causal = ctx
        hsl = slice(h * LANES, (h + 1) * LANES)
        return _dot_nt(k_ref[pl.ds(ks, ch), hsl], q_ref[:, hsl]) + causal

    def values_t(c, h):
        return vt_ref[c, h * MLA_V:(h + 1) * MLA_V, :]

    _attention(nvis, MLA_HEADS, prep, logits, values_t, o_ref, scratch)


def _mla_call(qa, ka, vat, tq, ch):
    b, s, _ = qa.shape
    wv = MLA_HEADS * MLA_V
    return pl.pallas_call(
        functools.partial(_mla_kernel, tq=tq, ch=ch),
        grid=(b, s // tq),
        in_specs=[
            pl.BlockSpec((None, tq, 8 * LANES), lambda bi, i: (bi, i, 0)),
            pl.BlockSpec((None, s, 8 * LANES), lambda bi, i: (bi, 0, 0)),
            pl.BlockSpec((None, s // ch, wv, ch), lambda bi, i: (bi, 0, 0, 0)),
        ],
        out_specs=pl.BlockSpec((None, tq, wv), lambda bi, i: (bi, i, 0)),
        out_shape=jax.ShapeDtypeStruct((b, s, wv), BF16),
        scratch_shapes=_attn_scratch(MLA_HEADS, MLA_V, tq, ch),
        compiler_params=pltpu.CompilerParams(
            dimension_semantics=("parallel", "arbitrary"), vmem_limit_bytes=VMEM_LIMIT),
        name="mla",
    )(qa, ka, vat)


def _dsa_kernel(qb_ref, kb_ref, vbt_ref, qi_ref, ki_ref, wi_ref, kpos_ref, o_ref,
                qih_scr, wih_scr, key_scr, hi_scr, lo_scr, bias_scr, jsel_scr, tau_scr, cnt_scr,
                *scratch,
                tq, ch, topk, seq):
    i = pl.program_id(1)
    nvis = ((i + 1) * tq + ch - 1) // ch
    key_idx = lax.broadcasted_iota(jnp.int32, (ch, tq), 0)
    query = i * tq + lax.broadcasted_iota(jnp.int32, (ch, tq), 1)
    lane = lax.broadcasted_iota(jnp.int32, (tq, LANES), 1)

    qi = qi_ref[...]
    wi_t = wi_ref[...].T * IDX_SCALE
    for h in range(IDX_HEADS):
        group = qi[:, (h // 4) * LANES:(h // 4 + 1) * LANES]
        lo = IDX_DIM * (h % 4)
        keep = jnp.where(lane >= lo, jnp.where(lane < lo + IDX_DIM, 1.0, 0.0), 0.0).astype(BF16)
        qih_scr[h] = group * keep
        wih_scr[h] = wi_t[h:h + 1, :]

    def score_chunk(c):
        ks = pl.multiple_of(c * ch, ch)
        kic = ki_ref[pl.ds(ks, ch), :]
        sc = jnp.zeros((ch, tq), F32)
        for h in range(IDX_HEADS):
            sc = sc + wih_scr[h] * jnp.maximum(_dot_nt(kic, qih_scr[h]), 0.0)
        sc = jnp.where(sc == 0.0, 0.0, sc)
        bits = pltpu.bitcast(sc, jnp.int32)
        key = bits ^ ((bits >> 31) & 0x7FFFFFFF)
        key = jnp.where(key_idx + ks <= query, key, INT_MIN)
        key_scr[c] = key
        hi_scr[c] = (key >> 16).astype(I16)
        lo_scr[c] = (((key ^ 0x8000) << 16) >> 16).astype(I16)

    def score_pair(pair, _):
        score_chunk(2 * pair)
        score_chunk(2 * pair + 1)
        return 0

    lax.fori_loop(0, nvis // 2, score_pair, 0)

    @pl.when(nvis % 2 == 1)
    def _():
        score_chunk(nvis - 1)

    def count(pred):
        def body(c, acc):
            return acc + _fold_rows(pred(key_scr[c], c), jnp.add)
        acc = lax.fori_loop(0, nvis, body, jnp.zeros((SUBLANES, tq), F32))
        return jnp.sum(acc, axis=0, keepdims=True)

    def search16(half_scr, nscan, target, cnt_all):
        slab = _slab_rows(I16)

        def count16(cand):
            cand16 = _tall(_tall(cand, slab // SUBLANES).astype(I16), ch // slab)
            acc = jnp.zeros((slab, tq), I16)
            for c in range(nscan):
                ones = jnp.where(half_scr[c] >= cand16, jnp.ones((), I16), jnp.zeros((), I16))
                acc = acc + _fold_rows(ones, jnp.add)
            cnt = _fold_rows(acc.astype(F32), jnp.add)
            for shift in (4, 2, 1):
                cnt = cnt + pltpu.roll(cnt, shift, axis=0)
            return cnt

        def body(_, carry):
            lo, step, cnt_lo, cnt_hi = carry
            cand = lo + step
            cnt = count16(cand)
            ok = cnt >= target
            return (jnp.where(ok, cand, lo), step >> 1,
                    jnp.where(ok, cnt, cnt_lo), jnp.where(ok, cnt_hi, cnt))

        rep = lambda x: jnp.broadcast_to(x, (SUBLANES, tq))
        out = lax.fori_loop(0, 16, body, (jnp.full((SUBLANES, tq), I16_MIN, jnp.int32),
                                          jnp.full((SUBLANES, tq), -I16_MIN, jnp.int32),
                                          rep(cnt_all), jnp.zeros((SUBLANES, tq), F32)),
                            unroll=SEARCH_UNROLL)
        return out[0][:1], out[2][:1], out[3][:1]

    tau_scr[...] = jnp.full((1, tq), INT_MIN, jnp.int32)
    cnt_scr[...] = jnp.zeros(cnt_scr.shape, F32)

    @pl.when(nvis % 2 == 1)
    def _():
        hi_scr[nvis] = jnp.full((ch, tq), I16_MIN, I16)
        lo_scr[nvis] = jnp.full((ch, tq), I16_MIN, I16)

    for pairs in range(1, key_scr.shape[0] // 2 + 1):
        @pl.when(jnp.logical_and((i + 1) * tq > topk, (nvis + 1) // 2 == pairs))
        def _(nscan=2 * pairs):
            all_keys = jnp.full((1, tq), float(nscan * ch), F32)
            tau_hi, cnt_ge_hi, cnt_gt_hi = search16(hi_scr, nscan, float(topk), all_keys)
            tau_hi16 = tau_hi.astype(I16)
            for c in range(nscan):
                lo_scr[c] = jnp.where(hi_scr[c] == tau_hi16, lo_scr[c], jnp.full((), I16_MIN, I16))
            tau_lo, cnt_ge_lo, cnt_gt_lo = search16(lo_scr, nscan, topk - cnt_gt_hi,
                                                    cnt_ge_hi - cnt_gt_hi)
            tau_scr[...] = (tau_hi << 16) + (tau_lo - I16_MIN)
            cnt_scr[0] = cnt_gt_hi + cnt_ge_lo
            cnt_scr[1] = cnt_gt_hi + cnt_gt_lo

    tau = tau_scr[...]
    cnt_ge = cnt_scr[0]
    cnt_gt = cnt_scr[1]
    has_tau = tau > INT_MIN
    need = topk - cnt_gt
    jsel_scr[...] = jnp.where(has_tau, seq, -1)
    excess = jnp.where(has_tau, jnp.where(cnt_ge - cnt_gt > need, 1.0, 0.0), 0.0)

    @pl.when(jnp.max(excess) > 0.0)
    def _():
        def idx_body(it, lo):
            cand = lo + jnp.left_shift(jnp.int32(1), seq.bit_length() - 2 - it)
            cnt = count(lambda k, c: jnp.where(
                k == tau, jnp.where(key_idx + c * ch < cand, 1.0, 0.0), 0.0))
            return jnp.where(cnt < need, cand, lo)
        last = lax.fori_loop(0, seq.bit_length() - 1, idx_body, jnp.zeros((1, tq), jnp.int32))
        jsel_scr[...] = jnp.where(has_tau, last, -1)

    jsel = jsel_scr[...]

    def bias_body(c, _):
        k = key_scr[c]
        tie = jnp.where(key_idx + c * ch <= jsel, 0.0, NEG)
        bias_scr[c] = jnp.where(k > tau, 0.0, jnp.where(k == tau, tie, NEG))
        return 0

    lax.fori_loop(0, nvis, bias_body, 0)

    q_heads = []
    for h in range(DSA_HEADS):
        qpair = qb_ref[:, (h // 2) * LANES:(h // 2 + 1) * LANES]
        if h % 2 == 0:
            keep = jnp.where(lane < DSA_HEAD_DIM, 1.0, 0.0).astype(BF16)
        else:
            keep = jnp.where(lane >= DSA_HEAD_DIM, 1.0, 0.0).astype(BF16)
        terms = _bf16_terms(ALIBI_SLOPES[h] * LOG2E, POS_TERMS)
        q_pos = jnp.zeros((tq, LANES), F32)
        for j, term in enumerate(terms):
            q_pos = jnp.where(lane == j, POS_SPLIT * term, jnp.where(lane == POS_TERMS + j, term, q_pos))
        q_heads.append(jnp.concatenate([qpair * keep, q_pos.astype(BF16)], axis=1))

    def prep(c):
        ks = pl.multiple_of(c * ch, ch)
        return ks, bias_scr[c], kpos_ref[pl.ds(ks, ch), :]

    def logits(ctx, h):
        ks, sel, kpos = ctx
        psl = slice((h // 2) * LANES, (h // 2 + 1) * LANES)
        k_aug = jnp.concatenate([kb_ref[pl.ds(ks, ch), psl], kpos], axis=1)
        return _dot_nt(k_aug, q_heads[h]) + sel

    def values_t(c, h):
        return vbt_ref[c, h * DSA_HEAD_DIM:(h + 1) * DSA_HEAD_DIM, :]

    _attention(nvis, DSA_HEADS, prep, logits, values_t, o_ref, scratch)


def _dsa_call(qb, kb, vbt, qi, ki, wi, kpos, tq, ch):
    b, s, w = qb.shape
    nch = s // ch
    topk = min(TOPK_MAX, s // 4)
    full = lambda wdt: pl.BlockSpec((None, s, wdt), lambda bi, i: (bi, 0, 0))
    tile = lambda wdt: pl.BlockSpec((None, tq, wdt), lambda bi, i: (bi, i, 0))
    return pl.pallas_call(
        functools.partial(_dsa_kernel, tq=tq, ch=ch, topk=topk, seq=s),
        grid=(b, s // tq),
        in_specs=[tile(w), full(w), pl.BlockSpec((None, nch, w, ch), lambda bi, i: (bi, 0, 0, 0)),
                  tile(2 * LANES), full(LANES), tile(LANES), full(LANES)],
        out_specs=tile(w),
        out_shape=jax.ShapeDtypeStruct((b, s, w), BF16),
        scratch_shapes=[pltpu.VMEM((IDX_HEADS, tq, LANES), BF16), pltpu.VMEM((IDX_HEADS, 1, tq), F32),
                        pltpu.VMEM((nch, ch, tq), jnp.int32), pltpu.VMEM((nch, ch, tq), I16),
                        pltpu.VMEM((nch, ch, tq), I16), pltpu.VMEM((nch, ch, tq), F32),
                        pltpu.VMEM((1, tq), jnp.int32), pltpu.VMEM((1, tq), jnp.int32),
                        pltpu.VMEM((2, 1, tq), F32)]
        + _attn_scratch(DSA_HEADS, DSA_HEAD_DIM, tq, ch),
        compiler_params=pltpu.CompilerParams(
            dimension_semantics=("parallel", "arbitrary"), vmem_limit_bytes=VMEM_LIMIT),
        name="dsa",
    )(qb, kb, vbt, qi, ki, wi, kpos)


def _post_kernel(x_ref, oa_ref, ob_ref, gate_ref, wa_ref, wb_ref, wo_ref, g1_ref, b1_ref,
                 wfi_ref, wfd_ref, g2_ref, b2_ref, out_ref):
    ya = _dot(oa_ref[...], wa_ref[...])
    yb = _dot(ob_ref[...], wb_ref[...])
    mixed = (_sigmoid(gate_ref[:, :D_MODEL].astype(F32)) * ya
             + _sigmoid(gate_ref[:, D_MODEL:].astype(F32)) * yb)
    mix_out = _dot(mixed.astype(BF16), wo_ref[...])
    h1 = _layer_norm(ALPHA * x_ref[...] + mix_out, g1_ref[...], b1_ref[...])
    gu = _dot(h1.astype(BF16), wfi_ref[...])
    gate = gu[:, :D_FF]
    act = gate * _sigmoid(gate) * gu[:, D_FF:]
    f = _dot(act.astype(BF16), wfd_ref[...])
    out_ref[...] = _layer_norm(ALPHA * h1 + f, g2_ref[...], b2_ref[...])


def _post_call(x2, oa, ob, gates, weights, rows):
    n = x2.shape[0]
    row_spec = lambda wdt: pl.BlockSpec((rows, wdt), lambda i: (i, 0))
    return pl.pallas_call(
        _post_kernel,
        grid=(n // rows,),
        in_specs=[row_spec(D_MODEL), row_spec(oa.shape[1]), row_spec(ob.shape[1]),
                  row_spec(2 * D_MODEL)] + [_const_spec(a.shape) for a in weights],
        out_specs=row_spec(D_MODEL),
        out_shape=jax.ShapeDtypeStruct((n, D_MODEL), F32),
        compiler_params=pltpu.CompilerParams(
            dimension_semantics=("parallel",), vmem_limit_bytes=VMEM_LIMIT),
        name="post",
    )(x2, oa, ob, gates, *weights)


def _pad_cols(w, width):
    return jnp.pad(w, ((0, 0), (0, width - w.shape[1])))


def _prep_proj_weights(w_in, q_norm, kv_norm, w_uq, w_ukv):
    sizes = (MLA_Q_RANK, MLA_KV_RANK, MLA_ROPE, 512, 512, 512, IDX_HEADS * IDX_DIM, IDX_DIM,
             IDX_HEADS, D_MODEL, D_MODEL)
    parts, start = [], 0
    for n in sizes:
        parts.append(w_in[:, start:start + n])
        start += n
    w_cq, w_ckv, w_kr, w_qb, w_kb, w_vb, w_qi, w_ki, w_wi, w_ga, w_gb = parts
    half = MLA_ROPE // 2
    d = w_in.shape[0]

    def rope_group(first, second):
        return jnp.concatenate([jnp.zeros((d, MLA_NOPE), F32), first, second,
                                jnp.zeros((d, LANES - MLA_QK), F32)], axis=1)

    wc = jnp.concatenate([w_cq, w_ckv], axis=1)
    wkr = jnp.concatenate([rope_group(w_kr[:, :half], w_kr[:, half:]),
                           rope_group(w_kr[:, half:], w_kr[:, :half])], axis=1)
    wdsa = jnp.concatenate([w_qb * DSA_SCALE, w_kb], axis=1)
    widx = jnp.concatenate([w_qi, w_ki, w_ki, w_ki, w_ki, _pad_cols(w_wi, LANES)], axis=1)
    wgate = jnp.concatenate([w_ga, w_gb], axis=1)

    r = w_uq.shape[0]
    uq = w_uq.reshape(r, MLA_HEADS, MLA_QK)
    zq = lambda n: jnp.zeros((r, MLA_HEADS, n), F32)
    wuqm = jnp.concatenate([uq, zq(LANES - MLA_QK)], axis=2).reshape(r, MLA_HEADS * LANES)
    wuqs = jnp.concatenate([zq(MLA_NOPE), uq[:, :, MLA_NOPE + half:], uq[:, :, MLA_NOPE:MLA_NOPE + half],
                            zq(LANES - MLA_QK)], axis=2).reshape(r, MLA_HEADS * LANES)
    ukv = w_ukv.reshape(r, MLA_HEADS, MLA_NOPE + MLA_V)
    wuk = jnp.concatenate([ukv[:, :, :MLA_NOPE], zq(LANES - MLA_NOPE)], axis=2).reshape(r, MLA_HEADS * LANES)
    wuvt = ukv[:, :, MLA_NOPE:].reshape(r, MLA_HEADS * MLA_V).T

    bf = lambda a: a.astype(BF16)
    return (bf(wc), bf(wkr), bf(wdsa), bf(w_vb.T), bf(widx), bf(wgate), q_norm.reshape(1, -1),
            kv_norm.reshape(1, -1), bf(wuqm), bf(wuqs), bf(wuk), bf(wuvt))


def _rope_freq():
    half = MLA_ROPE // 2
    return (ROPE_BASE ** (-jnp.arange(half, dtype=F32) / half)).reshape(half, 1)


def kernel(x, positions, w_in, mla_q_norm, mla_kv_norm, mla_w_uq, mla_w_ukv, w_branch_a, w_branch_b,
           w_out, ln1_g, ln1_b, ffn_w_in, ffn_w_down, ln2_g, ln2_b):
    b, s, d = x.shape
    assert w_in.shape[0] == DEPTH and d == D_MODEL
    assert s & (s - 1) == 0 and s % ATT_TQ == 0 and s % (2 * ATT_CH) == 0 and PROJ_ROWS % ATT_CH == 0
    assert (b * s) % PROJ_ROWS == 0 and s <= POS_SPLIT * 256
    n = b * s
    h = x.reshape(n, d)
    pos2 = positions.reshape(n, 1)
    posr = positions.reshape(n // PROJ_ROWS, 1, PROJ_ROWS)
    freq = _rope_freq()
    for l in range(DEPTH):
        pw = _prep_proj_weights(w_in[l], mla_q_norm[l], mla_kv_norm[l], mla_w_uq[l], mla_w_ukv[l])
        qa, ka, vat, qb, kb, vbt, qi, ki, wi, kpos, gates = _proj_call(
            h, pos2, posr, freq, pw, PROJ_ROWS, ATT_CH)
        r3 = lambda a: a.reshape(b, s, a.shape[1])
        t4 = lambda a: a.reshape(b, s // ATT_CH, a.shape[1], ATT_CH)
        o_a = _mla_call(r3(qa), r3(ka), t4(vat), ATT_TQ, ATT_CH)
        o_b = _dsa_call(r3(qb), r3(kb), t4(vbt), r3(qi), r3(ki), r3(wi), r3(kpos), ATT_TQ, ATT_CH)
        row = lambda v: v.reshape(1, -1)
        post_w = (w_branch_a[l].astype(BF16), w_branch_b[l].astype(BF16), w_out[l].astype(BF16),
                  row(ln1_g[l]), row(ln1_b[l]), ffn_w_in[l].astype(BF16), ffn_w_down[l].astype(BF16),
                  row(ln2_g[l]), row(ln2_b[l]))
        h = _post_call(h, o_a.reshape(n, -1), o_b.reshape(n, -1), gates, post_w, POST_ROWS)
    return h.reshape(b, s, d)
```

```python
import functools
import math
import struct

import jax
import jax.numpy as jnp
from jax import lax
from jax.experimental import pallas as pl
from jax.experimental.pallas import tpu as pltpu

F32 = jnp.float32
BF16 = jnp.bfloat16

D_MODEL = 1024
MLA_HEADS = 8
MLA_Q_RANK = 256
MLA_KV_RANK = 256
MLA_NOPE = 64
MLA_ROPE = 32
MLA_V = 64
ROPE_BASE = 10000.0
DSA_HEADS = 8
DSA_HEAD_DIM = 64
IDX_HEADS = 8
IDX_DIM = 32
TOPK_MAX = 256
D_FF = 2816
DEPTH = 1
ALPHA = (2 * DEPTH) ** 0.25
LN_EPS = 1e-5
RMS_EPS = 1e-6

LANES = 128
SUBLANES = 8
MLA_QK = MLA_NOPE + MLA_ROPE
MLA_SCALE = 1.0 / math.sqrt(MLA_QK)
DSA_SCALE = 1.0 / math.sqrt(DSA_HEAD_DIM)
IDX_SCALE = 1.0 / math.sqrt(IDX_DIM * IDX_HEADS)
ALIBI_SLOPES = tuple(2.0 ** (-8.0 * (i + 1) / DSA_HEADS) for i in range(DSA_HEADS))
POS_SPLIT = 64
POS_TERMS = 3
LOG2E = math.log2(math.e)

NEG = -1e30
INT_MIN = -(2 ** 31)
I16 = jnp.int16
I16_MIN = -(2 ** 15)

PROJ_ROWS = 512
POST_ROWS = 512
ATT_TQ = 256
ATT_CH = 256
SEARCH_UNROLL = 2
VMEM_LIMIT = 56 * 1024 * 1024


def _const_spec(shape):
    zeros = (0,) * len(shape)
    return pl.BlockSpec(shape, lambda *_: zeros, pipeline_mode=pl.Buffered(1))


def _dot(a, b):
    return jnp.dot(a, b, preferred_element_type=F32)


def _dot_nt(a, b):
    return lax.dot_general(a, b, (((1,), (1,)), ((), ())), preferred_element_type=F32)


def _rms_norm(x, g):
    return x * lax.rsqrt(jnp.mean(x * x, axis=-1, keepdims=True) + RMS_EPS) * g


def _layer_norm(x, g, b):
    mu = jnp.mean(x, axis=-1, keepdims=True)
    xc = x - mu
    var = jnp.mean(xc * xc, axis=-1, keepdims=True)
    return xc * lax.rsqrt(var + LN_EPS) * g + b


def _sigmoid(x):
    return 1.0 / (1.0 + jnp.exp(-x))


def _bf16_terms(x, n):
    terms = []
    for _ in range(n):
        bits = struct.unpack("<I", struct.pack("<f", x))[0]
        bits = (bits + 0x7FFF + ((bits >> 16) & 1)) & 0xFFFF0000
        term = struct.unpack("<f", struct.pack("<I", bits))[0]
        terms.append(term)
        x -= term
    return terms


def _slab_rows(dtype):
    return SUBLANES * (4 // jnp.dtype(dtype).itemsize)


def _fold_rows(x, op, ways=4):
    slab = _slab_rows(x.dtype)
    parts = [x[j * slab:(j + 1) * slab] for j in range(x.shape[0] // slab)]
    accs = parts[:ways]
    for j in range(ways, len(parts)):
        accs[j % ways] = op(accs[j % ways], parts[j])
    while len(accs) > 1:
        accs = [op(accs[k], accs[k + 1]) for k in range(0, len(accs) - 1, 2)] + accs[len(accs) & ~1:]
    return accs[0]


def _tall(x, n):
    return x if n == 1 else jnp.concatenate([x] * n, axis=0)


def _sublane_allreduce(x, op, red_ref):
    outs = []
    for t in range(x.shape[1] // LANES):
        red_ref[t] = x[:, t * LANES:(t + 1) * LANES]
        rows = [jnp.broadcast_to(red_ref[t, r:r + 1, :], (SUBLANES, LANES)) for r in range(SUBLANES)]
        while len(rows) > 1:
            rows = [op(rows[k], rows[k + 1]) for k in range(0, len(rows), 2)]
        outs.append(rows[0])
    return outs[0] if len(outs) == 1 else jnp.concatenate(outs, axis=1)


def _attention(nvis, heads, prep, logits, values_t, o_ref, scratch):
    s_scr, cmax_scr, m_scr, acc_scr, red_scr = scratch
    dv = acc_scr.shape[1] - _slab_rows(BF16)
    ones = jnp.ones((_slab_rows(BF16), s_scr.shape[2]), BF16)
    m_scr[...] = jnp.full(m_scr.shape, NEG, F32)
    acc_scr[...] = jnp.zeros(acc_scr.shape, F32)

    def logits_part(c, slot):
        ctx = prep(jnp.asarray(c, jnp.int32))
        for h in range(heads):
            s_t = logits(ctx, h)
            s_scr[h, slot] = s_t
            cmax_scr[h, slot] = _fold_rows(s_t, jnp.maximum)

    def values_part(c, slot):
        for h in range(heads):
            m_prev = m_scr[h]
            m_new = jnp.maximum(
                m_prev, _sublane_allreduce(cmax_scr[h, slot], jnp.maximum, red_scr.at[h]))
            alpha = jnp.exp2(m_prev - m_new)
            p = jnp.exp2(s_scr[h, slot] - _tall(m_new, s_scr.shape[2] // SUBLANES))
            v_aug = jnp.concatenate([values_t(c, h), ones], axis=0)
            acc_scr[h] = (_tall(alpha, acc_scr.shape[1] // SUBLANES) * acc_scr[h]
                          + _dot(v_aug, p.astype(BF16)))
            m_scr[h] = m_new

    logits_part(0, 0)

    def step(c, slot):
        values_part(c, slot)
        logits_part(c + 1, 1 - slot)

    def body(pair, _):
        step(2 * pair, 0)
        step(2 * pair + 1, 1)
        return 0

    lax.fori_loop(0, (nvis - 1) // 2, body, 0)

    @pl.when(nvis % 2 == 0)
    def _():
        step(nvis - 2, 0)
        values_part(nvis - 1, 1)

    @pl.when(nvis % 2 == 1)
    def _():
        values_part(nvis - 1, 0)

    norm = lambda h: acc_scr[h, :dv] / acc_scr[h, dv:dv + 1]
    for pair in range(heads // 2):
        o_t = jnp.concatenate([norm(2 * pair), norm(2 * pair + 1)], axis=0)
        o_ref[:, pair * LANES:(pair + 1) * LANES] = o_t.T.astype(BF16)


def _attn_scratch(heads, dv, tq, ch):
    return [pltpu.VMEM((heads, 2, ch, tq), F32), pltpu.VMEM((heads, 2, SUBLANES, tq), F32),
            pltpu.VMEM((heads, SUBLANES, tq), F32),
            pltpu.VMEM((heads, dv + _slab_rows(BF16), tq), F32),
            pltpu.VMEM((heads, tq // LANES, SUBLANES, LANES), F32)]


def _proj_kernel(x_ref, pos_ref, posr_ref, freq_ref, wc_ref, wkr_ref, wdsa_ref, wvbt_ref, widx_ref,
                 wgate_ref, qn_ref, kvn_ref, wuqm_ref, wuqs_ref, wuk_ref, wuvt_ref,
                 qa_ref, ka_ref, vat_ref, qb_ref, kb_ref, vbt_ref, qi_ref, ki_ref, wi_ref,
                 kpos_ref, gate_ref, *, ch):
    rows = x_ref.shape[0]
    xb = x_ref[...].astype(BF16)
    pos = pos_ref[...]
    ang = freq_ref[...] * posr_ref[...].astype(F32)
    cos_c, sin_c = jnp.cos(ang), jnp.sin(ang)
    cos_t = jnp.concatenate([jnp.ones((MLA_NOPE, rows), F32), cos_c, cos_c,
                             jnp.ones((LANES - MLA_QK, rows), F32)], axis=0).T
    sin_t = jnp.concatenate([jnp.zeros((MLA_NOPE, rows), F32), -sin_c, sin_c,
                             jnp.zeros((LANES - MLA_QK, rows), F32)], axis=0).T

    c = _dot(xb, wc_ref[...])
    cq = _rms_norm(c[:, :MLA_Q_RANK], qn_ref[...]).astype(BF16)
    ckv = _rms_norm(c[:, MLA_Q_RANK:], kvn_ref[...]).astype(BF16)

    gate_ref[...] = _dot(xb, wgate_ref[...]).astype(BF16)

    d = _dot(xb, wdsa_ref[...])
    w = DSA_HEADS * DSA_HEAD_DIM
    qb_ref[...] = (d[:, :w] * LOG2E).astype(BF16)
    kb_ref[...] = d[:, w:].astype(BF16)

    va_t = _dot_nt(wuvt_ref[...], ckv).astype(BF16)
    vb_t = _dot_nt(wvbt_ref[...], xb).astype(BF16)
    for k in range(rows // ch):
        vat_ref[k] = va_t[:, k * ch:(k + 1) * ch]
        vbt_ref[k] = vb_t[:, k * ch:(k + 1) * ch]

    ix = _dot(xb, widx_ref[...])
    qi_ref[...] = ix[:, :2 * LANES].astype(BF16)
    ki_ref[...] = ix[:, 2 * LANES:3 * LANES].astype(BF16)
    wi_ref[...] = ix[:, 3 * LANES:]

    lane = lax.broadcasted_iota(jnp.int32, (rows, LANES), 1)
    hi = (pos >> (POS_SPLIT.bit_length() - 1)).astype(F32)
    lo = (pos & (POS_SPLIT - 1)).astype(F32)
    kpos_ref[...] = jnp.where(lane < POS_TERMS, hi,
                              jnp.where(lane < 2 * POS_TERMS, lo, 0.0)).astype(BF16)

    qm = _dot(cq, wuqm_ref[...])
    qs = _dot(cq, wuqs_ref[...])
    cos_q = cos_t * (MLA_SCALE * LOG2E)
    sin_q = sin_t * (MLA_SCALE * LOG2E)
    for h in range(MLA_HEADS):
        sl = slice(h * LANES, (h + 1) * LANES)
        qa_ref[:, sl] = (qm[:, sl] * cos_q + qs[:, sl] * sin_q).astype(BF16)

    kr = _dot(xb, wkr_ref[...])
    kr_rot = kr[:, :LANES] * cos_t + kr[:, LANES:] * sin_t
    kn = _dot(ckv, wuk_ref[...])
    for h in range(MLA_HEADS):
        sl = slice(h * LANES, (h + 1) * LANES)
        ka_ref[:, sl] = (kn[:, sl] + kr_rot).astype(BF16)


def _proj_call(x2, pos2, posr, freq, weights, rows, ch):
    n = x2.shape[0]
    (wc, wkr, wdsa, wvbt, widx, wgate, qn, kvn, wuqm, wuqs, wuk, wuvt) = weights
    row_spec = lambda wdt: pl.BlockSpec((rows, wdt), lambda i: (i, 0))
    row_out = lambda wdt, dt: (row_spec(wdt), jax.ShapeDtypeStruct((n, wdt), dt))
    t_out = lambda wdt: (pl.BlockSpec((rows // ch, wdt, ch), lambda i: (i, 0, 0)),
                         jax.ShapeDtypeStruct((n // ch, wdt, ch), BF16))
    outs = [row_out(8 * LANES, BF16), row_out(8 * LANES, BF16), t_out(MLA_HEADS * MLA_V),
            row_out(512, BF16), row_out(512, BF16), t_out(DSA_HEADS * DSA_HEAD_DIM),
            row_out(2 * LANES, BF16), row_out(LANES, BF16), row_out(LANES, F32),
            row_out(LANES, BF16), row_out(2 * D_MODEL, BF16)]
    consts = (freq, wc, wkr, wdsa, wvbt, widx, wgate, qn, kvn, wuqm, wuqs, wuk, wuvt)
    return pl.pallas_call(
        functools.partial(_proj_kernel, ch=ch),
        grid=(n // rows,),
        in_specs=[row_spec(D_MODEL), row_spec(1), pl.BlockSpec((None, 1, rows), lambda i: (i, 0, 0))]
        + [_const_spec(a.shape) for a in consts],
        out_specs=[o[0] for o in outs],
        out_shape=[o[1] for o in outs],
        compiler_params=pltpu.CompilerParams(
            dimension_semantics=("parallel",), vmem_limit_bytes=VMEM_LIMIT),
        name="proj",
    )(x2, pos2, posr, *consts)


def _mla_kernel(q_ref, k_ref, vt_ref, o_ref, *scratch, tq, ch):
    i = pl.program_id(1)
    nvis = ((i + 1) * tq + ch - 1) // ch
    key = lax.broadcasted_iota(jnp.int32, (ch, tq), 0)
    query = i * tq + lax.broadcasted_iota(jnp.int32, (ch, tq), 1)

    def prep(c):
        ks = pl.multiple_of(c * ch, ch)
        return ks, jnp.where(key + ks <= query, 0.0, NEG)

    def logits(ctx, h):
        ks, causal = ctx
        hsl = slice(h * LANES, (h + 1) * LANES)
        return _dot_nt(k_ref[pl.ds(ks, ch), hsl], q_ref[:, hsl]) + causal

    def values_t(c, h):
        return vt_ref[c, h * MLA_V:(h + 1) * MLA_V, :]

    _attention(nvis, MLA_HEADS, prep, logits, values_t, o_ref, scratch)


def _mla_call(qa, ka, vat, tq, ch):
    b, s, _ = qa.shape
    wv = MLA_HEADS * MLA_V
    return pl.pallas_call(
        functools.partial(_mla_kernel, tq=tq, ch=ch),
        grid=(b, s // tq),
        in_specs=[
            pl.BlockSpec((None, tq, 8 * LANES), lambda bi, i: (bi, i, 0)),
            pl.BlockSpec((None, s, 8 * LANES), lambda bi, i: (bi, 0, 0)),
            pl.BlockSpec((None, s // ch, wv, ch), lambda bi, i: (bi, 0, 0, 0)),
        ],
        out_specs=pl.BlockSpec((None, tq, wv), lambda bi, i: (bi, i, 0)),
        out_shape=jax.ShapeDtypeStruct((b, s, wv), BF16),
        scratch_shapes=_attn_scratch(MLA_HEADS, MLA_V, tq, ch),
        compiler_params=pltpu.CompilerParams(
            dimension_semantics=("parallel", "arbitrary"), vmem_limit_bytes=VMEM_LIMIT),
        name="mla",
    )(qa, ka, vat)


def _dsa_kernel(qb_ref, kb_ref, vbt_ref, qi_ref, ki_ref, wi_ref, kpos_ref, o_ref,
                qih_scr, wih_scr, key_scr, hi_scr, lo_scr, bias_scr, jsel_scr, tau_scr, cnt_scr,
                sred_scr,
                *scratch,
                tq, ch, topk, seq):
    i = pl.program_id(1)
    nvis = ((i + 1) * tq + ch - 1) // ch
    key_idx = lax.broadcasted_iota(jnp.int32, (ch, tq), 0)
    query = i * tq + lax.broadcasted_iota(jnp.int32, (ch, tq), 1)
    lane = lax.broadcasted_iota(jnp.int32, (tq, LANES), 1)

    qi = qi_ref[...]
    wi_t = wi_ref[...].T * IDX_SCALE
    for h in range(IDX_HEADS):
        group = qi[:, (h // 4) * LANES:(h // 4 + 1) * LANES]
        lo = IDX_DIM * (h % 4)
        keep = jnp.where(lane >= lo, jnp.where(lane < lo + IDX_DIM, 1.0, 0.0), 0.0).astype(BF16)
        qih_scr[h] = group * keep
        wih_scr[h] = jnp.broadcast_to(wi_t[h:h + 1, :], (SUBLANES, tq))

    def score_chunk(c):
        ks = pl.multiple_of(c * ch, ch)
        kic = ki_ref[pl.ds(ks, ch), :]
        sc = jnp.zeros((ch, tq), F32)
        for h in range(IDX_HEADS):
            w_h = _tall(wih_scr[h], ch // SUBLANES)
            sc = sc + w_h * jnp.maximum(_dot_nt(kic, qih_scr[h]), 0.0)
        sc = jnp.where(sc == 0.0, 0.0, sc)
        bits = pltpu.bitcast(sc, jnp.int32)
        key = bits ^ ((bits >> 31) & 0x7FFFFFFF)
        key = jnp.where(key_idx + ks <= query, key, INT_MIN)
        key_scr[c] = key
        hi_scr[c] = (key >> 16).astype(I16)
        lo_scr[c] = (((key ^ 0x8000) << 16) >> 16).astype(I16)

    def score_pair(pair, _):
        score_chunk(2 * pair)
        score_chunk(2 * pair + 1)
        return 0

    lax.fori_loop(0, nvis // 2, score_pair, 0)

    @pl.when(nvis % 2 == 1)
    def _():
        score_chunk(nvis - 1)

    def count(pred):
        def body(c, acc):
            return acc + _fold_rows(pred(key_scr[c], c), jnp.add)
        acc = lax.fori_loop(0, nvis, body, jnp.zeros((SUBLANES, tq), F32))
        return jnp.sum(acc, axis=0, keepdims=True)

    def search16(half_scr, nscan, target, cnt_all):
        slab = _slab_rows(I16)

        def count16(cand):
            cand16 = _tall(_tall(cand, slab // SUBLANES).astype(I16), ch // slab)
            acc = jnp.zeros((slab, tq), I16)
            for c in range(nscan):
                ones = jnp.where(half_scr[c] >= cand16, jnp.ones((), I16), jnp.zeros((), I16))
                acc = acc + _fold_rows(ones, jnp.add)
            return _sublane_allreduce(_fold_rows(acc.astype(F32), jnp.add), jnp.add, sred_scr)

        def body(_, carry):
            lo, step, cnt_lo, cnt_hi = carry
            cand = lo + step
            cnt = count16(cand)
            ok = cnt >= target
            return (jnp.where(ok, cand, lo), step >> 1,
                    jnp.where(ok, cnt, cnt_lo), jnp.where(ok, cnt_hi, cnt))

        out = lax.fori_loop(0, 16, body, (jnp.full((SUBLANES, tq), I16_MIN, jnp.int32),
                                          jnp.full((SUBLANES, tq), -I16_MIN, jnp.int32),
                                          cnt_all, jnp.zeros((SUBLANES, tq), F32)),
                            unroll=SEARCH_UNROLL)
        return out[0], out[2], out[3]

    tau_scr[...] = jnp.full((1, tq), INT_MIN, jnp.int32)
    cnt_scr[...] = jnp.zeros(cnt_scr.shape, F32)

    @pl.when(nvis % 2 == 1)
    def _():
        hi_scr[nvis] = jnp.full((ch, tq), I16_MIN, I16)
        lo_scr[nvis] = jnp.full((ch, tq), I16_MIN, I16)

    for pairs in range(1, key_scr.shape[0] // 2 + 1):
        @pl.when(jnp.logical_and((i + 1) * tq > topk, (nvis + 1) // 2 == pairs))
        def _(nscan=2 * pairs):
            all_keys = jnp.full((SUBLANES, tq), float(nscan * ch), F32)
            tau_hi, cnt_ge_hi, cnt_gt_hi = search16(hi_scr, nscan, float(topk), all_keys)
            slab = _slab_rows(I16)
            tau_hi16 = _tall(_tall(tau_hi, slab // SUBLANES).astype(I16), ch // slab)
            for c in range(nscan):
                lo_scr[c] = jnp.where(hi_scr[c] == tau_hi16, lo_scr[c], jnp.full((), I16_MIN, I16))
            tau_lo, cnt_ge_lo, cnt_gt_lo = search16(lo_scr, nscan, topk - cnt_gt_hi,
                                                    cnt_ge_hi - cnt_gt_hi)
            tau_scr[...] = ((tau_hi << 16) + (tau_lo - I16_MIN))[:1]
            cnt_scr[0] = (cnt_gt_hi + cnt_ge_lo)[:1]
            cnt_scr[1] = (cnt_gt_hi + cnt_gt_lo)[:1]

    tau = tau_scr[...]
    cnt_ge = cnt_scr[0]
    cnt_gt = cnt_scr[1]
    has_tau = tau > INT_MIN
    need = topk - cnt_gt
    jsel_scr[...] = jnp.where(has_tau, seq, -1)
    excess = jnp.where(has_tau, jnp.where(cnt_ge - cnt_gt > need, 1.0, 0.0), 0.0)

    @pl.when(jnp.max(excess) > 0.0)
    def _():
        def idx_body(it, lo):
            cand = lo + jnp.left_shift(jnp.int32(1), seq.bit_length() - 2 - it)
            cnt = count(lambda k, c: jnp.where(
                k == tau, jnp.where(key_idx + c * ch < cand, 1.0, 0.0), 0.0))
            return jnp.where(cnt < need, cand, lo)
        last = lax.fori_loop(0, seq.bit_length() - 1, idx_body, jnp.zeros((1, tq), jnp.int32))
        jsel_scr[...] = jnp.where(has_tau, last, -1)

    jsel = jsel_scr[...]

    def bias_body(c, _):
        k = key_scr[c]
        tie = jnp.where(key_idx + c * ch <= jsel, 0.0, NEG)
        bias_scr[c] = jnp.where(k > tau, 0.0, jnp.where(k == tau, tie, NEG))
        return 0

    lax.fori_loop(0, nvis, bias_body, 0)

    q_heads = []
    for h in range(DSA_HEADS):
        qpair = qb_ref[:, (h // 2) * LANES:(h // 2 + 1) * LANES]
        if h % 2 == 0:
            keep = jnp.where(lane < DSA_HEAD_DIM, 1.0, 0.0).astype(BF16)
        else:
            keep = jnp.where(lane >= DSA_HEAD_DIM, 1.0, 0.0).astype(BF16)
        terms = _bf16_terms(ALIBI_SLOPES[h] * LOG2E, POS_TERMS)
        q_pos = jnp.zeros((tq, LANES), F32)
        for j, term in enumerate(terms):
            q_pos = jnp.where(lane == j, POS_SPLIT * term, jnp.where(lane == POS_TERMS + j, term, q_pos))
        q_heads.append(jnp.concatenate([qpair * keep, q_pos.astype(BF16)], axis=1))

    def prep(c):
        ks = pl.multiple_of(c * ch, ch)
        return ks, bias_scr[c], kpos_ref[pl.ds(ks, ch), :]

    def logits(ctx, h):
        ks, sel, kpos = ctx
        psl = slice((h // 2) * LANES, (h // 2 + 1) * LANES)
        k_aug = jnp.concatenate([kb_ref[pl.ds(ks, ch), psl], kpos], axis=1)
        return _dot_nt(k_aug, q_heads[h]) + sel

    def values_t(c, h):
        return vbt_ref[c, h * DSA_HEAD_DIM:(h + 1) * DSA_HEAD_DIM, :]

    _attention(nvis, DSA_HEADS, prep, logits, values_t, o_ref, scratch)


def _dsa_call(qb, kb, vbt, qi, ki, wi, kpos, tq, ch):
    b, s, w = qb.shape
    nch = s // ch
    topk = min(TOPK_MAX, s // 4)
    full = lambda wdt: pl.BlockSpec((None, s, wdt), lambda bi, i: (bi, 0, 0))
    tile = lambda wdt: pl.BlockSpec((None, tq, wdt), lambda bi, i: (bi, i, 0))
    return pl.pallas_call(
        functools.partial(_dsa_kernel, tq=tq, ch=ch, topk=topk, seq=s),
        grid=(b, s // tq),
        in_specs=[tile(w), full(w), pl.BlockSpec((None, nch, w, ch), lambda bi, i: (bi, 0, 0, 0)),
                  tile(2 * LANES), full(LANES), tile(LANES), full(LANES)],
        out_specs=tile(w),
        out_shape=jax.ShapeDtypeStruct((b, s, w), BF16),
        scratch_shapes=[pltpu.VMEM((IDX_HEADS, tq, LANES), BF16), pltpu.VMEM((IDX_HEADS, SUBLANES, tq), F32),
                        pltpu.VMEM((nch, ch, tq), jnp.int32), pltpu.VMEM((nch, ch, tq), I16),
                        pltpu.VMEM((nch, ch, tq), I16), pltpu.VMEM((nch, ch, tq), F32),
                        pltpu.VMEM((1, tq), jnp.int32), pltpu.VMEM((1, tq), jnp.int32),
                        pltpu.VMEM((2, 1, tq), F32),
                        pltpu.VMEM((tq // LANES, SUBLANES, LANES), F32)]
        + _attn_scratch(DSA_HEADS, DSA_HEAD_DIM, tq, ch),
        compiler_params=pltpu.CompilerParams(
            dimension_semantics=("parallel", "arbitrary"), vmem_limit_bytes=VMEM_LIMIT),
        name="dsa",
    )(qb, kb, vbt, qi, ki, wi, kpos)


def _post_kernel(x_ref, oa_ref, ob_ref, gate_ref, wa_ref, wb_ref, wo_ref, g1_ref, b1_ref,
                 wfi_ref, wfd_ref, g2_ref, b2_ref, out_ref):
    ya = _dot(oa_ref[...], wa_ref[...])
    yb = _dot(ob_ref[...], wb_ref[...])
    mixed = (_sigmoid(gate_ref[:, :D_MODEL].astype(F32)) * ya
             + _sigmoid(gate_ref[:, D_MODEL:].astype(F32)) * yb)
    mix_out = _dot(mixed.astype(BF16), wo_ref[...])
    h1 = _layer_norm(ALPHA * x_ref[...] + mix_out, g1_ref[...], b1_ref[...])
    gu = _dot(h1.astype(BF16), wfi_ref[...])
    gate = gu[:, :D_FF]
    act = gate * _sigmoid(gate) * gu[:, D_FF:]
    f = _dot(act.astype(BF16), wfd_ref[...])
    out_ref[...] = _layer_norm(ALPHA * h1 + f, g2_ref[...], b2_ref[...])


def _post_call(x2, oa, ob, gates, weights, rows):
    n = x2.shape[0]
    row_spec = lambda wdt: pl.BlockSpec((rows, wdt), lambda i: (i, 0))
    return pl.pallas_call(
        _post_kernel,
        grid=(n // rows,),
        in_specs=[row_spec(D_MODEL), row_spec(oa.shape[1]), row_spec(ob.shape[1]),
                  row_spec(2 * D_MODEL)] + [_const_spec(a.shape) for a in weights],
        out_specs=row_spec(D_MODEL),
        out_shape=jax.ShapeDtypeStruct((n, D_MODEL), F32),
        compiler_params=pltpu.CompilerParams(
            dimension_semantics=("parallel",), vmem_limit_bytes=VMEM_LIMIT),
        name="post",
    )(x2, oa, ob, gates, *weights)


def _pad_cols(w, width):
    return jnp.pad(w, ((0, 0), (0, width - w.shape[1])))


def _prep_proj_weights(w_in, q_norm, kv_norm, w_uq, w_ukv):
    sizes = (MLA_Q_RANK, MLA_KV_RANK, MLA_ROPE, 512, 512, 512, IDX_HEADS * IDX_DIM, IDX_DIM,
             IDX_HEADS, D_MODEL, D_MODEL)
    parts, start = [], 0
    for n in sizes:
        parts.append(w_in[:, start:start + n])
        start += n
    w_cq, w_ckv, w_kr, w_qb, w_kb, w_vb, w_qi, w_ki, w_wi, w_ga, w_gb = parts
    half = MLA_ROPE // 2
    d = w_in.shape[0]

    def rope_group(first, second):
        return jnp.concatenate([jnp.zeros((d, MLA_NOPE), F32), first, second,
                                jnp.zeros((d, LANES - MLA_QK), F32)], axis=1)

    wc = jnp.concatenate([w_cq, w_ckv], axis=1)
    wkr = jnp.concatenate([rope_group(w_kr[:, :half], w_kr[:, half:]),
                           rope_group(w_kr[:, half:], w_kr[:, :half])], axis=1)
    wdsa = jnp.concatenate([w_qb * DSA_SCALE, w_kb], axis=1)
    widx = jnp.concatenate([w_qi, w_ki, w_ki, w_ki, w_ki, _pad_cols(w_wi, LANES)], axis=1)
    wgate = jnp.concatenate([w_ga, w_gb], axis=1)

    r = w_uq.shape[0]
    uq = w_uq.reshape(r, MLA_HEADS, MLA_QK)
    zq = lambda n: jnp.zeros((r, MLA_HEADS, n), F32)
    wuqm = jnp.concatenate([uq, zq(LANES - MLA_QK)], axis=2).reshape(r, MLA_HEADS * LANES)
    wuqs = jnp.concatenate([zq(MLA_NOPE), uq[:, :, MLA_NOPE + half:], uq[:, :, MLA_NOPE:MLA_NOPE + half],
                            zq(LANES - MLA_QK)], axis=2).reshape(r, MLA_HEADS * LANES)
    ukv = w_ukv.reshape(r, MLA_HEADS, MLA_NOPE + MLA_V)
    wuk = jnp.concatenate([ukv[:, :, :MLA_NOPE], zq(LANES - MLA_NOPE)], axis=2).reshape(r, MLA_HEADS * LANES)
    wuvt = ukv[:, :, MLA_NOPE:].reshape(r, MLA_HEADS * MLA_V).T

    bf = lambda a: a.astype(BF16)
    return (bf(wc), bf(wkr), bf(wdsa), bf(w_vb.T), bf(widx), bf(wgate), q_norm.reshape(1, -1),
            kv_norm.reshape(1, -1), bf(wuqm), bf(wuqs), bf(wuk), bf(wuvt))


def _rope_freq():
    half = MLA_ROPE // 2
    return (ROPE_BASE ** (-jnp.arange(half, dtype=F32) / half)).reshape(half, 1)


def kernel(x, positions, w_in, mla_q_norm, mla_kv_norm, mla_w_uq, mla_w_ukv, w_branch_a, w_branch_b,
           w_out, ln1_g, ln1_b, ffn_w_in, ffn_w_down, ln2_g, ln2_b):
    b, s, d = x.shape
    assert w_in.shape[0] == DEPTH and d == D_MODEL
    assert s & (s - 1) == 0 and s % ATT_TQ == 0 and s % (2 * ATT_CH) == 0 and PROJ_ROWS % ATT_CH == 0
    assert (b * s) % PROJ_ROWS == 0 and s <= POS_SPLIT * 256
    n = b * s
    h = x.reshape(n, d)
    pos2 = positions.reshape(n, 1)
    posr = positions.reshape(n // PROJ_ROWS, 1, PROJ_ROWS)
    freq = _rope_freq()
    for l in range(DEPTH):
        pw = _prep_proj_weights(w_in[l], mla_q_norm[l], mla_kv_norm[l], mla_w_uq[l], mla_w_ukv[l])
        qa, ka, vat, qb, kb, vbt, qi, ki, wi, kpos, gates = _proj_call(
            h, pos2, posr, freq, pw, PROJ_ROWS, ATT_CH)
        r3 = lambda a: a.reshape(b, s, a.shape[1])
        t4 = lambda a: a.reshape(b, s // ATT_CH, a.shape[1], ATT_CH)
        o_a = _mla_call(r3(qa), r3(ka), t4(vat), ATT_TQ, ATT_CH)
        o_b = _dsa_call(r3(qb), r3(kb), t4(vbt), r3(qi), r3(ki), r3(wi), r3(kpos), ATT_TQ, ATT_CH)
        row = lambda v: v.reshape(1, -1)
        post_w = (w_branch_a[l].astype(BF16), w_branch_b[l].astype(BF16), w_out[l].astype(BF16),
                  row(ln1_g[l]), row(ln1_b[l]), ffn_w_in[l].astype(BF16), ffn_w_down[l].astype(BF16),
                  row(ln2_g[l]), row(ln2_b[l]))
        h = _post_call(h, o_a.reshape(n, -1), o_b.reshape(n, -1), gates, post_w, POST_ROWS)
    return h.reshape(b, s, d)
```

```python
import functools
import math
import struct

import jax
import jax.numpy as jnp
from jax import lax
from jax.experimental import pallas as pl
from jax.experimental.pallas import tpu as pltpu

F32 = jnp.float32
BF16 = jnp.bfloat16

D_MODEL = 1024
MLA_HEADS = 8
MLA_Q_RANK = 256
MLA_KV_RANK = 256
MLA_NOPE = 64
MLA_ROPE = 32
MLA_V = 64
ROPE_BASE = 10000.0
DSA_HEADS = 8
DSA_HEAD_DIM = 64
IDX_HEADS = 8
IDX_DIM = 32
TOPK_MAX = 256
D_FF = 2816
DEPTH = 1
ALPHA = (2 * DEPTH) ** 0.25
LN_EPS = 1e-5
RMS_EPS = 1e-6

LANES = 128
SUBLANES = 8
MLA_QK = MLA_NOPE + MLA_ROPE
MLA_SCALE = 1.0 / math.sqrt(MLA_QK)
DSA_SCALE = 1.0 / math.sqrt(DSA_HEAD_DIM)
IDX_SCALE = 1.0 / math.sqrt(IDX_DIM * IDX_HEADS)
ALIBI_SLOPES = tuple(2.0 ** (-8.0 * (i + 1) / DSA_HEADS) for i in range(DSA_HEADS))
POS_SPLIT = 64
POS_TERMS = 3
LOG2E = math.log2(math.e)

NEG = -1e30
INT_MIN = -(2 ** 31)
I16 = jnp.int16
I16_MIN = -(2 ** 15)

PROJ_ROWS = 512
POST_ROWS = 512
ATT_TQ = 256
ATT_CH = 256
SEARCH_UNROLL = 2
VMEM_LIMIT = 56 * 1024 * 1024


def _const_spec(shape):
    zeros = (0,) * len(shape)
    return pl.BlockSpec(shape, lambda *_: zeros, pipeline_mode=pl.Buffered(1))


def _dot(a, b):
    return jnp.dot(a, b, preferred_element_type=F32)


def _dot_nt(a, b):
    return lax.dot_general(a, b, (((1,), (1,)), ((), ())), preferred_element_type=F32)


def _rms_norm(x, g):
    return x * lax.rsqrt(jnp.mean(x * x, axis=-1, keepdims=True) + RMS_EPS) * g


def _layer_norm(x, g, b):
    mu = jnp.mean(x, axis=-1, keepdims=True)
    xc = x - mu
    var = jnp.mean(xc * xc, axis=-1, keepdims=True)
    return xc * lax.rsqrt(var + LN_EPS) * g + b


def _sigmoid(x):
    return 1.0 / (1.0 + jnp.exp(-x))


def _bf16_terms(x, n):
    terms = []
    for _ in range(n):
        bits = struct.unpack("<I", struct.pack("<f", x))[0]
        bits = (bits + 0x7FFF + ((bits >> 16) & 1)) & 0xFFFF0000
        term = struct.unpack("<f", struct.pack("<I", bits))[0]
        terms.append(term)
        x -= term
    return terms


def _slab_rows(dtype):
    return SUBLANES * (4 // jnp.dtype(dtype).itemsize)


def _fold_rows(x, op, ways=4):
    slab = _slab_rows(x.dtype)
    parts = [x[j * slab:(j + 1) * slab] for j in range(x.shape[0] // slab)]
    accs = parts[:ways]
    for j in range(ways, len(parts)):
        accs[j % ways] = op(accs[j % ways], parts[j])
    while len(accs) > 1:
        accs = [op(accs[k], accs[k + 1]) for k in range(0, len(accs) - 1, 2)] + accs[len(accs) & ~1:]
    return accs[0]


def _tall(x, n):
    return x if n == 1 else jnp.concatenate([x] * n, axis=0)


def _sublane_allreduce(x, op, red_ref):
    outs = []
    for t in range(x.shape[1] // LANES):
        red_ref[t] = x[:, t * LANES:(t + 1) * LANES]
        rows = [jnp.broadcast_to(red_ref[t, r:r + 1, :], (SUBLANES, LANES)) for r in range(SUBLANES)]
        while len(rows) > 1:
            rows = [op(rows[k], rows[k + 1]) for k in range(0, len(rows), 2)]
        outs.append(rows[0])
    return outs[0] if len(outs) == 1 else jnp.concatenate(outs, axis=1)


def _attention(nvis, heads, prep, logits, values_t, o_ref, scratch):
    s_scr, cmax_scr, m_scr, acc_scr, red_scr = scratch
    dv = acc_scr.shape[1] - _slab_rows(BF16)
    ones = jnp.ones((_slab_rows(BF16), s_scr.shape[2]), BF16)
    m_scr[...] = jnp.full(m_scr.shape, NEG, F32)
    acc_scr[...] = jnp.zeros(acc_scr.shape, F32)

    def logits_part(c, slot):
        ctx = prep(jnp.asarray(c, jnp.int32))
        for h in range(heads):
            s_t = logits(ctx, h)
            s_scr[h, slot] = s_t
            cmax_scr[h, slot] = _fold_rows(s_t, jnp.maximum)

    def values_part(c, slot):
        for h in range(heads):
            m_prev = m_scr[h]
            m_new = jnp.maximum(
                m_prev, _sublane_allreduce(cmax_scr[h, slot], jnp.maximum, red_scr.at[h]))
            alpha = jnp.exp2(m_prev - m_new)
            p = jnp.exp2(s_scr[h, slot] - _tall(m_new, s_scr.shape[2] // SUBLANES))
            v_aug = jnp.concatenate([values_t(c, h), ones], axis=0)
            acc_scr[h] = (_tall(alpha, acc_scr.shape[1] // SUBLANES) * acc_scr[h]
                          + _dot(v_aug, p.astype(BF16)))
            m_scr[h] = m_new

    logits_part(0, 0)

    def step(c, slot):
        values_part(c, slot)
        logits_part(c + 1, 1 - slot)

    def body(pair, _):
        step(2 * pair, 0)
        step(2 * pair + 1, 1)
        return 0

    lax.fori_loop(0, (nvis - 1) // 2, body, 0)

    @pl.when(nvis % 2 == 0)
    def _():
        step(nvis - 2, 0)
        values_part(nvis - 1, 1)

    @pl.when(nvis % 2 == 1)
    def _():
        values_part(nvis - 1, 0)

    norm = lambda h: acc_scr[h, :dv] / acc_scr[h, dv:dv + 1]
    for pair in range(heads // 2):
        o_t = jnp.concatenate([norm(2 * pair), norm(2 * pair + 1)], axis=0)
        o_ref[:, pair * LANES:(pair + 1) * LANES] = o_t.T.astype(BF16)


def _attn_scratch(heads, dv, tq, ch):
    return [pltpu.VMEM((heads, 2, ch, tq), F32), pltpu.VMEM((heads, 2, SUBLANES, tq), F32),
            pltpu.VMEM((heads, SUBLANES, tq), F32),
            pltpu.VMEM((heads, dv + _slab_rows(BF16), tq), F32),
            pltpu.VMEM((heads, tq // LANES, SUBLANES, LANES), F32)]


def _proj_kernel(x_ref, pos_ref, posr_ref, freq_ref, wc_ref, wkr_ref, wdsa_ref, wvbt_ref, widx_ref,
                 wgate_ref, qn_ref, kvn_ref, wuqm_ref, wuqs_ref, wuk_ref, wuvt_ref,
                 qa_ref, ka_ref, vat_ref, qb_ref, kb_ref, vbt_ref, qi_ref, ki_ref, wi_ref,
                 kpos_ref, gate_ref, *, ch):
    rows = x_ref.shape[0]
    xb = x_ref[...].astype(BF16)
    pos = pos_ref[...]
    ang = freq_ref[...] * posr_ref[...].astype(F32)
    cos_c, sin_c = jnp.cos(ang), jnp.sin(ang)
    cos_t = jnp.concatenate([jnp.ones((MLA_NOPE, rows), F32), cos_c, cos_c,
                             jnp.ones((LANES - MLA_QK, rows), F32)], axis=0).T
    sin_t = jnp.concatenate([jnp.zeros((MLA_NOPE, rows), F32), -sin_c, sin_c,
                             jnp.zeros((LANES - MLA_QK, rows), F32)], axis=0).T

    c = _dot(xb, wc_ref[...])
    cq = _rms_norm(c[:, :MLA_Q_RANK], qn_ref[...]).astype(BF16)
    ckv = _rms_norm(c[:, MLA_Q_RANK:], kvn_ref[...]).astype(BF16)

    gate_ref[...] = _dot(xb, wgate_ref[...]).astype(BF16)

    d = _dot(xb, wdsa_ref[...])
    w = DSA_HEADS * DSA_HEAD_DIM
    qb_ref[...] = (d[:, :w] * LOG2E).astype(BF16)
    kb_ref[...] = d[:, w:].astype(BF16)

    va_t = _dot_nt(wuvt_ref[...], ckv).astype(BF16)
    vb_t = _dot_nt(wvbt_ref[...], xb).astype(BF16)
    for k in range(rows // ch):
        vat_ref[k] = va_t[:, k * ch:(k + 1) * ch]
        vbt_ref[k] = vb_t[:, k * ch:(k + 1) * ch]

    ix = _dot(xb, widx_ref[...])
    qi_ref[...] = ix[:, :2 * LANES].astype(BF16)
    ki_ref[...] = ix[:, 2 * LANES:3 * LANES].astype(BF16)
    wi_ref[...] = ix[:, 3 * LANES:]

    lane = lax.broadcasted_iota(jnp.int32, (rows, LANES), 1)
    hi = (pos >> (POS_SPLIT.bit_length() - 1)).astype(F32)
    lo = (pos & (POS_SPLIT - 1)).astype(F32)
    kpos_ref[...] = jnp.where(lane < POS_TERMS, hi,
                              jnp.where(lane < 2 * POS_TERMS, lo, 0.0)).astype(BF16)

    qm = _dot(cq, wuqm_ref[...])
    qs = _dot(cq, wuqs_ref[...])
    cos_q = cos_t * (MLA_SCALE * LOG2E)
    sin_q = sin_t * (MLA_SCALE * LOG2E)
    for h in range(MLA_HEADS):
        sl = slice(h * LANES, (h + 1) * LANES)
        qa_ref[:, sl] = (qm[:, sl] * cos_q + qs[:, sl] * sin_q).astype(BF16)

    kr = _dot(xb, wkr_ref[...])
    kr_rot = kr[:, :LANES] * cos_t + kr[:, LANES:] * sin_t
    kn = _dot(ckv, wuk_ref[...])
    for h in range(MLA_HEADS):
        sl = slice(h * LANES, (h + 1) * LANES)
        ka_ref[:, sl] = (kn[:, sl] + kr_rot).astype(BF16)


def _proj_call(x2, pos2, posr, freq, weights, rows, ch):
    n = x2.shape[0]
    (wc, wkr, wdsa, wvbt, widx, wgate, qn, kvn, wuqm, wuqs, wuk, wuvt) = weights
    row_spec = lambda wdt: pl.BlockSpec((rows, wdt), lambda i: (i, 0))
    row_out = lambda wdt, dt: (row_spec(wdt), jax.ShapeDtypeStruct((n, wdt), dt))
    t_out = lambda wdt: (pl.BlockSpec((rows // ch, wdt, ch), lambda i: (i, 0, 0)),
                         jax.ShapeDtypeStruct((n // ch, wdt, ch), BF16))
    outs = [row_out(8 * LANES, BF16), row_out(8 * LANES, BF16), t_out(MLA_HEADS * MLA_V),
            row_out(512, BF16), row_out(512, BF16), t_out(DSA_HEADS * DSA_HEAD_DIM),
            row_out(2 * LANES, BF16), row_out(LANES, BF16), row_out(LANES, F32),
            row_out(LANES, BF16), row_out(2 * D_MODEL, BF16)]
    consts = (freq, wc, wkr, wdsa, wvbt, widx, wgate, qn, kvn, wuqm, wuqs, wuk, wuvt)
    return pl.pallas_call(
        functools.partial(_proj_kernel, ch=ch),
        grid=(n // rows,),
        in_specs=[row_spec(D_MODEL), row_spec(1), pl.BlockSpec((None, 1, rows), lambda i: (i, 0, 0))]
        + [_const_spec(a.shape) for a in consts],
        out_specs=[o[0] for o in outs],
        out_shape=[o[1] for o in outs],
        compiler_params=pltpu.CompilerParams(
            dimension_semantics=("parallel",), vmem_limit_bytes=VMEM_LIMIT),
        name="proj",
    )(x2, pos2, posr, *consts)


def _mla_kernel(q_ref, k_ref, vt_ref, o_ref, *scratch, tq, ch):
    i = pl.program_id(1)
    nvis = ((i + 1) * tq + ch - 1) // ch
    key = lax.broadcasted_iota(jnp.int32, (ch, tq), 0)
    query = i * tq + lax.broadcasted_iota(jnp.int32, (ch, tq), 1)

    def prep(c):
        ks = pl.multiple_of(c * ch, ch)
        return ks, jnp.where(key + ks <= query, 0.0, NEG)

    def logits(ctx, h):
        ks, causal = ctx
        hsl = slice(h * LANES, (h + 1) * LANES)
        return _dot_nt(k_ref[pl.ds(ks, ch), hsl], q_ref[:, hsl]) + causal

    def values_t(c, h):
        return vt_ref[c, h * MLA_V:(h + 1) * MLA_V, :]

    _attention(nvis, MLA_HEADS, prep, logits, values_t, o_ref, scratch)


def _mla_call(qa, ka, vat, tq, ch):
    b, s, _ = qa.shape
    wv = MLA_HEADS * MLA_V
    return pl.pallas_call(
        functools.partial(_mla_kernel, tq=tq, ch=ch),
        grid=(b, s // tq),
        in_specs=[
            pl.BlockSpec((None, tq, 8 * LANES), lambda bi, i: (bi, i, 0)),
            pl.BlockSpec((None, s, 8 * LANES), lambda bi, i: (bi, 0, 0)),
            pl.BlockSpec((None, s // ch, wv, ch), lambda bi, i: (bi, 0, 0, 0)),
        ],
        out_specs=pl.BlockSpec((None, tq, wv), lambda bi, i: (bi, i, 0)),
        out_shape=jax.ShapeDtypeStruct((b, s, wv), BF16),
        scratch_shapes=_attn_scratch(MLA_HEADS, MLA_V, tq, ch),
        compiler_params=pltpu.CompilerParams(
            dimension_semantics=("parallel", "arbitrary"), vmem_limit_bytes=VMEM_LIMIT),
        name="mla",
    )(qa, ka, vat)


def _dsa_kernel(qb_ref, kb_ref, vbt_ref, qi_ref, ki_ref, wi_ref, kpos_ref, o_ref,
                qih_scr, wih_scr, key_scr, hi_scr, lo_scr, bias_scr, jsel_scr, tau_scr, sred_scr,
                *scratch,
                tq, ch, topk, seq):
    i = pl.program_id(1)
    nvis = ((i + 1) * tq + ch - 1) // ch
    key_idx = lax.broadcasted_iota(jnp.int32, (ch, tq), 0)
    query = i * tq + lax.broadcasted_iota(jnp.int32, (ch, tq), 1)
    lane = lax.broadcasted_iota(jnp.int32, (tq, LANES), 1)

    qi = qi_ref[...]
    wi_t = wi_ref[...].T * IDX_SCALE
    for h in range(IDX_HEADS):
        group = qi[:, (h // 4) * LANES:(h // 4 + 1) * LANES]
        lo = IDX_DIM * (h % 4)
        keep = jnp.where(lane >= lo, jnp.where(lane < lo + IDX_DIM, 1.0, 0.0), 0.0).astype(BF16)
        qih_scr[h] = group * keep
        wih_scr[h] = jnp.broadcast_to(wi_t[h:h + 1, :], (SUBLANES, tq))

    def score_chunk(c):
        ks = pl.multiple_of(c * ch, ch)
        kic = ki_ref[pl.ds(ks, ch), :]
        sc = jnp.zeros((ch, tq), F32)
        for h in range(IDX_HEADS):
            w_h = _tall(wih_scr[h], ch // SUBLANES)
            sc = sc + w_h * jnp.maximum(_dot_nt(kic, qih_scr[h]), 0.0)
        sc = jnp.where(sc == 0.0, 0.0, sc)
        bits = pltpu.bitcast(sc, jnp.int32)
        key = bits ^ ((bits >> 31) & 0x7FFFFFFF)
        key = jnp.where(key_idx + ks <= query, key, INT_MIN)
        key_scr[c] = key
        hi_scr[c] = (key >> 16).astype(I16)
        lo_scr[c] = (((key ^ 0x8000) << 16) >> 16).astype(I16)

    def score_pair(pair, _):
        score_chunk(2 * pair)
        score_chunk(2 * pair + 1)
        return 0

    lax.fori_loop(0, nvis // 2, score_pair, 0)

    @pl.when(nvis % 2 == 1)
    def _():
        score_chunk(nvis - 1)

    slab = _slab_rows(I16)

    def packed(x):
        return _tall(_tall(x, slab // SUBLANES).astype(I16), ch // slab)

    def column_total(acc):
        return _sublane_allreduce(_fold_rows(acc.astype(F32), jnp.add), jnp.add, sred_scr)

    def search16(half_scr, nscan, target, cnt_all):
        def count16(cand):
            cand16 = packed(cand)
            acc = jnp.zeros((slab, tq), I16)
            for c in range(nscan):
                ones = jnp.where(half_scr[c] >= cand16, jnp.ones((), I16), jnp.zeros((), I16))
                acc = acc + _fold_rows(ones, jnp.add)
            return column_total(acc)

        def body(_, carry):
            lo, step, cnt_lo, cnt_hi = carry
            cand = lo + step
            cnt = count16(cand)
            ok = cnt >= target
            return (jnp.where(ok, cand, lo), step >> 1,
                    jnp.where(ok, cnt, cnt_lo), jnp.where(ok, cnt_hi, cnt))

        out = lax.fori_loop(0, 16, body, (jnp.full((SUBLANES, tq), I16_MIN, jnp.int32),
                                          jnp.full((SUBLANES, tq), -I16_MIN, jnp.int32),
                                          cnt_all, jnp.zeros((SUBLANES, tq), F32)),
                            unroll=SEARCH_UNROLL)
        return out[0], out[2], out[3]

    tau_scr[...] = jnp.full((1, tq), INT_MIN, jnp.int32)
    jsel_scr[...] = jnp.full((1, tq), -1, jnp.int32)
    key_idx16 = key_idx.astype(I16)

    @pl.when(nvis % 2 == 1)
    def _():
        hi_scr[nvis] = jnp.full((ch, tq), I16_MIN, I16)
        lo_scr[nvis] = jnp.full((ch, tq), I16_MIN, I16)

    for pairs in range(1, key_scr.shape[0] // 2 + 1):
        @pl.when(jnp.logical_and((i + 1) * tq > topk, (nvis + 1) // 2 == pairs))
        def _(nscan=2 * pairs):
            all_keys = jnp.full((SUBLANES, tq), float(nscan * ch), F32)
            tau_hi, cnt_ge_hi, cnt_gt_hi = search16(hi_scr, nscan, float(topk), all_keys)
            tau_hi16 = packed(tau_hi)
            for c in range(nscan):
                lo_scr[c] = jnp.where(hi_scr[c] == tau_hi16, lo_scr[c], jnp.full((), I16_MIN, I16))
            tau_lo, cnt_ge_lo, cnt_gt_lo = search16(lo_scr, nscan, topk - cnt_gt_hi,
                                                    cnt_ge_hi - cnt_gt_hi)
            tau = (tau_hi << 16) + (tau_lo - I16_MIN)
            cnt_gt = cnt_gt_hi + cnt_gt_lo
            n_ties = cnt_ge_lo - cnt_gt_lo
            need = topk - cnt_gt
            has_tau = tau > INT_MIN
            tau_scr[...] = tau[:1]
            jsel_scr[...] = jnp.where(has_tau, seq, -1)[:1]
            excess = jnp.where(has_tau, jnp.where(n_ties > need, 1.0, 0.0), 0.0)

            @pl.when(jnp.max(excess) > 0.0)
            def _():
                tau_w = _tall(tau, ch // SUBLANES)
                for c in range(nscan):
                    flags = jnp.where(key_scr[c] == tau_w, 1, 0)
                    if c == nscan - 1:
                        flags = jnp.where(nvis % 2 == 1, 0, flags)
                    hi_scr[c] = flags.astype(I16)

                def idx_body(_, carry):
                    lo, step = carry
                    cand = lo + step
                    acc = jnp.zeros((slab, tq), I16)
                    for c in range(nscan):
                        before = key_idx16 < packed(cand - c * ch)
                        acc = acc + _fold_rows(jnp.where(before, hi_scr[c], jnp.zeros((), I16)), jnp.add)
                    return jnp.where(column_total(acc) < need, cand, lo), step >> 1

                last, _ = lax.fori_loop(
                    0, seq.bit_length() - 1, idx_body,
                    (jnp.zeros((SUBLANES, tq), jnp.int32), jnp.full((SUBLANES, tq), seq // 2, jnp.int32)))
                jsel_scr[...] = jnp.where(has_tau, last, -1)[:1]

    tau = tau_scr[...]
    jsel = jsel_scr[...]

    def bias_body(c, _):
        k = key_scr[c]
        tie = jnp.where(key_idx + c * ch <= jsel, 0.0, NEG)
        bias_scr[c] = jnp.where(k > tau, 0.0, jnp.where(k == tau, tie, NEG))
        return 0

    lax.fori_loop(0, nvis, bias_body, 0)

    q_heads = []
    for h in range(DSA_HEADS):
        qpair = qb_ref[:, (h // 2) * LANES:(h // 2 + 1) * LANES]
        if h % 2 == 0:
            keep = jnp.where(lane < DSA_HEAD_DIM, 1.0, 0.0).astype(BF16)
        else:
            keep = jnp.where(lane >= DSA_HEAD_DIM, 1.0, 0.0).astype(BF16)
        terms = _bf16_terms(ALIBI_SLOPES[h] * LOG2E, POS_TERMS)
        q_pos = jnp.zeros((tq, LANES), F32)
        for j, term in enumerate(terms):
            q_pos = jnp.where(lane == j, POS_SPLIT * term, jnp.where(lane == POS_TERMS + j, term, q_pos))
        q_heads.append(jnp.concatenate([qpair * keep, q_pos.astype(BF16)], axis=1))

    def prep(c):
        ks = pl.multiple_of(c * ch, ch)
        return ks, bias_scr[c], kpos_ref[pl.ds(ks, ch), :]

    def logits(ctx, h):
        ks, sel, kpos = ctx
        psl = slice((h // 2) * LANES, (h // 2 + 1) * LANES)
        k_aug = jnp.concatenate([kb_ref[pl.ds(ks, ch), psl], kpos], axis=1)
        return _dot_nt(k_aug, q_heads[h]) + sel

    def values_t(c, h):
        return vbt_ref[c, h * DSA_HEAD_DIM:(h + 1) * DSA_HEAD_DIM, :]

    _attention(nvis, DSA_HEADS, prep, logits, values_t, o_ref, scratch)


def _dsa_call(qb, kb, vbt, qi, ki, wi, kpos, tq, ch):
    b, s, w = qb.shape
    nch = s // ch
    topk = min(TOPK_MAX, s // 4)
    full = lambda wdt: pl.BlockSpec((None, s, wdt), lambda bi, i: (bi, 0, 0))
    tile = lambda wdt: pl.BlockSpec((None, tq, wdt), lambda bi, i: (bi, i, 0))
    return pl.pallas_call(
        functools.partial(_dsa_kernel, tq=tq, ch=ch, topk=topk, seq=s),
        grid=(b, s // tq),
        in_specs=[tile(w), full(w), pl.BlockSpec((None, nch, w, ch), lambda bi, i: (bi, 0, 0, 0)),
                  tile(2 * LANES), full(LANES), tile(LANES), full(LANES)],
        out_specs=tile(w),
        out_shape=jax.ShapeDtypeStruct((b, s, w), BF16),
        scratch_shapes=[pltpu.VMEM((IDX_HEADS, tq, LANES), BF16), pltpu.VMEM((IDX_HEADS, SUBLANES, tq), F32),
                        pltpu.VMEM((nch, ch, tq), jnp.int32), pltpu.VMEM((nch, ch, tq), I16),
                        pltpu.VMEM((nch, ch, tq), I16), pltpu.VMEM((nch, ch, tq), F32),
                        pltpu.VMEM((1, tq), jnp.int32), pltpu.VMEM((1, tq), jnp.int32),
                        pltpu.VMEM((tq // LANES, SUBLANES, LANES), F32)]
        + _attn_scratch(DSA_HEADS, DSA_HEAD_DIM, tq, ch),
        compiler_params=pltpu.CompilerParams(
            dimension_semantics=("parallel", "arbitrary"), vmem_limit_bytes=VMEM_LIMIT),
        name="dsa",
    )(qb, kb, vbt, qi, ki, wi, kpos)


def _post_kernel(x_ref, oa_ref, ob_ref, gate_ref, wa_ref, wb_ref, wo_ref, g1_ref, b1_ref,
                 wfi_ref, wfd_ref, g2_ref, b2_ref, out_ref):
    ya = _dot(oa_ref[...], wa_ref[...])
    yb = _dot(ob_ref[...], wb_ref[...])
    mixed = (_sigmoid(gate_ref[:, :D_MODEL].astype(F32)) * ya
             + _sigmoid(gate_ref[:, D_MODEL:].astype(F32)) * yb)
    mix_out = _dot(mixed.astype(BF16), wo_ref[...])
    h1 = _layer_norm(ALPHA * x_ref[...] + mix_out, g1_ref[...], b1_ref[...])
    gu = _dot(h1.astype(BF16), wfi_ref[...])
    gate = gu[:, :D_FF]
    act = gate * _sigmoid(gate) * gu[:, D_FF:]
    f = _dot(act.astype(BF16), wfd_ref[...])
    out_ref[...] = _layer_norm(ALPHA * h1 + f, g2_ref[...], b2_ref[...])


def _post_call(x2, oa, ob, gates, weights, rows):
    n = x2.shape[0]
    row_spec = lambda wdt: pl.BlockSpec((rows, wdt), lambda i: (i, 0))
    return pl.pallas_call(
        _post_kernel,
        grid=(n // rows,),
        in_specs=[row_spec(D_MODEL), row_spec(oa.shape[1]), row_spec(ob.shape[1]),
                  row_spec(2 * D_MODEL)] + [_const_spec(a.shape) for a in weights],
        out_specs=row_spec(D_MODEL),
        out_shape=jax.ShapeDtypeStruct((n, D_MODEL), F32),
        compiler_params=pltpu.CompilerParams(
            dimension_semantics=("parallel",), vmem_limit_bytes=VMEM_LIMIT),
        name="post",
    )(x2, oa, ob, gates, *weights)


def _pad_cols(w, width):
    return jnp.pad(w, ((0, 0), (0, width - w.shape[1])))


def _prep_proj_weights(w_in, q_norm, kv_norm, w_uq, w_ukv):
    sizes = (MLA_Q_RANK, MLA_KV_RANK, MLA_ROPE, 512, 512, 512, IDX_HEADS * IDX_DIM, IDX_DIM,
             IDX_HEADS, D_MODEL, D_MODEL)
    parts, start = [], 0
    for n in sizes:
        parts.append(w_in[:, start:start + n])
        start += n
    w_cq, w_ckv, w_kr, w_qb, w_kb, w_vb, w_qi, w_ki, w_wi, w_ga, w_gb = parts
    half = MLA_ROPE // 2
    d = w_in.shape[0]

    def rope_group(first, second):
        return jnp.concatenate([jnp.zeros((d, MLA_NOPE), F32), first, second,
                                jnp.zeros((d, LANES - MLA_QK), F32)], axis=1)

    wc = jnp.concatenate([w_cq, w_ckv], axis=1)
    wkr = jnp.concatenate([rope_group(w_kr[:, :half], w_kr[:, half:]),
                           rope_group(w_kr[:, half:], w_kr[:, :half])], axis=1)
    wdsa = jnp.concatenate([w_qb * DSA_SCALE, w_kb], axis=1)
    widx = jnp.concatenate([w_qi, w_ki, w_ki, w_ki, w_ki, _pad_cols(w_wi, LANES)], axis=1)
    wgate = jnp.concatenate([w_ga, w_gb], axis=1)

    r = w_uq.shape[0]
    uq = w_uq.reshape(r, MLA_HEADS, MLA_QK)
    zq = lambda n: jnp.zeros((r, MLA_HEADS, n), F32)
    wuqm = jnp.concatenate([uq, zq(LANES - MLA_QK)], axis=2).reshape(r, MLA_HEADS * LANES)
    wuqs = jnp.concatenate([zq(MLA_NOPE), uq[:, :, MLA_NOPE + half:], uq[:, :, MLA_NOPE:MLA_NOPE + half],
                            zq(LANES - MLA_QK)], axis=2).reshape(r, MLA_HEADS * LANES)
    ukv = w_ukv.reshape(r, MLA_HEADS, MLA_NOPE + MLA_V)
    wuk = jnp.concatenate([ukv[:, :, :MLA_NOPE], zq(LANES - MLA_NOPE)], axis=2).reshape(r, MLA_HEADS * LANES)
    wuvt = ukv[:, :, MLA_NOPE:].reshape(r, MLA_HEADS * MLA_V).T

    bf = lambda a: a.astype(BF16)
    return (bf(wc), bf(wkr), bf(wdsa), bf(w_vb.T), bf(widx), bf(wgate), q_norm.reshape(1, -1),
            kv_norm.reshape(1, -1), bf(wuqm), bf(wuqs), bf(wuk), bf(wuvt))


def _rope_freq():
    half = MLA_ROPE // 2
    return (ROPE_BASE ** (-jnp.arange(half, dtype=F32) / half)).reshape(half, 1)


def kernel(x, positions, w_in, mla_q_norm, mla_kv_norm, mla_w_uq, mla_w_ukv, w_branch_a, w_branch_b,
           w_out, ln1_g, ln1_b, ffn_w_in, ffn_w_down, ln2_g, ln2_b):
    b, s, d = x.shape
    assert w_in.shape[0] == DEPTH and d == D_MODEL
    assert s & (s - 1) == 0 and s % ATT_TQ == 0 and s % (2 * ATT_CH) == 0 and PROJ_ROWS % ATT_CH == 0
    assert (b * s) % PROJ_ROWS == 0 and s <= POS_SPLIT * 256
    n = b * s
    h = x.reshape(n, d)
    pos2 = positions.reshape(n, 1)
    posr = positions.reshape(n // PROJ_ROWS, 1, PROJ_ROWS)
    freq = _rope_freq()
    for l in range(DEPTH):
        pw = _prep_proj_weights(w_in[l], mla_q_norm[l], mla_kv_norm[l], mla_w_uq[l], mla_w_ukv[l])
        qa, ka, vat, qb, kb, vbt, qi, ki, wi, kpos, gates = _proj_call(
            h, pos2, posr, freq, pw, PROJ_ROWS, ATT_CH)
        r3 = lambda a: a.reshape(b, s, a.shape[1])
        t4 = lambda a: a.reshape(b, s // ATT_CH, a.shape[1], ATT_CH)
        o_a = _mla_call(r3(qa), r3(ka), t4(vat), ATT_TQ, ATT_CH)
        o_b = _dsa_call(r3(qb), r3(kb), t4(vbt), r3(qi), r3(ki), r3(wi), r3(kpos), ATT_TQ, ATT_CH)
        row = lambda v: v.reshape(1, -1)
        post_w = (w_branch_a[l].astype(BF16), w_branch_b[l].astype(BF16), w_out[l].astype(BF16),
                  row(ln1_g[l]), row(ln1_b[l]), ffn_w_in[l].astype(BF16), ffn_w_down[l].astype(BF16),
                  row(ln2_g[l]), row(ln2_b[l]))
        h = _post_call(h, o_a.reshape(n, -1), o_b.reshape(n, -1), gates, post_w, POST_ROWS)
    return h.reshape(b, s, d)
```

```python
import functools
import math
import struct

import jax
import jax.numpy as jnp
from jax import lax
from jax.experimental import pallas as pl
from jax.experimental.pallas import tpu as pltpu

F32 = jnp.float32
BF16 = jnp.bfloat16

D_MODEL = 1024
MLA_HEADS = 8
MLA_Q_RANK = 256
MLA_KV_RANK = 256
MLA_NOPE = 64
MLA_ROPE = 32
MLA_V = 64
ROPE_BASE = 10000.0
DSA_HEADS = 8
DSA_HEAD_DIM = 64
IDX_HEADS = 8
IDX_DIM = 32
TOPK_MAX = 256
D_FF = 2816
DEPTH = 1
ALPHA = (2 * DEPTH) ** 0.25
LN_EPS = 1e-5
RMS_EPS = 1e-6

LANES = 128
SUBLANES = 8
MLA_QK = MLA_NOPE + MLA_ROPE
MLA_SCALE = 1.0 / math.sqrt(MLA_QK)
DSA_SCALE = 1.0 / math.sqrt(DSA_HEAD_DIM)
IDX_SCALE = 1.0 / math.sqrt(IDX_DIM * IDX_HEADS)
ALIBI_SLOPES = tuple(2.0 ** (-8.0 * (i + 1) / DSA_HEADS) for i in range(DSA_HEADS))
POS_SPLIT = 64
POS_TERMS = 3
LOG2E = math.log2(math.e)

NEG = -1e30
INT_MIN = -(2 ** 31)
I16 = jnp.int16
I16_MIN = -(2 ** 15)

PROJ_ROWS = 512
POST_ROWS = 512
ATT_TQ = 256
ATT_CH = 256
SEARCH_UNROLL = 2
VMEM_LIMIT = 56 * 1024 * 1024


def _const_spec(shape):
    zeros = (0,) * len(shape)
    return pl.BlockSpec(shape, lambda *_: zeros, pipeline_mode=pl.Buffered(1))


def _dot(a, b):
    return jnp.dot(a, b, preferred_element_type=F32)


def _dot_nt(a, b):
    return lax.dot_general(a, b, (((1,), (1,)), ((), ())), preferred_element_type=F32)


def _rms_norm(x, g):
    return x * lax.rsqrt(jnp.mean(x * x, axis=-1, keepdims=True) + RMS_EPS) * g


def _layer_norm(x, g, b):
    mu = jnp.mean(x, axis=-1, keepdims=True)
    xc = x - mu
    var = jnp.mean(xc * xc, axis=-1, keepdims=True)
    return xc * lax.rsqrt(var + LN_EPS) * g + b


def _sigmoid(x):
    return 1.0 / (1.0 + jnp.exp(-x))


def _bf16_terms(x, n):
    terms = []
    for _ in range(n):
        bits = struct.unpack("<I", struct.pack("<f", x))[0]
        bits = (bits + 0x7FFF + ((bits >> 16) & 1)) & 0xFFFF0000
        term = struct.unpack("<f", struct.pack("<I", bits))[0]
        terms.append(term)
        x -= term
    return terms


def _slab_rows(dtype):
    return SUBLANES * (4 // jnp.dtype(dtype).itemsize)


def _fold_rows(x, op, ways=4):
    slab = _slab_rows(x.dtype)
    parts = [x[j * slab:(j + 1) * slab] for j in range(x.shape[0] // slab)]
    accs = parts[:ways]
    for j in range(ways, len(parts)):
        accs[j % ways] = op(accs[j % ways], parts[j])
    while len(accs) > 1:
        accs = [op(accs[k], accs[k + 1]) for k in range(0, len(accs) - 1, 2)] + accs[len(accs) & ~1:]
    return accs[0]


def _tall(x, n):
    return x if n == 1 else jnp.concatenate([x] * n, axis=0)


def _sublane_allreduce(x, op, red_ref):
    outs = []
    for t in range(x.shape[1] // LANES):
        red_ref[t] = x[:, t * LANES:(t + 1) * LANES]
        rows = [jnp.broadcast_to(red_ref[t, r:r + 1, :], (SUBLANES, LANES)) for r in range(SUBLANES)]
        while len(rows) > 1:
            rows = [op(rows[k], rows[k + 1]) for k in range(0, len(rows), 2)]
        outs.append(rows[0])
    return outs[0] if len(outs) == 1 else jnp.concatenate(outs, axis=1)


def _attention(nvis, heads, prep, logits, values_t, o_ref, scratch):
    s_scr, cmax_scr, m_scr, acc_scr, red_scr = scratch
    dv = acc_scr.shape[1] - _slab_rows(BF16)
    ones = jnp.ones((_slab_rows(BF16), s_scr.shape[2]), BF16)
    m_scr[...] = jnp.full(m_scr.shape, NEG, F32)
    acc_scr[...] = jnp.zeros(acc_scr.shape, F32)

    def logits_part(c, slot):
        ctx = prep(jnp.asarray(c, jnp.int32))
        for h in range(heads):
            s_t = logits(ctx, h)
            s_scr[h, slot] = s_t
            cmax_scr[h, slot] = _fold_rows(s_t, jnp.maximum)

    def values_part(c, slot):
        for h in range(heads):
            m_prev = m_scr[h]
            m_new = jnp.maximum(
                m_prev, _sublane_allreduce(cmax_scr[h, slot], jnp.maximum, red_scr.at[h]))
            alpha = jnp.exp2(m_prev - m_new)
            p = jnp.exp2(s_scr[h, slot] - _tall(m_new, s_scr.shape[2] // SUBLANES))
            v_aug = jnp.concatenate([values_t(c, h), ones], axis=0)
            acc_scr[h] = (_tall(alpha, acc_scr.shape[1] // SUBLANES) * acc_scr[h]
                          + _dot(v_aug, p.astype(BF16)))
            m_scr[h] = m_new

    logits_part(0, 0)

    def step(c, slot):
        values_part(c, slot)
        logits_part(c + 1, 1 - slot)

    def body(pair, _):
        step(2 * pair, 0)
        step(2 * pair + 1, 1)
        return 0

    lax.fori_loop(0, (nvis - 1) // 2, body, 0)

    @pl.when(nvis % 2 == 0)
    def _():
        step(nvis - 2, 0)
        values_part(nvis - 1, 1)

    @pl.when(nvis % 2 == 1)
    def _():
        values_part(nvis - 1, 0)

    norm = lambda h: acc_scr[h, :dv] / acc_scr[h, dv:dv + 1]
    for pair in range(heads // 2):
        o_t = jnp.concatenate([norm(2 * pair), norm(2 * pair + 1)], axis=0)
        o_ref[:, pair * LANES:(pair + 1) * LANES] = o_t.T.astype(BF16)


def _attn_scratch(heads, dv, tq, ch):
    return [pltpu.VMEM((heads, 2, ch, tq), F32), pltpu.VMEM((heads, 2, SUBLANES, tq), F32),
            pltpu.VMEM((heads, SUBLANES, tq), F32),
            pltpu.VMEM((heads, dv + _slab_rows(BF16), tq), F32),
            pltpu.VMEM((heads, tq // LANES, SUBLANES, LANES), F32)]


def _proj_kernel(x_ref, pos_ref, posr_ref, freq_ref, wc_ref, wkr_ref, wdsa_ref, wvbt_ref, widx_ref,
                 wgate_ref, qn_ref, kvn_ref, wuqm_ref, wuqs_ref, wuk_ref, wuvt_ref,
                 qa_ref, ka_ref, vat_ref, qb_ref, kb_ref, vbt_ref, qi_ref, ki_ref, wi_ref,
                 kpos_ref, gate_ref, *, ch):
    rows = x_ref.shape[0]
    xb = x_ref[...].astype(BF16)
    pos = pos_ref[...]
    ang = freq_ref[...] * posr_ref[...].astype(F32)
    cos_c, sin_c = jnp.cos(ang), jnp.sin(ang)
    cos_t = jnp.concatenate([jnp.ones((MLA_NOPE, rows), F32), cos_c, cos_c,
                             jnp.ones((LANES - MLA_QK, rows), F32)], axis=0).T
    sin_t = jnp.concatenate([jnp.zeros((MLA_NOPE, rows), F32), -sin_c, sin_c,
                             jnp.zeros((LANES - MLA_QK, rows), F32)], axis=0).T

    c = _dot(xb, wc_ref[...])
    cq = _rms_norm(c[:, :MLA_Q_RANK], qn_ref[...]).astype(BF16)
    ckv = _rms_norm(c[:, MLA_Q_RANK:], kvn_ref[...]).astype(BF16)

    gate_ref[...] = _dot(xb, wgate_ref[...]).astype(BF16)

    d = _dot(xb, wdsa_ref[...])
    w = DSA_HEADS * DSA_HEAD_DIM
    qb_ref[...] = (d[:, :w] * LOG2E).astype(BF16)
    kb_ref[...] = d[:, w:].astype(BF16)

    va_t = _dot_nt(wuvt_ref[...], ckv).astype(BF16)
    vb_t = _dot_nt(wvbt_ref[...], xb).astype(BF16)
    for k in range(rows // ch):
        vat_ref[k] = va_t[:, k * ch:(k + 1) * ch]
        vbt_ref[k] = vb_t[:, k * ch:(k + 1) * ch]

    ix = _dot(xb, widx_ref[...])
    qi_ref[...] = ix[:, :2 * LANES].astype(BF16)
    ki_ref[...] = ix[:, 2 * LANES:3 * LANES].astype(BF16)
    wi_ref[...] = ix[:, 3 * LANES:]

    lane = lax.broadcasted_iota(jnp.int32, (rows, LANES), 1)
    hi = (pos >> (POS_SPLIT.bit_length() - 1)).astype(F32)
    lo = (pos & (POS_SPLIT - 1)).astype(F32)
    kpos_ref[...] = jnp.where(lane < POS_TERMS, hi,
                              jnp.where(lane < 2 * POS_TERMS, lo, 0.0)).astype(BF16)

    qm = _dot(cq, wuqm_ref[...])
    qs = _dot(cq, wuqs_ref[...])
    cos_q = cos_t * (MLA_SCALE * LOG2E)
    sin_q = sin_t * (MLA_SCALE * LOG2E)
    for h in range(MLA_HEADS):
        sl = slice(h * LANES, (h + 1) * LANES)
        qa_ref[:, sl] = (qm[:, sl] * cos_q + qs[:, sl] * sin_q).astype(BF16)

    kr = _dot(xb, wkr_ref[...])
    kr_rot = kr[:, :LANES] * cos_t + kr[:, LANES:] * sin_t
    kn = _dot(ckv, wuk_ref[...])
    for h in range(MLA_HEADS):
        sl = slice(h * LANES, (h + 1) * LANES)
        ka_ref[:, sl] = (kn[:, sl] + kr_rot).astype(BF16)


def _proj_call(x2, pos2, posr, freq, weights, rows, ch):
    n = x2.shape[0]
    (wc, wkr, wdsa, wvbt, widx, wgate, qn, kvn, wuqm, wuqs, wuk, wuvt) = weights
    row_spec = lambda wdt: pl.BlockSpec((rows, wdt), lambda i: (i, 0))
    row_out = lambda wdt, dt: (row_spec(wdt), jax.ShapeDtypeStruct((n, wdt), dt))
    t_out = lambda wdt: (pl.BlockSpec((rows // ch, wdt, ch), lambda i: (i, 0, 0)),
                         jax.ShapeDtypeStruct((n // ch, wdt, ch), BF16))
    outs = [row_out(8 * LANES, BF16), row_out(8 * LANES, BF16), t_out(MLA_HEADS * MLA_V),
            row_out(512, BF16), row_out(512, BF16), t_out(DSA_HEADS * DSA_HEAD_DIM),
            row_out(2 * LANES, BF16), row_out(LANES, BF16), row_out(LANES, F32),
            row_out(LANES, BF16), row_out(2 * D_MODEL, BF16)]
    consts = (freq, wc, wkr, wdsa, wvbt, widx, wgate, qn, kvn, wuqm, wuqs, wuk, wuvt)
    return pl.pallas_call(
        functools.partial(_proj_kernel, ch=ch),
        grid=(n // rows,),
        in_specs=[row_spec(D_MODEL), row_spec(1), pl.BlockSpec((None, 1, rows), lambda i: (i, 0, 0))]
        + [_const_spec(a.shape) for a in consts],
        out_specs=[o[0] for o in outs],
        out_shape=[o[1] for o in outs],
        compiler_params=pltpu.CompilerParams(
            dimension_semantics=("parallel",), vmem_limit_bytes=VMEM_LIMIT),
        name="proj",
    )(x2, pos2, posr, *consts)


def _mla_kernel(q_ref, k_ref, vt_ref, o_ref, *scratch, tq, ch):
    i = pl.program_id(1)
    nvis = ((i + 1) * tq + ch - 1) // ch
    key = lax.broadcasted_iota(jnp.int32, (ch, tq), 0)
    query = i * tq + lax.broadcasted_iota(jnp.int32, (ch, tq), 1)

    def prep(c):
        ks = pl.multiple_of(c * ch, ch)
        return ks, jnp.where(key + ks <= query, 0.0, NEG)

    def logits(ctx, h):
        ks, causal = ctx
        hsl = slice(h * LANES, (h + 1) * LANES)
        return _dot_nt(k_ref[pl.ds(ks, ch), hsl], q_ref[:, hsl]) + causal

    def values_t(c, h):
        return vt_ref[c, h * MLA_V:(h + 1) * MLA_V, :]

    _attention(nvis, MLA_HEADS, prep, logits, values_t, o_ref, scratch)


def _mla_call(qa, ka, vat, tq, ch):
    b, s, _ = qa.shape
    wv = MLA_HEADS * MLA_V
    return pl.pallas_call(
        functools.partial(_mla_kernel, tq=tq, ch=ch),
        grid=(b, s // tq),
        in_specs=[
            pl.BlockSpec((None, tq, 8 * LANES), lambda bi, i: (bi, i, 0)),
            pl.BlockSpec((None, s, 8 * LANES), lambda bi, i: (bi, 0, 0)),
            pl.BlockSpec((None, s // ch, wv, ch), lambda bi, i: (bi, 0, 0, 0)),
        ],
        out_specs=pl.BlockSpec((None, tq, wv), lambda bi, i: (bi, i, 0)),
        out_shape=jax.ShapeDtypeStruct((b, s, wv), BF16),
        scratch_shapes=_attn_scratch(MLA_HEADS, MLA_V, tq, ch),
        compiler_params=pltpu.CompilerParams(
            dimension_semantics=("parallel", "arbitrary"), vmem_limit_bytes=VMEM_LIMIT),
        name="mla",
    )(qa, ka, vat)


def _dsa_kernel(qb_ref, kb_ref, vbt_ref, qi_ref, ki_ref, wi_ref, kpos_ref, o_ref,
                qih_scr, wih_scr, key_scr, hi_scr, lo_scr, jsel_scr, tau_scr, sred_scr,
                *scratch,
                tq, ch, topk, seq):
    i = pl.program_id(1)
    nvis = ((i + 1) * tq + ch - 1) // ch
    key_idx = lax.broadcasted_iota(jnp.int32, (ch, tq), 0)
    query = i * tq + lax.broadcasted_iota(jnp.int32, (ch, tq), 1)
    lane = lax.broadcasted_iota(jnp.int32, (tq, LANES), 1)

    qi = qi_ref[...]
    wi_t = wi_ref[...].T * IDX_SCALE
    for h in range(IDX_HEADS):
        group = qi[:, (h // 4) * LANES:(h // 4 + 1) * LANES]
        lo = IDX_DIM * (h % 4)
        keep = jnp.where(lane >= lo, jnp.where(lane < lo + IDX_DIM, 1.0, 0.0), 0.0).astype(BF16)
        qih_scr[h] = group * keep
        wih_scr[h] = jnp.broadcast_to(wi_t[h:h + 1, :], (SUBLANES, tq))

    def score_chunk(c):
        ks = pl.multiple_of(c * ch, ch)
        kic = ki_ref[pl.ds(ks, ch), :]
        sc = jnp.zeros((ch, tq), F32)
        for h in range(IDX_HEADS):
            w_h = _tall(wih_scr[h], ch // SUBLANES)
            sc = sc + w_h * jnp.maximum(_dot_nt(kic, qih_scr[h]), 0.0)
        sc = jnp.where(sc == 0.0, 0.0, sc)
        bits = pltpu.bitcast(sc, jnp.int32)
        key = bits ^ ((bits >> 31) & 0x7FFFFFFF)
        key = jnp.where(key_idx + ks <= query, key, INT_MIN)
        key_scr[c] = key
        hi_scr[c] = (key >> 16).astype(I16)
        lo_scr[c] = (((key ^ 0x8000) << 16) >> 16).astype(I16)

    def score_pair(pair, _):
        score_chunk(2 * pair)
        score_chunk(2 * pair + 1)
        return 0

    lax.fori_loop(0, nvis // 2, score_pair, 0)

    @pl.when(nvis % 2 == 1)
    def _():
        score_chunk(nvis - 1)

    slab = _slab_rows(I16)

    def packed(x):
        return _tall(_tall(x, slab // SUBLANES).astype(I16), ch // slab)

    def column_total(acc):
        return _sublane_allreduce(_fold_rows(acc.astype(F32), jnp.add), jnp.add, sred_scr)

    def search16(half_scr, nscan, target, cnt_all):
        def count16(cand):
            cand16 = packed(cand)
            acc = jnp.zeros((slab, tq), I16)
            for c in range(nscan):
                ones = jnp.where(half_scr[c] >= cand16, jnp.ones((), I16), jnp.zeros((), I16))
                acc = acc + _fold_rows(ones, jnp.add)
            return column_total(acc)

        def body(_, carry):
            lo, step, cnt_lo, cnt_hi = carry
            cand = lo + step
            cnt = count16(cand)
            ok = cnt >= target
            return (jnp.where(ok, cand, lo), step >> 1,
                    jnp.where(ok, cnt, cnt_lo), jnp.where(ok, cnt_hi, cnt))

        out = lax.fori_loop(0, 16, body, (jnp.full((SUBLANES, tq), I16_MIN, jnp.int32),
                                          jnp.full((SUBLANES, tq), -I16_MIN, jnp.int32),
                                          cnt_all, jnp.zeros((SUBLANES, tq), F32)),
                            unroll=SEARCH_UNROLL)
        return out[0], out[2], out[3]

    tau_scr[...] = jnp.full((1, tq), INT_MIN, jnp.int32)
    jsel_scr[...] = jnp.full((1, tq), -1, jnp.int32)
    key_idx16 = key_idx.astype(I16)

    @pl.when(nvis % 2 == 1)
    def _():
        hi_scr[nvis] = jnp.full((ch, tq), I16_MIN, I16)
        lo_scr[nvis] = jnp.full((ch, tq), I16_MIN, I16)

    for pairs in range(1, key_scr.shape[0] // 2 + 1):
        @pl.when(jnp.logical_and((i + 1) * tq > topk, (nvis + 1) // 2 == pairs))
        def _(nscan=2 * pairs):
            all_keys = jnp.full((SUBLANES, tq), float(nscan * ch), F32)
            tau_hi, cnt_ge_hi, cnt_gt_hi = search16(hi_scr, nscan, float(topk), all_keys)
            tau_hi16 = packed(tau_hi)
            for c in range(nscan):
                lo_scr[c] = jnp.where(hi_scr[c] == tau_hi16, lo_scr[c], jnp.full((), I16_MIN, I16))
            tau_lo, cnt_ge_lo, cnt_gt_lo = search16(lo_scr, nscan, topk - cnt_gt_hi,
                                                    cnt_ge_hi - cnt_gt_hi)
            tau = (tau_hi << 16) + (tau_lo - I16_MIN)
            cnt_gt = cnt_gt_hi + cnt_gt_lo
            n_ties = cnt_ge_lo - cnt_gt_lo
            need = topk - cnt_gt
            has_tau = tau > INT_MIN
            tau_scr[...] = tau[:1]
            jsel_scr[...] = jnp.where(has_tau, seq, -1)[:1]
            excess = jnp.where(has_tau, jnp.where(n_ties > need, 1.0, 0.0), 0.0)

            @pl.when(jnp.max(excess) > 0.0)
            def _():
                tau_w = _tall(tau, ch // SUBLANES)
                for c in range(nscan):
                    flags = jnp.where(key_scr[c] == tau_w, 1, 0)
                    if c == nscan - 1:
                        flags = jnp.where(nvis % 2 == 1, 0, flags)
                    hi_scr[c] = flags.astype(I16)

                def idx_body(_, carry):
                    lo, step = carry
                    cand = lo + step
                    acc = jnp.zeros((slab, tq), I16)
                    for c in range(nscan):
                        before = key_idx16 < packed(cand - c * ch)
                        acc = acc + _fold_rows(jnp.where(before, hi_scr[c], jnp.zeros((), I16)), jnp.add)
                    return jnp.where(column_total(acc) < need, cand, lo), step >> 1

                last, _ = lax.fori_loop(
                    0, seq.bit_length() - 1, idx_body,
                    (jnp.zeros((SUBLANES, tq), jnp.int32), jnp.full((SUBLANES, tq), seq // 2, jnp.int32)))
                jsel_scr[...] = jnp.where(has_tau, last, -1)[:1]

    tau_w = _tall(jnp.broadcast_to(tau_scr[...], (SUBLANES, tq)), ch // SUBLANES)
    jsel_w = _tall(jnp.broadcast_to(jsel_scr[...], (SUBLANES, tq)), ch // SUBLANES)

    def selection_bias(c):
        k = key_scr[c]
        tie = jnp.where(key_idx + c * ch <= jsel_w, 0.0, NEG)
        return jnp.where(k > tau_w, 0.0, jnp.where(k == tau_w, tie, NEG))

    q_heads = []
    for h in range(DSA_HEADS):
        qpair = qb_ref[:, (h // 2) * LANES:(h // 2 + 1) * LANES]
        if h % 2 == 0:
            keep = jnp.where(lane < DSA_HEAD_DIM, 1.0, 0.0).astype(BF16)
        else:
            keep = jnp.where(lane >= DSA_HEAD_DIM, 1.0, 0.0).astype(BF16)
        terms = _bf16_terms(ALIBI_SLOPES[h] * LOG2E, POS_TERMS)
        q_pos = jnp.zeros((tq, LANES), F32)
        for j, term in enumerate(terms):
            q_pos = jnp.where(lane == j, POS_SPLIT * term, jnp.where(lane == POS_TERMS + j, term, q_pos))
        q_heads.append(jnp.concatenate([qpair * keep, q_pos.astype(BF16)], axis=1))

    def prep(c):
        ks = pl.multiple_of(c * ch, ch)
        return ks, selection_bias(c), kpos_ref[pl.ds(ks, ch), :]

    def logits(ctx, h):
        ks, sel, kpos = ctx
        psl = slice((h // 2) * LANES, (h // 2 + 1) * LANES)
        k_aug = jnp.concatenate([kb_ref[pl.ds(ks, ch), psl], kpos], axis=1)
        return _dot_nt(k_aug, q_heads[h]) + sel

    def values_t(c, h):
        return vbt_ref[c, h * DSA_HEAD_DIM:(h + 1) * DSA_HEAD_DIM, :]

    _attention(nvis, DSA_HEADS, prep, logits, values_t, o_ref, scratch)


def _dsa_call(qb, kb, vbt, qi, ki, wi, kpos, tq, ch):
    b, s, w = qb.shape
    nch = s // ch
    topk = min(TOPK_MAX, s // 4)
    full = lambda wdt: pl.BlockSpec((None, s, wdt), lambda bi, i: (bi, 0, 0))
    tile = lambda wdt: pl.BlockSpec((None, tq, wdt), lambda bi, i: (bi, i, 0))
    return pl.pallas_call(
        functools.partial(_dsa_kernel, tq=tq, ch=ch, topk=topk, seq=s),
        grid=(b, s // tq),
        in_specs=[tile(w), full(w), pl.BlockSpec((None, nch, w, ch), lambda bi, i: (bi, 0, 0, 0)),
                  tile(2 * LANES), full(LANES), tile(LANES), full(LANES)],
        out_specs=tile(w),
        out_shape=jax.ShapeDtypeStruct((b, s, w), BF16),
        scratch_shapes=[pltpu.VMEM((IDX_HEADS, tq, LANES), BF16), pltpu.VMEM((IDX_HEADS, SUBLANES, tq), F32),
                        pltpu.VMEM((nch, ch, tq), jnp.int32), pltpu.VMEM((nch, ch, tq), I16),
                        pltpu.VMEM((nch, ch, tq), I16),
                        pltpu.VMEM((1, tq), jnp.int32), pltpu.VMEM((1, tq), jnp.int32),
                        pltpu.VMEM((tq // LANES, SUBLANES, LANES), F32)]
        + _attn_scratch(DSA_HEADS, DSA_HEAD_DIM, tq, ch),
        compiler_params=pltpu.CompilerParams(
            dimension_semantics=("parallel", "arbitrary"), vmem_limit_bytes=VMEM_LIMIT),
        name="dsa",
    )(qb, kb, vbt, qi, ki, wi, kpos)


def _post_kernel(x_ref, oa_ref, ob_ref, gate_ref, wa_ref, wb_ref, wo_ref, g1_ref, b1_ref,
                 wfi_ref, wfd_ref, g2_ref, b2_ref, out_ref):
    ya = _dot(oa_ref[...], wa_ref[...])
    yb = _dot(ob_ref[...], wb_ref[...])
    mixed = (_sigmoid(gate_ref[:, :D_MODEL].astype(F32)) * ya
             + _sigmoid(gate_ref[:, D_MODEL:].astype(F32)) * yb)
    mix_out = _dot(mixed.astype(BF16), wo_ref[...])
    h1 = _layer_norm(ALPHA * x_ref[...] + mix_out, g1_ref[...], b1_ref[...])
    gu = _dot(h1.astype(BF16), wfi_ref[...])
    gate = gu[:, :D_FF]
    act = gate * _sigmoid(gate) * gu[:, D_FF:]
    f = _dot(act.astype(BF16), wfd_ref[...])
    out_ref[...] = _layer_norm(ALPHA * h1 + f, g2_ref[...], b2_ref[...])


def _post_call(x2, oa, ob, gates, weights, rows):
    n = x2.shape[0]
    row_spec = lambda wdt: pl.BlockSpec((rows, wdt), lambda i: (i, 0))
    return pl.pallas_call(
        _post_kernel,
        grid=(n // rows,),
        in_specs=[row_spec(D_MODEL), row_spec(oa.shape[1]), row_spec(ob.shape[1]),
                  row_spec(2 * D_MODEL)] + [_const_spec(a.shape) for a in weights],
        out_specs=row_spec(D_MODEL),
        out_shape=jax.ShapeDtypeStruct((n, D_MODEL), F32),
        compiler_params=pltpu.CompilerParams(
            dimension_semantics=("parallel",), vmem_limit_bytes=VMEM_LIMIT),
        name="post",
    )(x2, oa, ob, gates, *weights)


def _pad_cols(w, width):
    return jnp.pad(w, ((0, 0), (0, width - w.shape[1])))


def _prep_proj_weights(w_in, q_norm, kv_norm, w_uq, w_ukv):
    sizes = (MLA_Q_RANK, MLA_KV_RANK, MLA_ROPE, 512, 512, 512, IDX_HEADS * IDX_DIM, IDX_DIM,
             IDX_HEADS, D_MODEL, D_MODEL)
    parts, start = [], 0
    for n in sizes:
        parts.append(w_in[:, start:start + n])
        start += n
    w_cq, w_ckv, w_kr, w_qb, w_kb, w_vb, w_qi, w_ki, w_wi, w_ga, w_gb = parts
    half = MLA_ROPE // 2
    d = w_in.shape[0]

    def rope_group(first, second):
        return jnp.concatenate([jnp.zeros((d, MLA_NOPE), F32), first, second,
                                jnp.zeros((d, LANES - MLA_QK), F32)], axis=1)

    wc = jnp.concatenate([w_cq, w_ckv], axis=1)
    wkr = jnp.concatenate([rope_group(w_kr[:, :half], w_kr[:, half:]),
                           rope_group(w_kr[:, half:], w_kr[:, :half])], axis=1)
    wdsa = jnp.concatenate([w_qb * DSA_SCALE, w_kb], axis=1)
    widx = jnp.concatenate([w_qi, w_ki, w_ki, w_ki, w_ki, _pad_cols(w_wi, LANES)], axis=1)
    wgate = jnp.concatenate([w_ga, w_gb], axis=1)

    r = w_uq.shape[0]
    uq = w_uq.reshape(r, MLA_HEADS, MLA_QK)
    zq = lambda n: jnp.zeros((r, MLA_HEADS, n), F32)
    wuqm = jnp.concatenate([uq, zq(LANES - MLA_QK)], axis=2).reshape(r, MLA_HEADS * LANES)
    wuqs = jnp.concatenate([zq(MLA_NOPE), uq[:, :, MLA_NOPE + half:], uq[:, :, MLA_NOPE:MLA_NOPE + half],
                            zq(LANES - MLA_QK)], axis=2).reshape(r, MLA_HEADS * LANES)
    ukv = w_ukv.reshape(r, MLA_HEADS, MLA_NOPE + MLA_V)
    wuk = jnp.concatenate([ukv[:, :, :MLA_NOPE], zq(LANES - MLA_NOPE)], axis=2).reshape(r, MLA_HEADS * LANES)
    wuvt = ukv[:, :, MLA_NOPE:].reshape(r, MLA_HEADS * MLA_V).T

    bf = lambda a: a.astype(BF16)
    return (bf(wc), bf(wkr), bf(wdsa), bf(w_vb.T), bf(widx), bf(wgate), q_norm.reshape(1, -1),
            kv_norm.reshape(1, -1), bf(wuqm), bf(wuqs), bf(wuk), bf(wuvt))


def _rope_freq():
    half = MLA_ROPE // 2
    return (ROPE_BASE ** (-jnp.arange(half, dtype=F32) / half)).reshape(half, 1)


def kernel(x, positions, w_in, mla_q_norm, mla_kv_norm, mla_w_uq, mla_w_ukv, w_branch_a, w_branch_b,
           w_out, ln1_g, ln1_b, ffn_w_in, ffn_w_down, ln2_g, ln2_b):
    b, s, d = x.shape
    assert w_in.shape[0] == DEPTH and d == D_MODEL
    assert s & (s - 1) == 0 and s % ATT_TQ == 0 and s % (2 * ATT_CH) == 0 and PROJ_ROWS % ATT_CH == 0
    assert (b * s) % PROJ_ROWS == 0 and s <= POS_SPLIT * 256
    n = b * s
    h = x.reshape(n, d)
    pos2 = positions.reshape(n, 1)
    posr = positions.reshape(n // PROJ_ROWS, 1, PROJ_ROWS)
    freq = _rope_freq()
    for l in range(DEPTH):
        pw = _prep_proj_weights(w_in[l], mla_q_norm[l], mla_kv_norm[l], mla_w_uq[l], mla_w_ukv[l])
        qa, ka, vat, qb, kb, vbt, qi, ki, wi, kpos, gates = _proj_call(
            h, pos2, posr, freq, pw, PROJ_ROWS, ATT_CH)
        r3 = lambda a: a.reshape(b, s, a.shape[1])
        t4 = lambda a: a.reshape(b, s // ATT_CH, a.shape[1], ATT_CH)
        o_a = _mla_call(r3(qa), r3(ka), t4(vat), ATT_TQ, ATT_CH)
        o_b = _dsa_call(r3(qb), r3(kb), t4(vbt), r3(qi), r3(ki), r3(wi), r3(kpos), ATT_TQ, ATT_CH)
        row = lambda v: v.reshape(1, -1)
        post_w = (w_branch_a[l].astype(BF16), w_branch_b[l].astype(BF16), w_out[l].astype(BF16),
                  row(ln1_g[l]), row(ln1_b[l]), ffn_w_in[l].astype(BF16), ffn_w_down[l].astype(BF16),
                  row(ln2_g[l]), row(ln2_b[l]))
        h = _post_call(h, o_a.reshape(n, -1), o_b.reshape(n, -1), gates, post_w, POST_ROWS)
    return h.reshape(b, s, d)
```

```python
import functools
import math
import struct

import jax
import jax.numpy as jnp
from jax import lax
from jax.experimental import pallas as pl
from jax.experimental.pallas import tpu as pltpu

F32 = jnp.float32
BF16 = jnp.bfloat16

D_MODEL = 1024
MLA_HEADS = 8
MLA_Q_RANK = 256
MLA_KV_RANK = 256
MLA_NOPE = 64
MLA_ROPE = 32
MLA_V = 64
ROPE_BASE = 10000.0
DSA_HEADS = 8
DSA_HEAD_DIM = 64
IDX_HEADS = 8
IDX_DIM = 32
TOPK_MAX = 256
D_FF = 2816
DEPTH = 1
ALPHA = (2 * DEPTH) ** 0.25
LN_EPS = 1e-5
RMS_EPS = 1e-6

LANES = 128
SUBLANES = 8
MLA_QK = MLA_NOPE + MLA_ROPE
MLA_SCALE = 1.0 / math.sqrt(MLA_QK)
DSA_SCALE = 1.0 / math.sqrt(DSA_HEAD_DIM)
IDX_SCALE = 1.0 / math.sqrt(IDX_DIM * IDX_HEADS)
ALIBI_SLOPES = tuple(2.0 ** (-8.0 * (i + 1) / DSA_HEADS) for i in range(DSA_HEADS))
POS_SPLIT = 64
POS_TERMS = 3
LOG2E = math.log2(math.e)

NEG = -1e30
INT_MIN = -(2 ** 31)
I16 = jnp.int16
I16_MIN = -(2 ** 15)

PROJ_ROWS = 512
POST_ROWS = 512
ATT_TQ = 256
ATT_CH = 256
SEARCH_UNROLL = 2
VMEM_LIMIT = 56 * 1024 * 1024


def _const_spec(shape):
    zeros = (0,) * len(shape)
    return pl.BlockSpec(shape, lambda *_: zeros, pipeline_mode=pl.Buffered(1))


def _dot(a, b):
    return jnp.dot(a, b, preferred_element_type=F32)


def _dot_nt(a, b):
    return lax.dot_general(a, b, (((1,), (1,)), ((), ())), preferred_element_type=F32)


def _rms_norm(x, g):
    return x * lax.rsqrt(jnp.mean(x * x, axis=-1, keepdims=True) + RMS_EPS) * g


def _layer_norm(x, g, b):
    mu = jnp.mean(x, axis=-1, keepdims=True)
    xc = x - mu
    var = jnp.mean(xc * xc, axis=-1, keepdims=True)
    return xc * lax.rsqrt(var + LN_EPS) * g + b


def _sigmoid(x):
    return 1.0 / (1.0 + jnp.exp(-x))


def _bf16_terms(x, n):
    terms = []
    for _ in range(n):
        bits = struct.unpack("<I", struct.pack("<f", x))[0]
        bits = (bits + 0x7FFF + ((bits >> 16) & 1)) & 0xFFFF0000
        term = struct.unpack("<f", struct.pack("<I", bits))[0]
        terms.append(term)
        x -= term
    return terms


def _slab_rows(dtype):
    return SUBLANES * (4 // jnp.dtype(dtype).itemsize)


def _fold_rows(x, op, ways=4):
    slab = _slab_rows(x.dtype)
    parts = [x[j * slab:(j + 1) * slab] for j in range(x.shape[0] // slab)]
    accs = parts[:ways]
    for j in range(ways, len(parts)):
        accs[j % ways] = op(accs[j % ways], parts[j])
    while len(accs) > 1:
        accs = [op(accs[k], accs[k + 1]) for k in range(0, len(accs) - 1, 2)] + accs[len(accs) & ~1:]
    return accs[0]


def _tall(x, n):
    return x if n == 1 else jnp.concatenate([x] * n, axis=0)


def _sublane_allreduce(x, op, red_ref):
    outs = []
    for t in range(x.shape[1] // LANES):
        red_ref[t] = x[:, t * LANES:(t + 1) * LANES]
        rows = [jnp.broadcast_to(red_ref[t, r:r + 1, :], (SUBLANES, LANES)) for r in range(SUBLANES)]
        while len(rows) > 1:
            rows = [op(rows[k], rows[k + 1]) for k in range(0, len(rows), 2)]
        outs.append(rows[0])
    return outs[0] if len(outs) == 1 else jnp.concatenate(outs, axis=1)


def _attention(nvis, heads, prep, logits, values_t, o_ref, scratch):
    s_scr, cmax_scr, m_scr, acc_scr, red_scr = scratch
    dv = acc_scr.shape[1] - _slab_rows(BF16)
    ones = jnp.ones((_slab_rows(BF16), s_scr.shape[2]), BF16)
    m_scr[...] = jnp.full(m_scr.shape, NEG, F32)
    acc_scr[...] = jnp.zeros(acc_scr.shape, F32)

    def logits_part(c, slot, last=False):
        ctx = prep(jnp.asarray(c, jnp.int32), last)
        for h in range(heads):
            s_t = logits(ctx, h)
            s_scr[h, slot] = s_t
            cmax_scr[h, slot] = _fold_rows(s_t, jnp.maximum)

    def values_part(c, slot):
        for h in range(heads):
            m_prev = m_scr[h]
            m_new = jnp.maximum(
                m_prev, _sublane_allreduce(cmax_scr[h, slot], jnp.maximum, red_scr.at[h]))
            alpha = jnp.exp2(m_prev - m_new)
            p = jnp.exp2(s_scr[h, slot] - _tall(m_new, s_scr.shape[2] // SUBLANES))
            v_aug = jnp.concatenate([values_t(c, h), ones], axis=0)
            acc_scr[h] = (_tall(alpha, acc_scr.shape[1] // SUBLANES) * acc_scr[h]
                          + _dot(v_aug, p.astype(BF16)))
            m_scr[h] = m_new

    logits_part(0, 0, last=True)

    def step(c, slot, last=False):
        values_part(c, slot)
        logits_part(c + 1, 1 - slot, last)

    def body(pair, _):
        step(2 * pair, 0)
        step(2 * pair + 1, 1)
        return 0

    lax.fori_loop(0, (nvis - 2) // 2, body, 0)

    @pl.when(nvis % 2 == 0)
    def _():
        step(nvis - 2, 0, last=True)
        values_part(nvis - 1, 1)

    @pl.when(jnp.logical_and(nvis % 2 == 1, nvis > 1))
    def _():
        step(nvis - 3, 0)
        step(nvis - 2, 1, last=True)
        values_part(nvis - 1, 0)

    @pl.when(nvis == 1)
    def _():
        values_part(0, 0)

    norm = lambda h: acc_scr[h, :dv] / acc_scr[h, dv:dv + 1]
    for pair in range(heads // 2):
        o_t = jnp.concatenate([norm(2 * pair), norm(2 * pair + 1)], axis=0)
        o_ref[:, pair * LANES:(pair + 1) * LANES] = o_t.T.astype(BF16)


def _attn_scratch(heads, dv, tq, ch):
    return [pltpu.VMEM((heads, 2, ch, tq), F32), pltpu.VMEM((heads, 2, SUBLANES, tq), F32),
            pltpu.VMEM((heads, SUBLANES, tq), F32),
            pltpu.VMEM((heads, dv + _slab_rows(BF16), tq), F32),
            pltpu.VMEM((heads, tq // LANES, SUBLANES, LANES), F32)]


def _proj_kernel(x_ref, pos_ref, posr_ref, freq_ref, wc_ref, wkr_ref, wdsa_ref, wvbt_ref, widx_ref,
                 wgate_ref, qn_ref, kvn_ref, wuqm_ref, wuqs_ref, wuk_ref, wuvt_ref,
                 qa_ref, ka_ref, vat_ref, qb_ref, kb_ref, vbt_ref, qi_ref, ki_ref, wi_ref,
                 kpos_ref, gate_ref, *, ch):
    rows = x_ref.shape[0]
    xb = x_ref[...].astype(BF16)
    pos = pos_ref[...]
    ang = freq_ref[...] * posr_ref[...].astype(F32)
    cos_c, sin_c = jnp.cos(ang), jnp.sin(ang)
    cos_t = jnp.concatenate([jnp.ones((MLA_NOPE, rows), F32), cos_c, cos_c,
                             jnp.ones((LANES - MLA_QK, rows), F32)], axis=0).T
    sin_t = jnp.concatenate([jnp.zeros((MLA_NOPE, rows), F32), -sin_c, sin_c,
                             jnp.zeros((LANES - MLA_QK, rows), F32)], axis=0).T

    c = _dot(xb, wc_ref[...])
    cq = _rms_norm(c[:, :MLA_Q_RANK], qn_ref[...]).astype(BF16)
    ckv = _rms_norm(c[:, MLA_Q_RANK:], kvn_ref[...]).astype(BF16)

    gate_ref[...] = _dot(xb, wgate_ref[...]).astype(BF16)

    d = _dot(xb, wdsa_ref[...])
    w = DSA_HEADS * DSA_HEAD_DIM
    qb_ref[...] = (d[:, :w] * LOG2E).astype(BF16)
    kb_ref[...] = d[:, w:].astype(BF16)

    va_t = _dot_nt(wuvt_ref[...], ckv).astype(BF16)
    vb_t = _dot_nt(wvbt_ref[...], xb).astype(BF16)
    for k in range(rows // ch):
        vat_ref[k] = va_t[:, k * ch:(k + 1) * ch]
        vbt_ref[k] = vb_t[:, k * ch:(k + 1) * ch]

    ix = _dot(xb, widx_ref[...])
    qi_ref[...] = ix[:, :2 * LANES].astype(BF16)
    ki_ref[...] = ix[:, 2 * LANES:3 * LANES].astype(BF16)
    wi_ref[...] = ix[:, 3 * LANES:]

    lane = lax.broadcasted_iota(jnp.int32, (rows, LANES), 1)
    hi = (pos >> (POS_SPLIT.bit_length() - 1)).astype(F32)
    lo = (pos & (POS_SPLIT - 1)).astype(F32)
    kpos_ref[...] = jnp.where(lane < POS_TERMS, hi,
                              jnp.where(lane < 2 * POS_TERMS, lo, 0.0)).astype(BF16)

    qm = _dot(cq, wuqm_ref[...])
    qs = _dot(cq, wuqs_ref[...])
    cos_q = cos_t * (MLA_SCALE * LOG2E)
    sin_q = sin_t * (MLA_SCALE * LOG2E)
    for h in range(MLA_HEADS):
        sl = slice(h * LANES, (h + 1) * LANES)
        qa_ref[:, sl] = (qm[:, sl] * cos_q + qs[:, sl] * sin_q).astype(BF16)

    kr = _dot(xb, wkr_ref[...])
    kr_rot = kr[:, :LANES] * cos_t + kr[:, LANES:] * sin_t
    kn = _dot(ckv, wuk_ref[...])
    for h in range(MLA_HEADS):
        sl = slice(h * LANES, (h + 1) * LANES)
        ka_ref[:, sl] = (kn[:, sl] + kr_rot).astype(BF16)


def _proj_call(x2, pos2, posr, freq, weights, rows, ch):
    n = x2.shape[0]
    (wc, wkr, wdsa, wvbt, widx, wgate, qn, kvn, wuqm, wuqs, wuk, wuvt) = weights
    row_spec = lambda wdt: pl.BlockSpec((rows, wdt), lambda i: (i, 0))
    row_out = lambda wdt, dt: (row_spec(wdt), jax.ShapeDtypeStruct((n, wdt), dt))
    t_out = lambda wdt: (pl.BlockSpec((rows // ch, wdt, ch), lambda i: (i, 0, 0)),
                         jax.ShapeDtypeStruct((n // ch, wdt, ch), BF16))
    outs = [row_out(8 * LANES, BF16), row_out(8 * LANES, BF16), t_out(MLA_HEADS * MLA_V),
            row_out(512, BF16), row_out(512, BF16), t_out(DSA_HEADS * DSA_HEAD_DIM),
            row_out(2 * LANES, BF16), row_out(LANES, BF16), row_out(LANES, F32),
            row_out(LANES, BF16), row_out(2 * D_MODEL, BF16)]
    consts = (freq, wc, wkr, wdsa, wvbt, widx, wgate, qn, kvn, wuqm, wuqs, wuk, wuvt)
    return pl.pallas_call(
        functools.partial(_proj_kernel, ch=ch),
        grid=(n // rows,),
        in_specs=[row_spec(D_MODEL), row_spec(1), pl.BlockSpec((None, 1, rows), lambda i: (i, 0, 0))]
        + [_const_spec(a.shape) for a in consts],
        out_specs=[o[0] for o in outs],
        out_shape=[o[1] for o in outs],
        compiler_params=pltpu.CompilerParams(
            dimension_semantics=("parallel",), vmem_limit_bytes=VMEM_LIMIT),
        name="proj",
    )(x2, pos2, posr, *consts)


def _mla_kernel(q_ref, k_ref, vt_ref, o_ref, *scratch, tq, ch):
    i = pl.program_id(1)
    nvis = ((i + 1) * tq + ch - 1) // ch
    key = lax.broadcasted_iota(jnp.int32, (ch, tq), 0)
    query = i * tq + lax.broadcasted_iota(jnp.int32, (ch, tq), 1)

    def prep(c, last):
        ks = pl.multiple_of(c * ch, ch)
        return ks, (jnp.where(key + ks <= query, 0.0, NEG) if last else None)

    def logits(ctx, h):
        ks, causal = ctx
        hsl = slice(h * LANES, (h + 1) * LANES)
        s_t = _dot_nt(k_ref[pl.ds(ks, ch), hsl], q_ref[:, hsl])
        return s_t if causal is None else s_t + causal

    def values_t(c, h):
        return vt_ref[c, h * MLA_V:(h + 1) * MLA_V, :]

    _attention(nvis, MLA_HEADS, prep, logits, values_t, o_ref, scratch)


def _mla_call(qa, ka, vat, tq, ch):
    b, s, _ = qa.shape
    assert tq == ch
    wv = MLA_HEADS * MLA_V
    return pl.pallas_call(
        functools.partial(_mla_kernel, tq=tq, ch=ch),
        grid=(b, s // tq),
        in_specs=[
            pl.BlockSpec((None, tq, 8 * LANES), lambda bi, i: (bi, i, 0)),
            pl.BlockSpec((None, s, 8 * LANES), lambda bi, i: (bi, 0, 0)),
            pl.BlockSpec((None, s // ch, wv, ch), lambda bi, i: (bi, 0, 0, 0)),
        ],
        out_specs=pl.BlockSpec((None, tq, wv), lambda bi, i: (bi, i, 0)),
        out_shape=jax.ShapeDtypeStruct((b, s, wv), BF16),
        scratch_shapes=_attn_scratch(MLA_HEADS, MLA_V, tq, ch),
        compiler_params=pltpu.CompilerParams(
            dimension_semantics=("parallel", "arbitrary"), vmem_limit_bytes=VMEM_LIMIT),
        name="mla",
    )(qa, ka, vat)


def _dsa_kernel(qb_ref, kb_ref, vbt_ref, qi_ref, ki_ref, wi_ref, kpos_ref, o_ref,
                qih_scr, wih_scr, key_scr, hi_scr, lo_scr, jsel_scr, tau_scr, sred_scr,
                *scratch,
                tq, ch, topk, seq):
    i = pl.program_id(1)
    nvis = ((i + 1) * tq + ch - 1) // ch
    key_idx = lax.broadcasted_iota(jnp.int32, (ch, tq), 0)
    query = i * tq + lax.broadcasted_iota(jnp.int32, (ch, tq), 1)
    lane = lax.broadcasted_iota(jnp.int32, (tq, LANES), 1)

    qi = qi_ref[...]
    wi_t = wi_ref[...].T * IDX_SCALE
    for h in range(IDX_HEADS):
        group = qi[:, (h // 4) * LANES:(h // 4 + 1) * LANES]
        lo = IDX_DIM * (h % 4)
        keep = jnp.where(lane >= lo, jnp.where(lane < lo + IDX_DIM, 1.0, 0.0), 0.0).astype(BF16)
        qih_scr[h] = group * keep
        wih_scr[h] = jnp.broadcast_to(wi_t[h:h + 1, :], (SUBLANES, tq))

    def score_chunk(c):
        ks = pl.multiple_of(c * ch, ch)
        kic = ki_ref[pl.ds(ks, ch), :]
        sc = jnp.zeros((ch, tq), F32)
        for h in range(IDX_HEADS):
            w_h = _tall(wih_scr[h], ch // SUBLANES)
            sc = sc + w_h * jnp.maximum(_dot_nt(kic, qih_scr[h]), 0.0)
        sc = jnp.where(sc == 0.0, 0.0, sc)
        bits = pltpu.bitcast(sc, jnp.int32)
        key = bits ^ ((bits >> 31) & 0x7FFFFFFF)
        key = jnp.where(key_idx + ks <= query, key, INT_MIN)
        key_scr[c] = key
        hi_scr[c] = (key >> 16).astype(I16)
        lo_scr[c] = (((key ^ 0x8000) << 16) >> 16).astype(I16)

    def score_pair(pair, _):
        score_chunk(2 * pair)
        score_chunk(2 * pair + 1)
        return 0

    lax.fori_loop(0, nvis // 2, score_pair, 0)

    @pl.when(nvis % 2 == 1)
    def _():
        score_chunk(nvis - 1)

    slab = _slab_rows(I16)

    def packed(x):
        return _tall(_tall(x, slab // SUBLANES).astype(I16), ch // slab)

    def column_total(acc):
        return _sublane_allreduce(_fold_rows(acc.astype(F32), jnp.add), jnp.add, sred_scr)

    def search16(half_scr, nscan, target, cnt_all):
        def count16(cand):
            cand16 = packed(cand)
            acc = jnp.zeros((slab, tq), I16)
            for c in range(nscan):
                ones = jnp.where(half_scr[c] >= cand16, jnp.ones((), I16), jnp.zeros((), I16))
                acc = acc + _fold_rows(ones, jnp.add)
            return column_total(acc)

        def body(_, carry):
            lo, step, cnt_lo, cnt_hi = carry
            cand = lo + step
            cnt = count16(cand)
            ok = cnt >= target
            return (jnp.where(ok, cand, lo), step >> 1,
                    jnp.where(ok, cnt, cnt_lo), jnp.where(ok, cnt_hi, cnt))

        out = lax.fori_loop(0, 16, body, (jnp.full((SUBLANES, tq), I16_MIN, jnp.int32),
                                          jnp.full((SUBLANES, tq), -I16_MIN, jnp.int32),
                                          cnt_all, jnp.zeros((SUBLANES, tq), F32)),
                            unroll=SEARCH_UNROLL)
        return out[0], out[2], out[3]

    tau_scr[...] = jnp.full((1, tq), INT_MIN, jnp.int32)
    jsel_scr[...] = jnp.full((1, tq), -1, jnp.int32)
    key_idx16 = key_idx.astype(I16)

    @pl.when(nvis % 2 == 1)
    def _():
        hi_scr[nvis] = jnp.full((ch, tq), I16_MIN, I16)
        lo_scr[nvis] = jnp.full((ch, tq), I16_MIN, I16)

    for pairs in range(1, key_scr.shape[0] // 2 + 1):
        @pl.when(jnp.logical_and((i + 1) * tq > topk, (nvis + 1) // 2 == pairs))
        def _(nscan=2 * pairs):
            all_keys = jnp.full((SUBLANES, tq), float(nscan * ch), F32)
            tau_hi, cnt_ge_hi, cnt_gt_hi = search16(hi_scr, nscan, float(topk), all_keys)
            tau_hi16 = packed(tau_hi)
            for c in range(nscan):
                lo_scr[c] = jnp.where(hi_scr[c] == tau_hi16, lo_scr[c], jnp.full((), I16_MIN, I16))
            tau_lo, cnt_ge_lo, cnt_gt_lo = search16(lo_scr, nscan, topk - cnt_gt_hi,
                                                    cnt_ge_hi - cnt_gt_hi)
            tau = (tau_hi << 16) + (tau_lo - I16_MIN)
            cnt_gt = cnt_gt_hi + cnt_gt_lo
            n_ties = cnt_ge_lo - cnt_gt_lo
            need = topk - cnt_gt
            has_tau = tau > INT_MIN
            tau_scr[...] = tau[:1]
            jsel_scr[...] = jnp.where(has_tau, seq, -1)[:1]
            excess = jnp.where(has_tau, jnp.where(n_ties > need, 1.0, 0.0), 0.0)

            @pl.when(jnp.max(excess) > 0.0)
            def _():
                tau_w = _tall(tau, ch // SUBLANES)
                for c in range(nscan):
                    flags = jnp.where(key_scr[c] == tau_w, 1, 0)
                    if c == nscan - 1:
                        flags = jnp.where(nvis % 2 == 1, 0, flags)
                    hi_scr[c] = flags.astype(I16)

                def idx_body(_, carry):
                    lo, step = carry
                    cand = lo + step
                    acc = jnp.zeros((slab, tq), I16)
                    for c in range(nscan):
                        before = key_idx16 < packed(cand - c * ch)
                        acc = acc + _fold_rows(jnp.where(before, hi_scr[c], jnp.zeros((), I16)), jnp.add)
                    return jnp.where(column_total(acc) < need, cand, lo), step >> 1

                last, _ = lax.fori_loop(
                    0, seq.bit_length() - 1, idx_body,
                    (jnp.zeros((SUBLANES, tq), jnp.int32), jnp.full((SUBLANES, tq), seq // 2, jnp.int32)))
                jsel_scr[...] = jnp.where(has_tau, last, -1)[:1]

    tau_w = _tall(jnp.broadcast_to(tau_scr[...], (SUBLANES, tq)), ch // SUBLANES)
    jsel_w = _tall(jnp.broadcast_to(jsel_scr[...], (SUBLANES, tq)), ch // SUBLANES)

    def selection_bias(c):
        k = key_scr[c]
        tie = jnp.where(key_idx + c * ch <= jsel_w, 0.0, NEG)
        return jnp.where(k > tau_w, 0.0, jnp.where(k == tau_w, tie, NEG))

    q_heads = []
    for h in range(DSA_HEADS):
        qpair = qb_ref[:, (h // 2) * LANES:(h // 2 + 1) * LANES]
        if h % 2 == 0:
            keep = jnp.where(lane < DSA_HEAD_DIM, 1.0, 0.0).astype(BF16)
        else:
            keep = jnp.where(lane >= DSA_HEAD_DIM, 1.0, 0.0).astype(BF16)
        terms = _bf16_terms(ALIBI_SLOPES[h] * LOG2E, POS_TERMS)
        q_pos = jnp.zeros((tq, LANES), F32)
        for j, term in enumerate(terms):
            q_pos = jnp.where(lane == j, POS_SPLIT * term, jnp.where(lane == POS_TERMS + j, term, q_pos))
        q_heads.append(jnp.concatenate([qpair * keep, q_pos.astype(BF16)], axis=1))

    def prep(c, last):
        ks = pl.multiple_of(c * ch, ch)
        return ks, selection_bias(c), kpos_ref[pl.ds(ks, ch), :]

    def logits(ctx, h):
        ks, sel, kpos = ctx
        psl = slice((h // 2) * LANES, (h // 2 + 1) * LANES)
        k_aug = jnp.concatenate([kb_ref[pl.ds(ks, ch), psl], kpos], axis=1)
        return _dot_nt(k_aug, q_heads[h]) + sel

    def values_t(c, h):
        return vbt_ref[c, h * DSA_HEAD_DIM:(h + 1) * DSA_HEAD_DIM, :]

    _attention(nvis, DSA_HEADS, prep, logits, values_t, o_ref, scratch)


def _dsa_call(qb, kb, vbt, qi, ki, wi, kpos, tq, ch):
    b, s, w = qb.shape
    nch = s // ch
    topk = min(TOPK_MAX, s // 4)
    full = lambda wdt: pl.BlockSpec((None, s, wdt), lambda bi, i: (bi, 0, 0))
    tile = lambda wdt: pl.BlockSpec((None, tq, wdt), lambda bi, i: (bi, i, 0))
    return pl.pallas_call(
        functools.partial(_dsa_kernel, tq=tq, ch=ch, topk=topk, seq=s),
        grid=(b, s // tq),
        in_specs=[tile(w), full(w), pl.BlockSpec((None, nch, w, ch), lambda bi, i: (bi, 0, 0, 0)),
                  tile(2 * LANES), full(LANES), tile(LANES), full(LANES)],
        out_specs=tile(w),
        out_shape=jax.ShapeDtypeStruct((b, s, w), BF16),
        scratch_shapes=[pltpu.VMEM((IDX_HEADS, tq, LANES), BF16), pltpu.VMEM((IDX_HEADS, SUBLANES, tq), F32),
                        pltpu.VMEM((nch, ch, tq), jnp.int32), pltpu.VMEM((nch, ch, tq), I16),
                        pltpu.VMEM((nch, ch, tq), I16),
                        pltpu.VMEM((1, tq), jnp.int32), pltpu.VMEM((1, tq), jnp.int32),
                        pltpu.VMEM((tq // LANES, SUBLANES, LANES), F32)]
        + _attn_scratch(DSA_HEADS, DSA_HEAD_DIM, tq, ch),
        compiler_params=pltpu.CompilerParams(
            dimension_semantics=("parallel", "arbitrary"), vmem_limit_bytes=VMEM_LIMIT),
        name="dsa",
    )(qb, kb, vbt, qi, ki, wi, kpos)


def _post_kernel(x_ref, oa_ref, ob_ref, gate_ref, wa_ref, wb_ref, wo_ref, g1_ref, b1_ref,
                 wfi_ref, wfd_ref, g2_ref, b2_ref, out_ref):
    ya = _dot(oa_ref[...], wa_ref[...])
    yb = _dot(ob_ref[...], wb_ref[...])
    mixed = (_sigmoid(gate_ref[:, :D_MODEL].astype(F32)) * ya
             + _sigmoid(gate_ref[:, D_MODEL:].astype(F32)) * yb)
    mix_out = _dot(mixed.astype(BF16), wo_ref[...])
    h1 = _layer_norm(ALPHA * x_ref[...] + mix_out, g1_ref[...], b1_ref[...])
    gu = _dot(h1.astype(BF16), wfi_ref[...])
    gate = gu[:, :D_FF]
    act = gate * _sigmoid(gate) * gu[:, D_FF:]
    f = _dot(act.astype(BF16), wfd_ref[...])
    out_ref[...] = _layer_norm(ALPHA * h1 + f, g2_ref[...], b2_ref[...])


def _post_call(x2, oa, ob, gates, weights, rows):
    n = x2.shape[0]
    row_spec = lambda wdt: pl.BlockSpec((rows, wdt), lambda i: (i, 0))
    return pl.pallas_call(
        _post_kernel,
        grid=(n // rows,),
        in_specs=[row_spec(D_MODEL), row_spec(oa.shape[1]), row_spec(ob.shape[1]),
                  row_spec(2 * D_MODEL)] + [_const_spec(a.shape) for a in weights],
        out_specs=row_spec(D_MODEL),
        out_shape=jax.ShapeDtypeStruct((n, D_MODEL), F32),
        compiler_params=pltpu.CompilerParams(
            dimension_semantics=("parallel",), vmem_limit_bytes=VMEM_LIMIT),
        name="post",
    )(x2, oa, ob, gates, *weights)


def _pad_cols(w, width):
    return jnp.pad(w, ((0, 0), (0, width - w.shape[1])))


def _prep_proj_weights(w_in, q_norm, kv_norm, w_uq, w_ukv):
    sizes = (MLA_Q_RANK, MLA_KV_RANK, MLA_ROPE, 512, 512, 512, IDX_HEADS * IDX_DIM, IDX_DIM,
             IDX_HEADS, D_MODEL, D_MODEL)
    parts, start = [], 0
    for n in sizes:
        parts.append(w_in[:, start:start + n])
        start += n
    w_cq, w_ckv, w_kr, w_qb, w_kb, w_vb, w_qi, w_ki, w_wi, w_ga, w_gb = parts
    half = MLA_ROPE // 2
    d = w_in.shape[0]

    def rope_group(first, second):
        return jnp.concatenate([jnp.zeros((d, MLA_NOPE), F32), first, second,
                                jnp.zeros((d, LANES - MLA_QK), F32)], axis=1)

    wc = jnp.concatenate([w_cq, w_ckv], axis=1)
    wkr = jnp.concatenate([rope_group(w_kr[:, :half], w_kr[:, half:]),
                           rope_group(w_kr[:, half:], w_kr[:, :half])], axis=1)
    wdsa = jnp.concatenate([w_qb * DSA_SCALE, w_kb], axis=1)
    widx = jnp.concatenate([w_qi, w_ki, w_ki, w_ki, w_ki, _pad_cols(w_wi, LANES)], axis=1)
    wgate = jnp.concatenate([w_ga, w_gb], axis=1)

    r = w_uq.shape[0]
    uq = w_uq.reshape(r, MLA_HEADS, MLA_QK)
    zq = lambda n: jnp.zeros((r, MLA_HEADS, n), F32)
    wuqm = jnp.concatenate([uq, zq(LANES - MLA_QK)], axis=2).reshape(r, MLA_HEADS * LANES)
    wuqs = jnp.concatenate([zq(MLA_NOPE), uq[:, :, MLA_NOPE + half:], uq[:, :, MLA_NOPE:MLA_NOPE + half],
                            zq(LANES - MLA_QK)], axis=2).reshape(r, MLA_HEADS * LANES)
    ukv = w_ukv.reshape(r, MLA_HEADS, MLA_NOPE + MLA_V)
    wuk = jnp.concatenate([ukv[:, :, :MLA_NOPE], zq(LANES - MLA_NOPE)], axis=2).reshape(r, MLA_HEADS * LANES)
    wuvt = ukv[:, :, MLA_NOPE:].reshape(r, MLA_HEADS * MLA_V).T

    bf = lambda a: a.astype(BF16)
    return (bf(wc), bf(wkr), bf(wdsa), bf(w_vb.T), bf(widx), bf(wgate), q_norm.reshape(1, -1),
            kv_norm.reshape(1, -1), bf(wuqm), bf(wuqs), bf(wuk), bf(wuvt))


def _rope_freq():
    half = MLA_ROPE // 2
    return (ROPE_BASE ** (-jnp.arange(half, dtype=F32) / half)).reshape(half, 1)


def kernel(x, positions, w_in, mla_q_norm, mla_kv_norm, mla_w_uq, mla_w_ukv, w_branch_a, w_branch_b,
           w_out, ln1_g, ln1_b, ffn_w_in, ffn_w_down, ln2_g, ln2_b):
    b, s, d = x.shape
    assert w_in.shape[0] == DEPTH and d == D_MODEL
    assert s & (s - 1) == 0 and s % ATT_TQ == 0 and s % (2 * ATT_CH) == 0 and PROJ_ROWS % ATT_CH == 0
    assert (b * s) % PROJ_ROWS == 0 and s <= POS_SPLIT * 256
    n = b * s
    h = x.reshape(n, d)
    pos2 = positions.reshape(n, 1)
    posr = positions.reshape(n // PROJ_ROWS, 1, PROJ_ROWS)
    freq = _rope_freq()
    for l in range(DEPTH):
        pw = _prep_proj_weights(w_in[l], mla_q_norm[l], mla_kv_norm[l], mla_w_uq[l], mla_w_ukv[l])
        qa, ka, vat, qb, kb, vbt, qi, ki, wi, kpos, gates = _proj_call(
            h, pos2, posr, freq, pw, PROJ_ROWS, ATT_CH)
        r3 = lambda a: a.reshape(b, s, a.shape[1])
        t4 = lambda a: a.reshape(b, s // ATT_CH, a.shape[1], ATT_CH)
        o_a = _mla_call(r3(qa), r3(ka), t4(vat), ATT_TQ, ATT_CH)
        o_b = _dsa_call(r3(qb), r3(kb), t4(vbt), r3(qi), r3(ki), r3(wi), r3(kpos), ATT_TQ, ATT_CH)
        row = lambda v: v.reshape(1, -1)
        post_w = (w_branch_a[l].astype(BF16), w_branch_b[l].astype(BF16), w_out[l].astype(BF16),
                  row(ln1_g[l]), row(ln1_b[l]), ffn_w_in[l].astype(BF16), ffn_w_down[l].astype(BF16),
                  row(ln2_g[l]), row(ln2_b[l]))
        h = _post_call(h, o_a.reshape(n, -1), o_b.reshape(n, -1), gates, post_w, POST_ROWS)
    return h.reshape(b, s, d)
```
